```python
import math
import jax
import jax.numpy as jnp
from jax import lax
import numpy as np

D_MODEL = 1024
BATCH = 8
SEQ = 2048
DEPTH = 2
DEC_BATCH = 32
DEC_SEQ = 1
PAST_LEN = 8192
PAGE_SIZE = 128

N_MIXERS = 2
N_SSM_LAYERS = (DEPTH + 1) // 2
N_ATTN_LAYERS = DEPTH // 2
EPS = 1e-6

SSM_EXPAND = 2
SSM_D_INNER = SSM_EXPAND * D_MODEL
SSM_HEAD_DIM = 64
SSM_HEADS = SSM_D_INNER // SSM_HEAD_DIM
SSM_GROUPS = 4
SSM_HEADS_PER_GROUP = SSM_HEADS // SSM_GROUPS
SSM_D_STATE = 128
CONV_WIDTH = 4
SSD_CHUNK = 128
SSM_CONV_DIM = SSM_D_INNER + 2 * SSM_GROUPS * SSM_D_STATE
SSM_IN_DIM = SSM_D_INNER + SSM_CONV_DIM + SSM_HEADS
DT_MIN = 0.001
DT_MAX = 0.1

ATTN_PATTERNS = ((128, 1), (512, 4), (2048, 16))
N_ATTN_GROUPS = len(ATTN_PATTERNS)
ATTN_HEAD_DIM = 64
ATTN_HEADS = D_MODEL // ATTN_HEAD_DIM
ATTN_WIDTH = ATTN_HEADS * ATTN_HEAD_DIM
ATTN_BLOCK = 128
ATTN_SCALE = ATTN_HEAD_DIM ** -0.5

N_EXPERT_GROUPS = 4
EXPERTS_PER_GROUP = 8
TOP_K_INNER = 2
D_EXPERT = 512

kernel_name = 'hybrid_ssd_dilated_swa_hmoe_step'


def rmsnorm(x, g):
    xf = x.astype(jnp.float32)
    y = xf * lax.rsqrt(jnp.mean(xf * xf, axis=-1, keepdims=True) + EPS)
    return (y * g.astype(jnp.float32)).astype(x.dtype)


def _pad_axis(a, axis, front, back):
    cfg = [(0, 0)] * a.ndim
    cfg[axis] = (front, back)
    return jnp.pad(a, cfg)


def ssd_scan(x, dt, a, b_in, c_in, h0):
    f32 = jnp.float32
    nb, L = x.shape[:2]
    q = min(SSD_CHUNK, L)
    nc = -(-L // q)
    pad = nc * q - L
    G, E, P, N = SSM_GROUPS, SSM_HEADS_PER_GROUP, SSM_HEAD_DIM, SSM_D_STATE
    x = _pad_axis(x.astype(f32), 1, 0, pad).reshape(nb, nc, q, G, E, P)
    dt = _pad_axis(dt, 1, 0, pad).reshape(nb, nc, q, G, E)
    bm = _pad_axis(b_in.astype(f32), 1, 0, pad).reshape(nb, nc, q, G, N)
    cm = _pad_axis(c_in.astype(f32), 1, 0, pad).reshape(nb, nc, q, G, N)
    cs = jnp.cumsum(dt * a.reshape(G, E), axis=2)
    xdt = x * dt[..., None]
    causal = jnp.tril(jnp.ones((q, q), dtype=bool))[None, None, :, :, None, None]
    seg = cs[:, :, :, None] - cs[:, :, None, :]
    decay = jnp.exp(jnp.where(causal, seg, -jnp.inf))
    cb = jnp.einsum('bcign,bcjgn->bcijg', cm, bm)
    y_diag = jnp.einsum('bcijg,bcijge,bcjgep->bcigep', cb, decay, xdt)
    to_end = jnp.exp(cs[:, :, -1:] - cs)
    states = jnp.einsum('bcjgn,bcjge,bcjgep->bcgepn', bm, to_end, xdt)
    chunk_decay = jnp.exp(cs[:, :, -1])

    def step(h, inp):
        s, dcy = inp
        return h * dcy[..., None, None] + s, h

    h_last, h_in = lax.scan(step, h0.astype(f32).reshape(nb, G, E, P, N),
                            (jnp.moveaxis(states, 1, 0), jnp.moveaxis(chunk_decay, 1, 0)))
    h_in = jnp.moveaxis(h_in, 0, 1)
    y_off = jnp.einsum('bcign,bcige,bcgepn->bcigep', cm, jnp.exp(cs), h_in)
    y = (y_diag + y_off).reshape(nb, nc * q, SSM_HEADS, P)[:, :L]
    return y, h_last.reshape(nb, SSM_HEADS, P, N)


def mamba2_mixer(h, conv_state, ssm_state, w_in, conv_w, conv_b, dt_bias, a_log, d_skip, norm_g, w_out):
    f32 = jnp.float32
    nb, L, _ = h.shape
    proj = h @ w_in
    z = proj[..., :SSM_D_INNER]
    xbc = proj[..., SSM_D_INNER:SSM_D_INNER + SSM_CONV_DIM]
    dt_raw = proj[..., SSM_D_INNER + SSM_CONV_DIM:]
    xpad = jnp.concatenate([conv_state.astype(xbc.dtype), xbc], axis=1)
    conv = lax.conv_general_dilated(xpad, conv_w[:, None, :].astype(xbc.dtype), window_strides=(1,),
                                    padding='VALID', dimension_numbers=('NWC', 'WIO', 'NWC'),
                                    feature_group_count=SSM_CONV_DIM)
    new_conv_state = xpad[:, xpad.shape[1] - (CONV_WIDTH - 1):]
    xbc = jax.nn.silu((conv + conv_b).astype(f32))
    nbc = SSM_GROUPS * SSM_D_STATE
    xs = xbc[..., :SSM_D_INNER].reshape(nb, L, SSM_HEADS, SSM_HEAD_DIM)
    bm = xbc[..., SSM_D_INNER:SSM_D_INNER + nbc].reshape(nb, L, SSM_GROUPS, SSM_D_STATE)
    cm = xbc[..., SSM_D_INNER + nbc:].reshape(nb, L, SSM_GROUPS, SSM_D_STATE)
    dt = jax.nn.softplus(dt_raw.astype(f32) + dt_bias.astype(f32))
    a = -jnp.exp(a_log.astype(f32))
    y, h_last = ssd_scan(xs, dt, a, bm, cm, ssm_state)
    y = y + xs * d_skip.astype(f32)[:, None]
    y = y.reshape(nb, L, SSM_D_INNER) * jax.nn.silu(z.astype(f32))
    y = y * lax.rsqrt(jnp.mean(y * y, axis=-1, keepdims=True) + EPS) * norm_g.astype(f32)
    return y.astype(h.dtype) @ w_out, new_conv_state, h_last.astype(ssm_state.dtype)


def to_strided(a, dil):
    b, L = a.shape[:2]
    return jnp.swapaxes(a.reshape((b, L // dil, dil) + a.shape[2:]), 1, 2)


def from_strided(a, dil):
    b, _, lq = a.shape[:3]
    return jnp.swapaxes(a, 1, 2).reshape((b, lq * dil) + a.shape[3:])


def band_attention(q, k, v, n_back):
    nb, r, L = q.shape[:3]
    qb_len = min(ATTN_BLOCK, L)
    n_blk = -(-L // qb_len)
    lp = n_blk * qb_len
    qp = _pad_axis(q, 2, 0, lp - L).reshape(nb, r, n_blk, qb_len, ATTN_HEADS, ATTN_HEAD_DIM)
    kp = _pad_axis(k, 2, n_back, lp - L)
    vp = _pad_axis(v, 2, n_back, lp - L)
    kidx = (jnp.arange(n_blk) * qb_len)[:, None] + jnp.arange(qb_len + n_back)[None, :]
    kb = kp[:, :, kidx]
    vb = vp[:, :, kidx]
    s = jnp.einsum('brnqhd,brnkhd->brnhqk', qp, kb, preferred_element_type=jnp.float32) * ATTN_SCALE
    dist = jnp.arange(qb_len)[:, None] + n_back - jnp.arange(qb_len + n_back)[None, :]
    key_pos = kidx - n_back
    mask = (dist >= 0)[None] & (dist <= n_back)[None] & (key_pos >= 0)[:, None, :]
    s = jnp.where(mask[None, None, :, None], s, -jnp.inf)
    m = jnp.max(s, axis=-1)
    p = jnp.exp(s - m[..., None])
    l = jnp.sum(p, axis=-1)
    o = jnp.einsum('brnhqk,brnkhd->brnqhd', p, vb.astype(jnp.float32))
    m = jnp.swapaxes(m, 3, 4)
    l = jnp.swapaxes(l, 3, 4)
    o = (o / l[..., None]).reshape(nb, r, lp, ATTN_HEADS, ATTN_HEAD_DIM)[:, :, :L]
    lam = (m + jnp.log(l)).reshape(nb, r, lp, ATTN_HEADS)[:, :, :L]
    return o, lam


def combine_groups(outs, lams):
    alpha = jax.nn.softmax(jnp.stack(lams), axis=0)
    return jnp.sum(alpha[..., None] * jnp.stack(outs), axis=0)


def dilated_attention_prompt(h, w_qkv, w_out):
    nb, L, _ = h.shape
    qkv = (h @ w_qkv).reshape(nb, L, N_ATTN_GROUPS, 3, ATTN_HEADS, ATTN_HEAD_DIM)
    outs, lams, rows = [], [], []
    for g, (window, dil) in enumerate(ATTN_PATTERNS):
        q, k, v = qkv[:, :, g, 0], qkv[:, :, g, 1], qkv[:, :, g, 2]
        o, lam = band_attention(to_strided(q, dil), to_strided(k, dil), to_strided(v, dil), window // dil)
        outs.append(from_strided(o, dil))
        lams.append(from_strided(lam, dil))
        keep = min(window, L)
        rows.append(jnp.stack([k, v], axis=2)[:, L - keep:])
    y = combine_groups(outs, lams).reshape(nb, L, ATTN_WIDTH)
    return y.astype(h.dtype) @ w_out, rows


def dilated_attention_sample(h, kv_bufs, w_qkv, w_out):
    nb, S, _ = h.shape
    qkv = (h @ w_qkv).reshape(nb, S, N_ATTN_GROUPS, 3, ATTN_HEADS, ATTN_HEAD_DIM)
    outs, lams, rows = [], [], []
    for g, (window, dil) in enumerate(ATTN_PATTERNS):
        q, k, v = qkv[:, :, g, 0], qkv[:, :, g, 1], qkv[:, :, g, 2]
        buf = kv_bufs[g]
        wl = buf.shape[1]
        k_all = jnp.concatenate([buf[:, :, 0].astype(k.dtype), k], axis=1)
        v_all = jnp.concatenate([buf[:, :, 1].astype(v.dtype), v], axis=1)
        n_back = window // dil
        idx = wl + jnp.arange(S)[:, None] - dil * jnp.arange(n_back + 1)[None, :]
        valid = idx >= 0
        idx = jnp.maximum(idx, 0)
        kg = k_all[:, idx]
        vg = v_all[:, idx]
        s = jnp.einsum('bshd,bskhd->bshk', q, kg, preferred_element_type=jnp.float32) * ATTN_SCALE
        s = jnp.where(valid[None, :, None, :], s, -jnp.inf)
        m = jnp.max(s, axis=-1)
        p = jnp.exp(s - m[..., None])
        l = jnp.sum(p, axis=-1)
        o = jnp.einsum('bshk,bskhd->bshd', p, vg.astype(jnp.float32)) / l[..., None]
        outs.append(o)
        lams.append(m + jnp.log(l))
        rows.append(jnp.stack([k, v], axis=2))
    y = combine_groups(outs, lams).reshape(nb, S, ATTN_WIDTH)
    return y.astype(h.dtype) @ w_out, rows


def hier_moe(t, w_route_group, w_route_expert, w_gate_up, w_down):
    f32 = jnp.float32
    g_logits = jnp.einsum('td,dg->tg', t, w_route_group, preferred_element_type=f32)
    g_prob = jax.nn.softmax(g_logits, axis=-1)
    grp = jnp.argmax(g_logits, axis=-1)
    g_gate = jnp.take_along_axis(g_prob, grp[:, None], axis=-1)
    e_logits_all = jnp.einsum('td,gde->tge', t, w_route_expert, preferred_element_type=f32)
    e_logits = jnp.take_along_axis(e_logits_all, grp[:, None, None], axis=1)[:, 0]
    e_prob = jax.nn.softmax(e_logits, axis=-1)
    top_v, top_i = lax.top_k(e_prob, TOP_K_INNER)
    top_v = top_v / jnp.sum(top_v, axis=-1, keepdims=True)
    inner = jnp.sum(jax.nn.one_hot(top_i, EXPERTS_PER_GROUP, dtype=f32) * top_v[..., None], axis=1)
    combine = jax.nn.one_hot(grp, N_EXPERT_GROUPS, dtype=f32)[:, :, None] * (g_gate * inner)[:, None, :]
    out = jnp.zeros(t.shape, f32)
    for g in range(N_EXPERT_GROUPS):
        gu = jnp.einsum('td,edf->tef', t, w_gate_up[g])
        act = jax.nn.silu(gu[..., :D_EXPERT].astype(f32)) * gu[..., D_EXPERT:].astype(f32) * combine[:, g, :, None]
        out = out + jnp.einsum('tef,efd->td', act.astype(t.dtype), w_down[g], preferred_element_type=f32)
    return out.astype(t.dtype)


def setup_inputs(seed: int = 0) -> dict:
    key = jax.random.key(seed)
    ks = jax.random.split(key, 24)
    f32 = jnp.float32

    def nrm(i, shape, scale):
        return scale * jax.random.normal(ks[i], shape, f32)

    kv_shape = lambda n_rows: (N_ATTN_LAYERS, DEC_BATCH, n_rows, 2, ATTN_HEADS, ATTN_HEAD_DIM)
    wl = [min(w, PAST_LEN) for w, _ in ATTN_PATTERNS]
    u = jax.random.uniform(ks[13], (N_SSM_LAYERS, SSM_HEADS), f32)
    dt0 = jnp.exp(u * (math.log(DT_MAX) - math.log(DT_MIN)) + math.log(DT_MIN))
    dt_bias = dt0 + jnp.log(-jnp.expm1(-dt0))
    a_log = jnp.log(jax.random.uniform(ks[14], (N_SSM_LAYERS, SSM_HEADS), f32, minval=1.0, maxval=16.0))
    return {
        'x_prompt': nrm(0, (BATCH, SEQ, D_MODEL), 1.0),
        'x_sample': nrm(1, (DEC_BATCH, DEC_SEQ, D_MODEL), 1.0),
        'state_conv': nrm(2, (N_SSM_LAYERS, DEC_BATCH, CONV_WIDTH - 1, SSM_CONV_DIM), 1.0),
        'state_ssm': nrm(3, (N_SSM_LAYERS, DEC_BATCH, SSM_HEADS, SSM_HEAD_DIM, SSM_D_STATE), 0.1),
        'cache_kv_w128': nrm(4, kv_shape(wl[0]), 1.0),
        'cache_kv_w512': nrm(5, kv_shape(wl[1]), 1.0),
        'cache_kv_w2048': nrm(6, kv_shape(wl[2]), 1.0),
        'norm_mix': 1.0 + nrm(7, (DEPTH, D_MODEL), 0.1),
        'norm_ffn': 1.0 + nrm(8, (DEPTH, D_MODEL), 0.1),
        'norm_final': 1.0 + nrm(9, (D_MODEL,), 0.1),
        'ssm_in_proj': nrm(10, (N_SSM_LAYERS, D_MODEL, SSM_IN_DIM), D_MODEL ** -0.5),
        'ssm_conv_w': nrm(11, (N_SSM_LAYERS, CONV_WIDTH, SSM_CONV_DIM), CONV_WIDTH ** -0.5),
        'ssm_conv_b': nrm(12, (N_SSM_LAYERS, SSM_CONV_DIM), 0.1),
        'ssm_dt_bias': dt_bias,
        'ssm_a_log': a_log,
        'ssm_d': 1.0 + nrm(15, (N_SSM_LAYERS, SSM_HEADS), 0.1),
        'ssm_norm': 1.0 + nrm(16, (N_SSM_LAYERS, SSM_D_INNER), 0.1),
        'ssm_out_proj': nrm(17, (N_SSM_LAYERS, SSM_D_INNER, D_MODEL), SSM_D_INNER ** -0.5),
        'attn_qkv': nrm(18, (N_ATTN_LAYERS, D_MODEL, N_ATTN_GROUPS * 3 * ATTN_WIDTH), D_MODEL ** -0.5),
        'attn_out': nrm(19, (N_ATTN_LAYERS, ATTN_WIDTH, D_MODEL), ATTN_WIDTH ** -0.5),
        'moe_route_group': nrm(20, (DEPTH, D_MODEL, N_EXPERT_GROUPS), D_MODEL ** -0.5),
        'moe_route_expert': nrm(21, (DEPTH, N_EXPERT_GROUPS, D_MODEL, EXPERTS_PER_GROUP), D_MODEL ** -0.5),
        'moe_w_gate_up': nrm(22, (DEPTH, N_EXPERT_GROUPS, EXPERTS_PER_GROUP, D_MODEL, 2 * D_EXPERT), D_MODEL ** -0.5),
        'moe_w_down': nrm(23, (DEPTH, N_EXPERT_GROUPS, EXPERTS_PER_GROUP, D_EXPERT, D_MODEL), D_EXPERT ** -0.5),
    }


def reference(x_prompt, x_sample, state_conv, state_ssm, cache_kv_w128, cache_kv_w512, cache_kv_w2048,
              norm_mix, norm_ffn, norm_final, ssm_in_proj, ssm_conv_w, ssm_conv_b, ssm_dt_bias, ssm_a_log,
              ssm_d, ssm_norm, ssm_out_proj, attn_qkv, attn_out, moe_route_group, moe_route_expert,
              moe_w_gate_up, moe_w_down):
    kv_caches = (cache_kv_w128, cache_kv_w512, cache_kv_w2048)
    xp, xs = x_prompt, x_sample
    conv_p, conv_s, ssm_p, ssm_s = [], [], [], []
    kv_p = [[] for _ in ATTN_PATTERNS]
    kv_s = [[] for _ in ATTN_PATTERNS]
    for i in range(DEPTH):
        j = i // N_MIXERS
        hp = rmsnorm(xp, norm_mix[i])
        hs = rmsnorm(xs, norm_mix[i])
        if i % N_MIXERS == 0:
            params = (ssm_in_proj[j], ssm_conv_w[j], ssm_conv_b[j], ssm_dt_bias[j], ssm_a_log[j],
                      ssm_d[j], ssm_norm[j], ssm_out_proj[j])
            zero_conv = jnp.zeros((xp.shape[0], CONV_WIDTH - 1, SSM_CONV_DIM), xp.dtype)
            zero_ssm = jnp.zeros((xp.shape[0], SSM_HEADS, SSM_HEAD_DIM, SSM_D_STATE), xp.dtype)
            yp, c_new, s_new = mamba2_mixer(hp, zero_conv, zero_ssm, *params)
            conv_p.append(c_new)
            ssm_p.append(s_new)
            ys, c_new, s_new = mamba2_mixer(hs, state_conv[j], state_ssm[j], *params)
            conv_s.append(c_new)
            ssm_s.append(s_new)
        else:
            yp, rows_p = dilated_attention_prompt(hp, attn_qkv[j], attn_out[j])
            ys, rows_s = dilated_attention_sample(hs, [c[j] for c in kv_caches], attn_qkv[j], attn_out[j])
            for g in range(N_ATTN_GROUPS):
                kv_p[g].append(rows_p[g])
                kv_s[g].append(rows_s[g])
        xp = xp + yp
        xs = xs + ys
        moe_params = (moe_route_group[i], moe_route_expert[i], moe_w_gate_up[i], moe_w_down[i])
        xp = xp + hier_moe(rmsnorm(xp, norm_ffn[i]).reshape(-1, D_MODEL), *moe_params).reshape(xp.shape)
        xs = xs + hier_moe(rmsnorm(xs, norm_ffn[i]).reshape(-1, D_MODEL), *moe_params).reshape(xs.shape)
    y_prompt = rmsnorm(xp, norm_final)
    y_sample = rmsnorm(xs, norm_final)
    return (y_prompt, y_sample, jnp.stack(conv_p), jnp.stack(conv_s), jnp.stack(ssm_p), jnp.stack(ssm_s),
            jnp.stack(kv_p[0]), jnp.stack(kv_s[0]), jnp.stack(kv_p[1]), jnp.stack(kv_s[1]),
            jnp.stack(kv_p[2]), jnp.stack(kv_s[2]))
```

```python
import functools
import math

import jax
import jax.numpy as jnp
from jax import lax
from jax.experimental import pallas as pl
from jax.experimental.pallas import tpu as pltpu

F32 = jnp.float32
BF16 = jnp.bfloat16
I32 = jnp.int32

EPS = 1e-6
D_MODEL = 1024
LANES = 128
SUBLANES = 8

SSM_D_INNER = 2048
SSM_HEAD_DIM = 64
SSM_HEADS = 32
SSM_GROUPS = 4
SSM_HEADS_PER_GROUP = 8
SSM_D_STATE = 128
CONV_WIDTH = 4
SSD_CHUNK = 128
SSM_BC_DIM = 2 * SSM_GROUPS * SSM_D_STATE
SSM_CONV_DIM = SSM_D_INNER + SSM_BC_DIM

ATTN_PATTERNS = ((128, 1), (512, 4), (2048, 16))
ATTN_HEAD_DIM = 64
ATTN_HEADS = 16
ATTN_WIDTH = 1024
ATTN_BLOCK = 128
ATTN_SCALE = ATTN_HEAD_DIM ** -0.5

N_EXPERT_GROUPS = 4
EXPERTS_PER_GROUP = 8
N_EXPERTS = N_EXPERT_GROUPS * EXPERTS_PER_GROUP
D_EXPERT = 512
GROUP_LOGIT_LANE = N_EXPERTS

VMEM_LIMIT = 56 * 1024 * 1024


def _cparams(sem):
    return pltpu.CompilerParams(dimension_semantics=sem, vmem_limit_bytes=VMEM_LIMIT)


def _rms(x, g):
    ms = jnp.mean(x * x, axis=-1, keepdims=True)
    return x * lax.rsqrt(ms + EPS) * g


def _silu(x):
    return x * jax.nn.sigmoid(x)


def _softplus(x):
    return jnp.maximum(x, 0.0) + jnp.log1p(jnp.exp(-jnp.abs(x)))


def _mm(a, b):
    prec = lax.Precision.HIGHEST if b.dtype == F32 else None
    return jnp.dot(a.astype(b.dtype), b, preferred_element_type=F32, precision=prec)


def _norm_matmul_kernel(x_ref, g_ref, w_ref, o_ref, xn_ref):
    @pl.when(pl.program_id(1) == 0)
    def _():
        xn_ref[...] = _rms(x_ref[...], g_ref[...]).astype(xn_ref.dtype)

    o_ref[...] = _mm(xn_ref[...], w_ref[...])


def _norm_matmul2_kernel(x_ref, g_ref, w_ref, w2_ref, o_ref, o2_ref, xn_ref):
    @pl.when(pl.program_id(1) == 0)
    def _():
        xn = _rms(x_ref[...], g_ref[...]).astype(xn_ref.dtype)
        xn_ref[...] = xn
        o2_ref[...] = _mm(xn, w2_ref[...])

    o_ref[...] = _mm(xn_ref[...], w_ref[...])


def norm_matmul(x, g, w, w2=None, *, tm, tn):
    t, d = x.shape
    n = w.shape[1]
    grid = (t // tm, n // tn)
    x_spec = pl.BlockSpec((tm, d), lambda i, j: (i, 0))
    g_spec = pl.BlockSpec((1, d), lambda i, j: (0, 0))
    w_spec = pl.BlockSpec((d, tn), lambda i, j: (0, j))
    o_spec = pl.BlockSpec((tm, tn), lambda i, j: (i, j))
    scratch = [pltpu.VMEM((tm, d), w.dtype)]
    if w2 is None:
        return pl.pallas_call(
            _norm_matmul_kernel, grid=grid,
            in_specs=[x_spec, g_spec, w_spec], out_specs=o_spec,
            out_shape=jax.ShapeDtypeStruct((t, n), F32),
            scratch_shapes=scratch,
            compiler_params=_cparams(("parallel", "arbitrary")),
            name="norm_matmul")(x, g, w)
    n2 = w2.shape[1]
    return pl.pallas_call(
        _norm_matmul2_kernel, grid=grid,
        in_specs=[x_spec, g_spec, w_spec, pl.BlockSpec((d, n2), lambda i, j: (0, 0))],
        out_specs=[o_spec, pl.BlockSpec((tm, n2), lambda i, j: (i, 0))],
        out_shape=[jax.ShapeDtypeStruct((t, n), F32), jax.ShapeDtypeStruct((t, n2), F32)],
        scratch_shapes=scratch,
        compiler_params=_cparams(("parallel", "arbitrary")),
        name="norm_matmul2")(x, g, w, w2)


def _matmul_res_kernel(a_ref, w_ref, r_ref, o_ref):
    o_ref[...] = r_ref[...] + _mm(a_ref[...], w_ref[...])


def matmul_res(a, w, res, *, tm):
    t, k = a.shape
    n = w.shape[1]
    return pl.pallas_call(
        _matmul_res_kernel, grid=(t // tm,),
        in_specs=[pl.BlockSpec((tm, k), lambda i: (i, 0)),
                  pl.BlockSpec((k, n), lambda i: (0, 0)),
                  pl.BlockSpec((tm, n), lambda i: (i, 0))],
        out_specs=pl.BlockSpec((tm, n), lambda i: (i, 0)),
        out_shape=jax.ShapeDtypeStruct((t, n), F32),
        compiler_params=_cparams(("parallel",)),
        name="matmul_res")(a, w, res)


_CONV_COLS = 512


def _ssd_kernel(z_ref, xs_ref, bc_ref, dt_ref, cw_ref, cb_ref, dtb_ref, alog_ref,
                dsk_ref, ng_ref,
                y_ref, tail_out_ref, st_ref,
                tail_ref, h_ref, xc_ref, xst_ref, yt_ref, dtt_ref, cst_ref, xde_ref, yn_ref):
    q = SSD_CHUNK
    c = pl.program_id(1)

    @pl.when(c == 0)
    def _():
        tail_ref[...] = jnp.zeros_like(tail_ref)
        h_ref[...] = jnp.zeros_like(h_ref)

    row = lax.broadcasted_iota(I32, (q, _CONV_COLS), 0)
    for k in range(SSM_CONV_DIM // _CONV_COLS):
        lo = k * _CONV_COLS
        if lo < SSM_D_INNER:
            src = xs_ref[0, :, lo:lo + _CONV_COLS]
        else:
            src = bc_ref[0, :, lo - SSM_D_INNER:lo - SSM_D_INNER + _CONV_COLS]
        tl = tail_ref[:, lo:lo + _CONV_COLS]
        w = cw_ref[:, lo:lo + _CONV_COLS]
        acc = src * w[CONV_WIDTH - 1:CONV_WIDTH] + cb_ref[:, lo:lo + _CONV_COLS]
        for s in range(1, CONV_WIDTH):
            cur = pltpu.roll(src, s, axis=0)
            prev = jnp.tile(pltpu.roll(tl, s, axis=0), (q // SUBLANES, 1))
            shifted = jnp.where(row < s, prev, cur)
            acc = acc + shifted * w[CONV_WIDTH - 1 - s:CONV_WIDTH - s]
        xc_ref[:, lo:lo + _CONV_COLS] = _silu(acc)
        tail_ref[:, lo:lo + _CONV_COLS] = src[q - SUBLANES:, :]

    for k in range(SSM_D_INNER // LANES):
        xst_ref[k * LANES:(k + 1) * LANES, :] = xc_ref[:, k * LANES:(k + 1) * LANES].T

    dt = _softplus(dt_ref[0] + dtb_ref[...])
    da = dt * (-jnp.exp(alog_ref[...]))
    ri = lax.broadcasted_iota(I32, (q, q), 0)
    ci = lax.broadcasted_iota(I32, (q, q), 1)
    tril = (ci <= ri).astype(F32)
    cs = jnp.dot(tril, da, preferred_element_type=F32, precision=lax.Precision.HIGHEST)
    dtt_ref[...] = dt.T
    cst_ref[...] = cs.T

    causal_t = ci >= ri
    hpg = SSM_HEADS_PER_GROUP
    gp = hpg * SSM_HEAD_DIM
    for g in range(SSM_GROUPS):
        b0 = SSM_D_INNER + g * SSM_D_STATE
        c0 = SSM_D_INNER + SSM_GROUPS * SSM_D_STATE + g * SSM_D_STATE
        bm = xc_ref[:, b0:b0 + SSM_D_STATE].astype(BF16)
        cm = xc_ref[:, c0:c0 + SSM_D_STATE].astype(BF16)
        cbt = lax.dot_general(bm, cm, (((1,), (1,)), ((), ())), preferred_element_type=F32)
        hg = h_ref[g * gp:(g + 1) * gp, :].astype(BF16)
        yt_ref[g * gp:(g + 1) * gp, :] = lax.dot_general(
            hg, cm, (((1,), (1,)), ((), ())), preferred_element_type=F32)

        for e in range(hpg):
            hd = g * hpg + e
            rows = slice(hd * SSM_HEAD_DIM, (hd + 1) * SSM_HEAD_DIM)
            csr = cst_ref[hd:hd + 1, :]
            rowb = jnp.broadcast_to(csr, (q, q))
            colb = rowb.T
            dec = jnp.exp(jnp.where(causal_t, rowb - colb, -jnp.inf))
            mt = (cbt * dec).astype(BF16)
            dtr = dtt_ref[hd:hd + 1, :]
            xsh = xst_ref[rows, :]
            xdt = xsh * dtr
            ydt = jnp.dot(xdt.astype(BF16), mt, preferred_element_type=F32)
            cs_end = csr[:, q - 1:q]
            yt_ref[rows, :] = yt_ref[rows, :] * jnp.exp(csr) + ydt + xsh * dsk_ref[rows, :]
            xde_ref[e * SSM_HEAD_DIM:(e + 1) * SSM_HEAD_DIM, :] = (
                xdt * jnp.exp(cs_end - csr)).astype(BF16)
            h_ref[rows, :] = h_ref[rows, :] * jnp.exp(cs_end)
        h_ref[g * gp:(g + 1) * gp, :] = h_ref[g * gp:(g + 1) * gp, :] + jnp.dot(
            xde_ref[...], bm, preferred_element_type=F32)

    ssq = jnp.zeros((q, 1), F32)
    for k in range(SSM_D_INNER // LANES):
        yk = yt_ref[k * LANES:(k + 1) * LANES, :].T * _silu(z_ref[0, :, k * LANES:(k + 1) * LANES])
        yn_ref[:, k * LANES:(k + 1) * LANES] = yk
        ssq = ssq + jnp.sum(yk * yk, axis=-1, keepdims=True)
    scale = lax.rsqrt(ssq * (1.0 / SSM_D_INNER) + EPS)
    y_ref[0] = (yn_ref[...] * scale * ng_ref[...]).astype(BF16)

    @pl.when(c == pl.num_programs(1) - 1)
    def _():
        tail_out_ref[0] = tail_ref[...]
        st_ref[0] = h_ref[...]


def ssd_prompt(proj, dt_raw, conv_w, conv_b, dt_bias, a_log, d_rows, norm_g, nb, seq):
    nc = seq // SSD_CHUNK
    q = SSD_CHUNK
    const2 = lambda b, c: (0, 0)
    return pl.pallas_call(
        _ssd_kernel, grid=(nb, nc),
        in_specs=[
            pl.BlockSpec((1, q, SSM_D_INNER), lambda b, c: (b, c, 0)),
            pl.BlockSpec((1, q, SSM_D_INNER), lambda b, c: (b, c, 1)),
            pl.BlockSpec((1, q, SSM_BC_DIM), lambda b, c: (b, c, 2 * SSM_D_INNER // SSM_BC_DIM)),
            pl.BlockSpec((1, q, LANES), lambda b, c: (b, c, 0)),
            pl.BlockSpec((CONV_WIDTH, SSM_CONV_DIM), const2),
            pl.BlockSpec((1, SSM_CONV_DIM), const2),
            pl.BlockSpec((1, LANES), const2),
            pl.BlockSpec((1, LANES), const2),
            pl.BlockSpec((SSM_D_INNER, LANES), const2),
            pl.BlockSpec((1, SSM_D_INNER), const2),
        ],
        out_specs=[
            pl.BlockSpec((1, q, SSM_D_INNER), lambda b, c: (b, c, 0)),
            pl.BlockSpec((1, SUBLANES, SSM_CONV_DIM), lambda b, c: (b, 0, 0)),
            pl.BlockSpec((1, SSM_D_INNER, SSM_D_STATE), lambda b, c: (b, 0, 0)),
        ],
        out_shape=[
            jax.ShapeDtypeStruct((nb, seq, SSM_D_INNER), BF16),
            jax.ShapeDtypeStruct((nb, SUBLANES, SSM_CONV_DIM), F32),
            jax.ShapeDtypeStruct((nb, SSM_D_INNER, SSM_D_STATE), F32),
        ],
        scratch_shapes=[
            pltpu.VMEM((SUBLANES, SSM_CONV_DIM), F32),
            pltpu.VMEM((SSM_D_INNER, SSM_D_STATE), F32),
            pltpu.VMEM((q, SSM_CONV_DIM), F32),
            pltpu.VMEM((SSM_D_INNER, q), F32),
            pltpu.VMEM((SSM_D_INNER, q), F32),
            pltpu.VMEM((LANES, q), F32),
            pltpu.VMEM((LANES, q), F32),
            pltpu.VMEM((SSM_HEADS_PER_GROUP * SSM_HEAD_DIM, q), BF16),
            pltpu.VMEM((q, SSM_D_INNER), F32),
        ],
        compiler_params=_cparams(("parallel", "arbitrary")),
        name="ssd_prompt")(proj, proj, proj, dt_raw, conv_w, conv_b, dt_bias, a_log, d_rows, norm_g)


def _pad_lanes(v, n=LANES):
    return jnp.pad(v.astype(F32), (0, n - v.shape[0])).reshape(1, n)


def _ssm_weights(w_in, dtype):
    n_main = SSM_D_INNER + SSM_CONV_DIM
    w_main = w_in[:, :n_main].astype(dtype)
    w_dt = jnp.pad(w_in[:, n_main:], ((0, 0), (0, LANES - SSM_HEADS))).astype(dtype)
    return w_main, w_dt


def _mamba_prompt(x, norm_g, w_in, conv_w, conv_b, dt_bias, a_log, d_skip, ssm_norm, w_out):
    nb, seq, d = x.shape
    t = nb * seq
    w_main, w_dt = _ssm_weights(w_in, BF16)
    tm = min(t, 1024)
    proj, dt_raw = norm_matmul(x.reshape(t, d), norm_g.reshape(1, d), w_main, w_dt, tm=tm, tn=1024)
    d_rows = jnp.broadcast_to(jnp.repeat(d_skip.astype(F32), SSM_HEAD_DIM)[:, None], (SSM_D_INNER, LANES))
    y, tail, st = ssd_prompt(
        proj.reshape(nb, seq, -1), dt_raw.reshape(nb, seq, LANES),
        conv_w, conv_b.reshape(1, -1), _pad_lanes(dt_bias), _pad_lanes(a_log), d_rows,
        ssm_norm.reshape(1, -1), nb, seq)
    x_new = matmul_res(y.reshape(t, SSM_D_INNER), w_out.astype(BF16), x.reshape(t, d), tm=tm)
    conv_state = tail[:, SUBLANES - (CONV_WIDTH - 1):]
    ssm_state = st.reshape(nb, SSM_HEADS, SSM_HEAD_DIM, SSM_D_STATE)
    return x_new.reshape(nb, seq, d), conv_state, ssm_state


_DEC_HEADS = 4


def _ssd_decode_kernel(zt_ref, xbct_ref, convt_ref, dtt_raw_ref, cwt_ref, cbt_ref, dtbt_ref,
                       alogt_ref, dsk_ref, ngt_ref, wout_ref, res_ref, st_ref,
                       so_ref, out_ref,
                       xct_ref, dtt_ref, dect_ref, bcn_ref, yt_ref, *, nb):
    step = pl.program_id(0)
    rb = 512

    @pl.when(step == 0)
    def _():
        for k in range(SSM_CONV_DIM // rb):
            rows = slice(k * rb, (k + 1) * rb)
            acc = xbct_ref[rows, :] * cwt_ref[rows, CONV_WIDTH - 1:CONV_WIDTH] + cbt_ref[rows, 0:1]
            for j in range(CONV_WIDTH - 1):
                acc = acc + convt_ref[j, rows, :] * cwt_ref[rows, j:j + 1]
            xct_ref[rows, :] = _silu(acc)
        dt = _softplus(dtt_raw_ref[...] + dtbt_ref[:, 0:1])
        dtt_ref[...] = dt
        dect_ref[...] = jnp.exp(dt * (-jnp.exp(alogt_ref[:, 0:1])))
        for k in range(SSM_BC_DIM // LANES):
            bcn_ref[:, k * LANES:(k + 1) * LANES] = xct_ref[
                SSM_D_INNER + k * LANES:SSM_D_INNER + (k + 1) * LANES, :].T
        yt_ref[...] = jnp.zeros_like(yt_ref)

    for hh in range(_DEC_HEADS):
        hd = step * _DEC_HEADS + hh
        r0 = pl.multiple_of(hd * SSM_HEAD_DIM, SSM_HEAD_DIM)
        grp = hd // SSM_HEADS_PER_GROUP
        b_lane = pl.multiple_of(grp * SSM_D_STATE, SSM_D_STATE)
        c_lane = pl.multiple_of(SSM_GROUPS * SSM_D_STATE + grp * SSM_D_STATE, SSM_D_STATE)
        xh = xct_ref[pl.ds(r0, SSM_HEAD_DIM), :]
        dth = dtt_ref[pl.ds(hd, 1), :]
        dech = dect_ref[pl.ds(hd, 1), :]
        xdt = xh * dth
        for b in range(nb):
            brow = bcn_ref[b:b + 1, pl.ds(b_lane, SSM_D_STATE)]
            crow = bcn_ref[b:b + 1, pl.ds(c_lane, SSM_D_STATE)]
            hn = st_ref[b, hh] * dech[:, b:b + 1] + xdt[:, b:b + 1] * brow
            so_ref[b, hh] = hn
            yt_ref[pl.ds(r0, SSM_HEAD_DIM), b:b + 1] = jnp.sum(hn * crow, axis=-1, keepdims=True)

    @pl.when(step == pl.num_programs(0) - 1)
    def _():
        ssq = jnp.zeros((1, LANES), F32)
        for k in range(SSM_D_INNER // rb):
            rows = slice(k * rb, (k + 1) * rb)
            y = (yt_ref[rows, :] + xct_ref[rows, :] * dsk_ref[rows, :]) * _silu(zt_ref[rows, :])
            yt_ref[rows, :] = y
            ssq = ssq + jnp.sum(y * y, axis=0, keepdims=True)
        scale = lax.rsqrt(ssq * (1.0 / SSM_D_INNER) + EPS)
        acc = res_ref[...]
        for k in range(SSM_D_INNER // LANES):
            rows = slice(k * LANES, (k + 1) * LANES)
            yn = (yt_ref[rows, :] * scale * ngt_ref[rows, :]).T
            acc = acc + _mm(yn, wout_ref[rows, :])
        out_ref[...] = acc


def ssd_decode(zt, xbct, convt, dtt_raw, cwt, cbt, dtbt, alogt, d_rows, ngt, w_out, res, state):
    nb = state.shape[0]
    n_steps = SSM_HEADS // _DEC_HEADS
    full = lambda a: pl.BlockSpec(a.shape, lambda s, n=a.ndim: (0,) * n)
    st_spec = pl.BlockSpec((nb, _DEC_HEADS, SSM_HEAD_DIM, SSM_D_STATE), lambda s: (0, s, 0, 0))
    small = (zt, xbct, convt, dtt_raw, cwt, cbt, dtbt, alogt, d_rows, ngt, w_out, res)
    return pl.pallas_call(
        functools.partial(_ssd_decode_kernel, nb=nb), grid=(n_steps,),
        in_specs=[full(a) for a in small] + [st_spec],
        out_specs=[st_spec, full(res)],
        out_shape=[jax.ShapeDtypeStruct(state.shape, F32), jax.ShapeDtypeStruct(res.shape, F32)],
        scratch_shapes=[
            pltpu.VMEM((SSM_CONV_DIM, LANES), F32),
            pltpu.VMEM((LANES, LANES), F32),
            pltpu.VMEM((LANES, LANES), F32),
            pltpu.VMEM((LANES, SSM_BC_DIM), F32),
            pltpu.VMEM((SSM_D_INNER, LANES), F32),
        ],
        compiler_params=_cparams(("arbitrary",)),
        name="ssd_decode")(*small, state)


def _to_lanes(a, n=LANES):
    return jnp.pad(a.T, ((0, 0), (0, n - a.shape[0])))


def _mamba_sample(x, norm_g, conv_state, ssm_state, w_in, conv_w, conv_b, dt_bias, a_log, d_skip,
                  ssm_norm, w_out):
    nb, _, d = x.shape
    assert nb <= LANES and nb % SUBLANES == 0
    w_main, w_dt = _ssm_weights(w_in, BF16)
    proj, dt_raw = norm_matmul(x.reshape(nb, d), norm_g.reshape(1, d), w_main, w_dt, tm=nb, tn=1024)
    xbc = proj[:, SSM_D_INNER:]
    col = lambda v: jnp.broadcast_to(v.astype(F32)[:, None], (v.shape[0], LANES))
    convt = jnp.pad(jnp.transpose(conv_state, (1, 2, 0)), ((0, 0), (0, 0), (0, LANES - nb)))
    res = jnp.pad(x.reshape(nb, d), ((0, LANES - nb), (0, 0)))
    st, out = ssd_decode(
        _to_lanes(proj[:, :SSM_D_INNER]), _to_lanes(xbc), convt, _to_lanes(dt_raw),
        jnp.pad(conv_w.T, ((0, 0), (0, LANES - CONV_WIDTH))), col(conv_b),
        col(_pad_lanes(dt_bias)[0]), col(_pad_lanes(a_log)[0]),
        col(jnp.repeat(d_skip, SSM_HEAD_DIM)), col(ssm_norm), w_out.astype(BF16), res, ssm_state)
    new_conv = jnp.concatenate([conv_state[:, 1:], xbc[:, None, :]], axis=1)
    return out[:nb].reshape(nb, 1, d), new_conv, st


def _attn_block(q_ref, k_ref, v_ref, o_scr, lam_scr, start, dil, has_prev):
    nq = ATTN_BLOCK
    rows = pl.ds(start, nq, stride=dil) if dil > 1 else pl.ds(start, nq)
    hd = ATTN_HEAD_DIM
    q2 = q_ref[rows, :]
    lane = lax.broadcasted_iota(I32, (nq, LANES), 1)
    qblk = jnp.concatenate([jnp.where(lane < hd, q2, 0.0), jnp.where(lane >= hd, q2, 0.0)],
                           axis=0).astype(BF16)
    kj = lax.broadcasted_iota(I32, (nq, 2 * nq), 0)
    qi = lax.broadcasted_iota(I32, (nq, 2 * nq), 1) % nq
    nt = (((1,), (1,)), ((), ()))
    s = lax.dot_general(k_ref[rows, :].astype(BF16), qblk, nt,
                        preferred_element_type=F32) * ATTN_SCALE
    s = jnp.where(kj <= qi, s, -jnp.inf)
    vt = v_ref[rows, :].T.astype(BF16)
    if has_prev:
        prows = (pl.ds(start - dil * nq, nq, stride=dil) if dil > 1 else pl.ds(start - nq, nq))
        sp = lax.dot_general(k_ref[prows, :].astype(BF16), qblk, nt,
                             preferred_element_type=F32) * ATTN_SCALE
        s = jnp.concatenate([jnp.where(kj >= qi, sp, -jnp.inf), s], axis=0)
        vt = jnp.concatenate([v_ref[prows, :].T.astype(BF16), vt], axis=1)
    m = jnp.max(s, axis=0, keepdims=True)
    p = jnp.exp(s - m)
    l = jnp.sum(p, axis=0, keepdims=True)
    ot = jnp.dot(vt, p.astype(BF16), preferred_element_type=F32) * (1.0 / l)
    lam = jnp.broadcast_to(m + jnp.log(l), (hd, 2 * nq))
    o_scr[rows, :] = jnp.concatenate([ot[:hd, :nq], ot[hd:, nq:]], axis=0).T
    lam_scr[rows, :] = jnp.concatenate([lam[:, :nq], lam[:, nq:]], axis=0).T


def _attn_prompt_kernel(*refs, seq):
    qkv = refs[:9]
    y_ref = refs[9]
    o_scrs, lam_scrs = refs[10:13], refs[13:16]
    nq = ATTN_BLOCK
    for g, (window, dil) in enumerate(ATTN_PATTERNS):
        q_ref, k_ref, v_ref = qkv[3 * g:3 * g + 3]
        o_scr, lam_scr = o_scrs[g], lam_scrs[g]
        n_blk = seq // (dil * nq)
        for r in range(dil):
            for n in range(n_blk):
                _attn_block(q_ref, k_ref, v_ref, o_scr, lam_scr, r + dil * nq * n, dil, n > 0)

    rows_per = 256

    def merge(i, carry):
        rows = pl.ds(pl.multiple_of(i * rows_per, rows_per), rows_per)
        l0, l1, l2 = lam_scrs[0][rows, :], lam_scrs[1][rows, :], lam_scrs[2][rows, :]
        mx = jnp.maximum(jnp.maximum(l0, l1), l2)
        e0, e1, e2 = jnp.exp(l0 - mx), jnp.exp(l1 - mx), jnp.exp(l2 - mx)
        num = e0 * o_scrs[0][rows, :] + e1 * o_scrs[1][rows, :] + e2 * o_scrs[2][rows, :]
        y_ref[rows, :] = (num / (e0 + e1 + e2)).astype(BF16)
        return carry

    lax.fori_loop(0, seq // rows_per, merge, 0)


def attn_prompt(qkv, nb, seq):
    for window, dil in ATTN_PATTERNS:
        assert window // dil == ATTN_BLOCK and seq % (dil * ATTN_BLOCK) == 0
    n_pairs = ATTN_WIDTH // LANES
    in_specs = []
    for g in range(len(ATTN_PATTERNS)):
        for j in range(3):
            in_specs.append(pl.BlockSpec(
                (None, seq, LANES), lambda b, hp, g=g, j=j: (b, 0, (g * 3 + j) * n_pairs + hp)))
    return pl.pallas_call(
        functools.partial(_attn_prompt_kernel, seq=seq),
        grid=(nb, n_pairs),
        in_specs=in_specs,
        out_specs=pl.BlockSpec((None, seq, LANES), lambda b, hp: (b, 0, hp)),
        out_shape=jax.ShapeDtypeStruct((nb, seq, ATTN_WIDTH), BF16),
        scratch_shapes=[pltpu.VMEM((seq, LANES), F32)] * 6,
        compiler_params=_cparams(("parallel", "parallel")),
        name="attn_prompt")(*([qkv] * 9))


def _kv_rows_kernel(k_ref, v_ref, o_ref):
    tl = k_ref.shape[0]
    for kv, ref in enumerate((k_ref, v_ref)):
        for h in range(ATTN_HEADS):
            o_ref[pl.ds(kv * ATTN_HEADS + h, tl, stride=2 * ATTN_HEADS), :] = (
                ref[:, h * ATTN_HEAD_DIM:(h + 1) * ATTN_HEAD_DIM])


def kv_rows(qkv2d, g, nb, seq, keep, *, tl=128):
    per_tok = 2 * ATTN_HEADS
    first = (seq - keep) // tl
    row_blk = lambda b, i: b * (seq // tl) + first + i
    out = pl.pallas_call(
        _kv_rows_kernel, grid=(nb, keep // tl),
        in_specs=[pl.BlockSpec((tl, ATTN_WIDTH), lambda b, i: (row_blk(b, i), g * 3 + 1)),
                  pl.BlockSpec((tl, ATTN_WIDTH), lambda b, i: (row_blk(b, i), g * 3 + 2))],
        out_specs=pl.BlockSpec((tl * per_tok, ATTN_HEAD_DIM), lambda b, i: (b * (keep // tl) + i, 0)),
        out_shape=jax.ShapeDtypeStruct((nb * keep * per_tok, ATTN_HEAD_DIM), F32),
        compiler_params=_cparams(("parallel", "parallel")),
        name="kv_rows")(qkv2d, qkv2d)
    return out.reshape(nb, keep, 2, ATTN_HEADS, ATTN_HEAD_DIM)


def _attn_decode_kernel(qkv_ref, c0_ref, c1_ref, c2_ref, y_ref):
    caches = (c0_ref, c1_ref, c2_ref)
    outs, lams = [], []
    rnd = lambda a: a.astype(BF16).astype(F32)
    for g in range(len(ATTN_PATTERNS)):
        q = rnd(qkv_ref[0, g, 0])
        k_new = rnd(qkv_ref[0, g, 1])
        v_new = rnd(qkv_ref[0, g, 2])
        kc = rnd(caches[g][:, 0])
        vc = rnd(caches[g][:, 1])
        s = jnp.sum(kc * q[None], axis=-1, keepdims=True) * ATTN_SCALE
        s_new = jnp.sum(k_new * q, axis=-1, keepdims=True) * ATTN_SCALE
        m = jnp.maximum(jnp.max(s, axis=0), s_new)
        p = jnp.exp(s - m[None])
        p_new = jnp.exp(s_new - m)
        l = jnp.sum(p, axis=0) + p_new
        o = (jnp.sum(rnd(p) * vc, axis=0) + rnd(p_new) * v_new) / l
        outs.append(o)
        lams.append(m + jnp.log(l))
    mx = jnp.maximum(jnp.maximum(lams[0], lams[1]), lams[2])
    es = [jnp.exp(lm - mx) for lm in lams]
    num = es[0] * outs[0] + es[1] * outs[1] + es[2] * outs[2]
    y_ref[0] = num / (es[0] + es[1] + es[2])


def attn_decode(qkv_s, caches):
    nb = qkv_s.shape[0]
    in_specs = [pl.BlockSpec((1, 3, 3, ATTN_HEADS, ATTN_HEAD_DIM), lambda b: (b, 0, 0, 0, 0))]
    args = [qkv_s]
    for c, (window, dil) in zip(caches, ATTN_PATTERNS):
        assert c.shape[1] == window, "cache must hold a full window"
        n_back = window // dil
        args.append(c.reshape(nb, n_back, dil, 2, ATTN_HEADS, ATTN_HEAD_DIM))
        in_specs.append(pl.BlockSpec((None, n_back, None, 2, ATTN_HEADS, ATTN_HEAD_DIM),
                                     lambda b: (b, 0, 0, 0, 0, 0)))
    return pl.pallas_call(
        _attn_decode_kernel, grid=(nb,),
        in_specs=in_specs,
        out_specs=pl.BlockSpec((1, ATTN_HEADS, ATTN_HEAD_DIM), lambda b: (b, 0, 0)),
        out_shape=jax.ShapeDtypeStruct((nb, ATTN_HEADS, ATTN_HEAD_DIM), F32),
        compiler_params=_cparams(("parallel",)),
        name="attn_decode")(*args)


_R_E1, _R_E2, _R_RANK1, _R_RANK2, _R_W1, _R_W2 = range(6)


def _router_kernel(x_ref, g_ref, wr_ref, info_ref, cnt_ref, carry_ref):
    tm = x_ref.shape[0]

    @pl.when(pl.program_id(0) == 0)
    def _():
        carry_ref[...] = jnp.zeros_like(carry_ref)

    xn = _rms(x_ref[...], g_ref[...])
    logits = _mm(xn, wr_ref[...])
    lane = lax.broadcasted_iota(I32, (tm, LANES), 1)
    lanef = lane.astype(F32)
    big = float(LANES)
    neg = -jnp.inf

    is_g = (lane >= GROUP_LOGIT_LANE) & (lane < GROUP_LOGIT_LANE + N_EXPERT_GROUPS)
    gl = jnp.where(is_g, logits, neg)
    gmax = jnp.max(gl, axis=-1, keepdims=True)
    grp = jnp.min(jnp.where(gl == gmax, lanef, big), axis=-1, keepdims=True) - GROUP_LOGIT_LANE
    g_gate = 1.0 / jnp.sum(jnp.exp(gl - gmax), axis=-1, keepdims=True)

    lo = grp * EXPERTS_PER_GROUP
    in_grp = (lanef >= lo) & (lanef < lo + EXPERTS_PER_GROUP)
    el = jnp.where(in_grp, logits, neg)
    m1 = jnp.max(el, axis=-1, keepdims=True)
    i1 = jnp.min(jnp.where(el == m1, lanef, big), axis=-1, keepdims=True)
    el2 = jnp.where(lanef == i1, neg, el)
    m2 = jnp.max(el2, axis=-1, keepdims=True)
    i2 = jnp.min(jnp.where(el2 == m2, lanef, big), axis=-1, keepdims=True)
    ratio = jnp.exp(m2 - m1)
    w1 = g_gate / (1.0 + ratio)
    w2 = g_gate * ratio / (1.0 + ratio)

    oh1 = lanef == i1
    oh2 = lanef == i2
    a = (oh1 | oh2).astype(BF16)
    ri = lax.broadcasted_iota(I32, (tm, tm), 0)
    ci = lax.broadcasted_iota(I32, (tm, tm), 1)
    before = (ci < ri).astype(BF16)
    pref = jnp.dot(before, a, preferred_element_type=F32) + carry_ref[...]
    rank1 = jnp.sum(jnp.where(oh1, pref, 0.0), axis=-1, keepdims=True)
    rank2 = jnp.sum(jnp.where(oh2, pref, 0.0), axis=-1, keepdims=True)
    carry_ref[...] = carry_ref[...] + jnp.sum(a.astype(F32), axis=0, keepdims=True)

    info = jnp.zeros((tm, LANES), F32)
    for ln, val in ((_R_E1, i1), (_R_E2, i2), (_R_RANK1, rank1), (_R_RANK2, rank2),
                    (_R_W1, w1), (_R_W2, w2)):
        info = jnp.where(lane == ln, val, info)
    info_ref[...] = info
    cnt_ref[...] = carry_ref[...]


def moe_router(x, g, wr, *, tm):
    t, d = x.shape
    return pl.pallas_call(
        _router_kernel, grid=(t // tm,),
        in_specs=[pl.BlockSpec((tm, d), lambda i: (i, 0)),
                  pl.BlockSpec((1, d), lambda i: (0, 0)),
                  pl.BlockSpec((d, LANES), lambda i: (0, 0))],
        out_specs=[pl.BlockSpec((tm, LANES), lambda i: (i, 0)),
                   pl.BlockSpec((1, LANES), lambda i: (0, 0))],
        out_shape=[jax.ShapeDtypeStruct((t, LANES), F32), jax.ShapeDtypeStruct((1, LANES), F32)],
        scratch_shapes=[pltpu.VMEM((1, LANES), F32)],
        compiler_params=_cparams(("arbitrary",)),
        name="moe_router")(x, g, wr)


_ROW_UNROLL = 8


def _row_copies(n_rows, make_copies):
    def body(i, carry):
        for u in range(_ROW_UNROLL):
            for cp in make_copies(i * _ROW_UNROLL + u):
                cp.start()
        return carry
    lax.fori_loop(0, n_rows // _ROW_UNROLL, body, 0)


def _dispatch_kernel(pos_ref, x_ref, g_ref, xs_in_ref, xs_ref, xn_ref, sem):
    del xs_in_ref
    tm = x_ref.shape[0]
    xn_ref[...] = _rms(x_ref[...], g_ref[...])

    def copies(r):
        src = xn_ref.at[pl.ds(r, 1)]
        return [pltpu.make_async_copy(src, xs_ref.at[pl.ds(pos_ref[0, 0, k * tm + r], 1)], sem)
                for k in range(2)]

    _row_copies(tm, copies)
    for _ in range(2):
        pltpu.make_async_copy(xn_ref, xs_ref.at[pl.ds(0, tm)], sem).wait()


def moe_dispatch(x, g, pos, xs_init, *, tm):
    t, d = x.shape
    n = t // tm
    pos_blk = pos.reshape(n, tm, 2).transpose(0, 2, 1).reshape(n, 1, 2 * tm)
    return pl.pallas_call(
        _dispatch_kernel, grid=(n,),
        in_specs=[pl.BlockSpec((1, 1, 2 * tm), lambda i: (i, 0, 0), memory_space=pltpu.SMEM),
                  pl.BlockSpec((tm, d), lambda i: (i, 0)),
                  pl.BlockSpec((1, d), lambda i: (0, 0)),
                  pl.BlockSpec(memory_space=pl.ANY)],
        out_specs=pl.BlockSpec(memory_space=pl.ANY),
        out_shape=jax.ShapeDtypeStruct(xs_init.shape, F32),
        scratch_shapes=[pltpu.VMEM((tm, d), F32), pltpu.SemaphoreType.DMA(())],
        input_output_aliases={3: 0},
        compiler_params=_cparams(("arbitrary",)),
        name="moe_dispatch")(pos_blk, x, g, xs_init)


def _expert_kernel(te_ref, tv_ref, xs_ref, wgu_ref, wdn_ref, ys_ref, wgu_bf, wdn_bf):
    t = pl.program_id(0)
    changed = jnp.logical_or(t == 0, te_ref[t] != te_ref[jnp.maximum(t - 1, 0)])

    @pl.when(changed)
    def _():
        wgu_bf[...] = wgu_ref[...].astype(BF16)
        wdn_bf[...] = wdn_ref[...].astype(BF16)

    @pl.when(tv_ref[t] > 0)
    def _():
        gu = jnp.dot(xs_ref[...].astype(BF16), wgu_bf[...], preferred_element_type=F32)
        act = _silu(gu[:, :D_EXPERT]) * gu[:, D_EXPERT:]
        ys_ref[...] = jnp.dot(act.astype(BF16), wdn_bf[...], preferred_element_type=F32)

    @pl.when(tv_ref[t] == 0)
    def _():
        ys_ref[...] = jnp.zeros_like(ys_ref)


def moe_experts(xs, w_gate_up, w_down, layer, tile_expert, tile_valid, *, tmx):
    r, d = xs.shape
    n_tiles = r // tmx
    grid_spec = pltpu.PrefetchScalarGridSpec(
        num_scalar_prefetch=2, grid=(n_tiles,),
        in_specs=[pl.BlockSpec((tmx, d), lambda t, te, tv: (t, 0)),
                  pl.BlockSpec((None, None, d, 2 * D_EXPERT), lambda t, te, tv: (layer, te[t], 0, 0)),
                  pl.BlockSpec((None, None, D_EXPERT, d), lambda t, te, tv: (layer, te[t], 0, 0))],
        out_specs=pl.BlockSpec((tmx, d), lambda t, te, tv: (t, 0)),
        scratch_shapes=[pltpu.VMEM((d, 2 * D_EXPERT), BF16), pltpu.VMEM((D_EXPERT, d), BF16)])
    return pl.pallas_call(
        _expert_kernel, grid_spec=grid_spec,
        out_shape=jax.ShapeDtypeStruct((r, d), F32),
        compiler_params=_cparams(("arbitrary",)),
        name="moe_experts")(tile_expert, tile_valid, xs, w_gate_up, w_down)


def _combine_kernel(pos_ref, x_ref, info_ref, g_ref, ys_ref, o_ref, buf_ref, sem, *, final_norm):
    tm = x_ref.shape[0]

    def copies(r):
        return [pltpu.make_async_copy(ys_ref.at[pl.ds(pos_ref[0, 0, k * tm + r], 1)],
                                      buf_ref.at[k, pl.ds(r, 1)], sem)
                for k in range(2)]

    _row_copies(tm, copies)
    for k in range(2):
        pltpu.make_async_copy(ys_ref.at[pl.ds(0, tm)], buf_ref.at[k], sem).wait()
    info = info_ref[...]
    out = (x_ref[...] + info[:, _R_W1:_R_W1 + 1] * buf_ref[0]
           + info[:, _R_W2:_R_W2 + 1] * buf_ref[1])
    if final_norm:
        out = _rms(out, g_ref[...])
    o_ref[...] = out


def moe_combine(x, info, pos, ys, g_final, *, tm, final_norm):
    t, d = x.shape
    n = t // tm
    pos_blk = pos.reshape(n, tm, 2).transpose(0, 2, 1).reshape(n, 1, 2 * tm)
    return pl.pallas_call(
        functools.partial(_combine_kernel, final_norm=final_norm), grid=(n,),
        in_specs=[pl.BlockSpec((1, 1, 2 * tm), lambda i: (i, 0, 0), memory_space=pltpu.SMEM),
                  pl.BlockSpec((tm, d), lambda i: (i, 0)),
                  pl.BlockSpec((tm, LANES), lambda i: (i, 0)),
                  pl.BlockSpec((1, d), lambda i: (0, 0)),
                  pl.BlockSpec(memory_space=pl.ANY)],
        out_specs=pl.BlockSpec((tm, d), lambda i: (i, 0)),
        out_shape=jax.ShapeDtypeStruct((t, d), F32),
        scratch_shapes=[pltpu.VMEM((2, tm, d), F32), pltpu.SemaphoreType.DMA(())],
        compiler_params=_cparams(("arbitrary",)),
        name="moe_combine")(pos_blk, x, info, g_final, ys)


def _router_weight(w_group, w_expert):
    d = w_group.shape[0]
    we = jnp.transpose(w_expert, (1, 0, 2)).reshape(d, N_EXPERTS)
    w = jnp.concatenate([we, w_group], axis=1)
    return jnp.pad(w, ((0, 0), (0, LANES - w.shape[1]))).astype(BF16)


def hier_moe_layer(x, norm_g, w_group, w_expert, w_gate_up, w_down, layer, g_final, *, tm, tmx,
                   final_norm):
    t, d = x.shape
    g = norm_g.reshape(1, d)
    info, counts = moe_router(x, g, _router_weight(w_group, w_expert), tm=tm)

    counts = counts[0, :N_EXPERTS].astype(I32)
    padded = ((counts + tmx - 1) // tmx) * tmx
    ends = jnp.cumsum(padded)
    offsets = ends - padded
    e12 = info[:, _R_E1:_R_E2 + 1].astype(I32)
    rank12 = info[:, _R_RANK1:_R_RANK2 + 1].astype(I32)
    pos = offsets[e12] + rank12
    n_rows = 2 * t + N_EXPERTS * tmx
    n_tiles = n_rows // tmx
    tile_start = jnp.arange(n_tiles, dtype=I32) * tmx
    tile_valid = (tile_start < ends[-1]).astype(I32)
    probe = jnp.minimum(tile_start, jnp.maximum(ends[-1] - 1, 0))
    tile_expert = jnp.sum((probe[:, None] >= ends[None, :]).astype(I32), axis=1)
    tile_expert = jnp.minimum(tile_expert, N_EXPERTS - 1)

    depth = w_gate_up.shape[0]
    xs = moe_dispatch(x, g, pos, jnp.zeros((n_rows, d), F32), tm=tm)
    ys = moe_experts(xs, w_gate_up.reshape(depth, N_EXPERTS, d, 2 * D_EXPERT),
                     w_down.reshape(depth, N_EXPERTS, D_EXPERT, d), layer,
                     tile_expert, tile_valid, tmx=tmx)
    return moe_combine(x, info, pos, ys, g_final.reshape(1, d), tm=tm, final_norm=final_norm)


def kernel(x_prompt, x_sample, state_conv, state_ssm, cache_kv_w128, cache_kv_w512, cache_kv_w2048, norm_mix, norm_ffn, norm_final, ssm_in_proj, ssm_conv_w, ssm_conv_b, ssm_dt_bias, ssm_a_log, ssm_d, ssm_norm, ssm_out_proj, attn_qkv, attn_out, moe_route_group, moe_route_expert, moe_w_gate_up, moe_w_down):
    nb, seq, d = x_prompt.shape
    ns = x_sample.shape[0]
    t = nb * seq
    tm = min(t, 1024)
    n_groups = len(ATTN_PATTERNS)
    kv_caches = (cache_kv_w128, cache_kv_w512, cache_kv_w2048)

    ssm_params = (ssm_in_proj[0], ssm_conv_w[0], ssm_conv_b[0], ssm_dt_bias[0], ssm_a_log[0],
                  ssm_d[0], ssm_norm[0], ssm_out_proj[0])
    xp, conv_p, ssm_p = _mamba_prompt(x_prompt, norm_mix[0], *ssm_params)
    xs, conv_s, ssm_s = _mamba_sample(x_sample, norm_mix[0], state_conv[0], state_ssm[0], *ssm_params)

    def moe(x2d, i, tm_moe, tmx, final_norm):
        return hier_moe_layer(x2d, norm_ffn[i], moe_route_group[i], moe_route_expert[i],
                              moe_w_gate_up, moe_w_down, i, norm_final,
                              tm=tm_moe, tmx=tmx, final_norm=final_norm)

    xp = moe(xp.reshape(t, d), 0, 256, 256, False)
    xs = moe(xs.reshape(ns, d), 0, ns, 16, False)

    g1 = norm_mix[1].reshape(1, d)
    w_qkv = attn_qkv[0].astype(BF16)
    w_o = attn_out[0].astype(BF16)
    qkv_p = norm_matmul(xp, g1, w_qkv, tm=tm, tn=1024)
    y_p = attn_prompt(qkv_p.reshape(nb, seq, -1), nb, seq)
    xp = matmul_res(y_p.reshape(t, ATTN_WIDTH), w_o, xp, tm=tm)
    kv_p = [kv_rows(qkv_p, g, nb, seq, min(window, seq)) for g, (window, _) in enumerate(ATTN_PATTERNS)]

    qkv_s = norm_matmul(xs, g1, w_qkv, tm=ns, tn=1024)
    qkv_s5 = qkv_s.reshape(ns, n_groups, 3, ATTN_HEADS, ATTN_HEAD_DIM)
    y_s = attn_decode(qkv_s5, [c[0] for c in kv_caches])
    xs = matmul_res(y_s.reshape(ns, ATTN_WIDTH), w_o, xs, tm=ns)
    kv_s = [qkv_s5[:, g, 1:3][:, None] for g in range(n_groups)]

    y_prompt = moe(xp, 1, 256, 256, True).reshape(nb, seq, d)
    y_sample = moe(xs, 1, ns, 16, True).reshape(ns, 1, d)

    return (y_prompt, y_sample, conv_p[None], conv_s[None], ssm_p[None], ssm_s[None],
            kv_p[0][None], kv_s[0][None], kv_p[1][None], kv_s[1][None], kv_p[2][None], kv_s[2][None])
```

```python
import functools
import math

import jax
import jax.numpy as jnp
from jax import lax
from jax.experimental import pallas as pl
from jax.experimental.pallas import tpu as pltpu

F32 = jnp.float32
BF16 = jnp.bfloat16
I32 = jnp.int32

EPS = 1e-6
D_MODEL = 1024
LANES = 128
SUBLANES = 8

SSM_D_INNER = 2048
SSM_HEAD_DIM = 64
SSM_HEADS = 32
SSM_GROUPS = 4
SSM_HEADS_PER_GROUP = 8
SSM_D_STATE = 128
CONV_WIDTH = 4
SSD_CHUNK = 128
SSM_BC_DIM = 2 * SSM_GROUPS * SSM_D_STATE
SSM_CONV_DIM = SSM_D_INNER + SSM_BC_DIM

ATTN_PATTERNS = ((128, 1), (512, 4), (2048, 16))
ATTN_HEAD_DIM = 64
ATTN_HEADS = 16
ATTN_WIDTH = 1024
ATTN_BLOCK = 128
ATTN_SCALE = ATTN_HEAD_DIM ** -0.5

N_EXPERT_GROUPS = 4
EXPERTS_PER_GROUP = 8
N_EXPERTS = N_EXPERT_GROUPS * EXPERTS_PER_GROUP
D_EXPERT = 512
GROUP_LOGIT_LANE = N_EXPERTS

VMEM_LIMIT = 56 * 1024 * 1024


def _cparams(sem):
    return pltpu.CompilerParams(dimension_semantics=sem, vmem_limit_bytes=VMEM_LIMIT)


def _rms(x, g):
    ms = jnp.mean(x * x, axis=-1, keepdims=True)
    return x * lax.rsqrt(ms + EPS) * g


def _silu(x):
    return x * jax.nn.sigmoid(x)


def _softplus(x):
    return jnp.maximum(x, 0.0) + jnp.log1p(jnp.exp(-jnp.abs(x)))


def _mm(a, b):
    prec = lax.Precision.HIGHEST if b.dtype == F32 else None
    return jnp.dot(a.astype(b.dtype), b, preferred_element_type=F32, precision=prec)


def _norm_matmul_kernel(x_ref, g_ref, w_ref, o_ref, xn_ref):
    @pl.when(pl.program_id(1) == 0)
    def _():
        xn_ref[...] = _rms(x_ref[...], g_ref[...]).astype(xn_ref.dtype)

    o_ref[...] = _mm(xn_ref[...], w_ref[...])


def _norm_matmul2_kernel(x_ref, g_ref, w_ref, w2_ref, o_ref, o2_ref, xn_ref):
    @pl.when(pl.program_id(1) == 0)
    def _():
        xn = _rms(x_ref[...], g_ref[...]).astype(xn_ref.dtype)
        xn_ref[...] = xn
        o2_ref[...] = _mm(xn, w2_ref[...])

    o_ref[...] = _mm(xn_ref[...], w_ref[...])


def norm_matmul(x, g, w, w2=None, *, tm, tn):
    t, d = x.shape
    n = w.shape[1]
    grid = (t // tm, n // tn)
    x_spec = pl.BlockSpec((tm, d), lambda i, j: (i, 0))
    g_spec = pl.BlockSpec((1, d), lambda i, j: (0, 0))
    w_spec = pl.BlockSpec((d, tn), lambda i, j: (0, j))
    o_spec = pl.BlockSpec((tm, tn), lambda i, j: (i, j))
    scratch = [pltpu.VMEM((tm, d), w.dtype)]
    if w2 is None:
        return pl.pallas_call(
            _norm_matmul_kernel, grid=grid,
            in_specs=[x_spec, g_spec, w_spec], out_specs=o_spec,
            out_shape=jax.ShapeDtypeStruct((t, n), F32),
            scratch_shapes=scratch,
            compiler_params=_cparams(("parallel", "arbitrary")),
            name="norm_matmul")(x, g, w)
    n2 = w2.shape[1]
    return pl.pallas_call(
        _norm_matmul2_kernel, grid=grid,
        in_specs=[x_spec, g_spec, w_spec, pl.BlockSpec((d, n2), lambda i, j: (0, 0))],
        out_specs=[o_spec, pl.BlockSpec((tm, n2), lambda i, j: (i, 0))],
        out_shape=[jax.ShapeDtypeStruct((t, n), F32), jax.ShapeDtypeStruct((t, n2), F32)],
        scratch_shapes=scratch,
        compiler_params=_cparams(("parallel", "arbitrary")),
        name="norm_matmul2")(x, g, w, w2)


def _matmul_res_kernel(a_ref, w_ref, r_ref, o_ref):
    o_ref[...] = r_ref[...] + _mm(a_ref[...], w_ref[...])


def matmul_res(a, w, res, *, tm):
    t, k = a.shape
    n = w.shape[1]
    return pl.pallas_call(
        _matmul_res_kernel, grid=(t // tm,),
        in_specs=[pl.BlockSpec((tm, k), lambda i: (i, 0)),
                  pl.BlockSpec((k, n), lambda i: (0, 0)),
                  pl.BlockSpec((tm, n), lambda i: (i, 0))],
        out_specs=pl.BlockSpec((tm, n), lambda i: (i, 0)),
        out_shape=jax.ShapeDtypeStruct((t, n), F32),
        compiler_params=_cparams(("parallel",)),
        name="matmul_res")(a, w, res)


_CONV_COLS = 512


def _ssd_kernel(z_ref, xs_ref, bc_ref, dt_ref, cw_ref, cb_ref, dtb_ref, alog_ref,
                dsk_ref, ng_ref,
                y_ref, tail_out_ref, st_ref,
                tail_ref, h_ref, xc_ref, xst_ref, yt_ref, dtt_ref, cst_ref, xde_ref, yn_ref):
    q = SSD_CHUNK
    c = pl.program_id(1)

    @pl.when(c == 0)
    def _():
        tail_ref[...] = jnp.zeros_like(tail_ref)
        h_ref[...] = jnp.zeros_like(h_ref)

    row = lax.broadcasted_iota(I32, (q, _CONV_COLS), 0)
    for k in range(SSM_CONV_DIM // _CONV_COLS):
        lo = k * _CONV_COLS
        if lo < SSM_D_INNER:
            src = xs_ref[0, :, lo:lo + _CONV_COLS]
        else:
            src = bc_ref[0, :, lo - SSM_D_INNER:lo - SSM_D_INNER + _CONV_COLS]
        tl = tail_ref[:, lo:lo + _CONV_COLS]
        w = cw_ref[:, lo:lo + _CONV_COLS]
        acc = src * w[CONV_WIDTH - 1:CONV_WIDTH] + cb_ref[:, lo:lo + _CONV_COLS]
        for s in range(1, CONV_WIDTH):
            cur = pltpu.roll(src, s, axis=0)
            prev = jnp.tile(pltpu.roll(tl, s, axis=0), (q // SUBLANES, 1))
            shifted = jnp.where(row < s, prev, cur)
            acc = acc + shifted * w[CONV_WIDTH - 1 - s:CONV_WIDTH - s]
        xc_ref[:, lo:lo + _CONV_COLS] = _silu(acc)
        tail_ref[:, lo:lo + _CONV_COLS] = src[q - SUBLANES:, :]

    for k in range(SSM_D_INNER // LANES):
        xst_ref[k * LANES:(k + 1) * LANES, :] = xc_ref[:, k * LANES:(k + 1) * LANES].T

    dt = _softplus(dt_ref[0] + dtb_ref[...])
    da = dt * (-jnp.exp(alog_ref[...]))
    ri = lax.broadcasted_iota(I32, (q, q), 0)
    ci = lax.broadcasted_iota(I32, (q, q), 1)
    tril = (ci <= ri).astype(F32)
    cs = jnp.dot(tril, da, preferred_element_type=F32, precision=lax.Precision.HIGHEST)
    dtt_ref[...] = dt.T
    cst_ref[...] = cs.T

    causal_t = ci >= ri
    hpg = SSM_HEADS_PER_GROUP
    gp = hpg * SSM_HEAD_DIM
    for g in range(SSM_GROUPS):
        b0 = SSM_D_INNER + g * SSM_D_STATE
        c0 = SSM_D_INNER + SSM_GROUPS * SSM_D_STATE + g * SSM_D_STATE
        bm = xc_ref[:, b0:b0 + SSM_D_STATE].astype(BF16)
        cm = xc_ref[:, c0:c0 + SSM_D_STATE].astype(BF16)
        cbt = lax.dot_general(bm, cm, (((1,), (1,)), ((), ())), preferred_element_type=F32)
        hg = h_ref[g * gp:(g + 1) * gp, :].astype(BF16)
        yt_ref[g * gp:(g + 1) * gp, :] = lax.dot_general(
            hg, cm, (((1,), (1,)), ((), ())), preferred_element_type=F32)

        for e in range(hpg):
            hd = g * hpg + e
            rows = slice(hd * SSM_HEAD_DIM, (hd + 1) * SSM_HEAD_DIM)
            csr = cst_ref[hd:hd + 1, :]
            rowb = jnp.broadcast_to(csr, (q, q))
            colb = rowb.T
            dec = jnp.exp(jnp.where(causal_t, rowb - colb, -jnp.inf))
            mt = (cbt * dec).astype(BF16)
            dtr = dtt_ref[hd:hd + 1, :]
            xsh = xst_ref[rows, :]
            xdt = xsh * dtr
            ydt = jnp.dot(xdt.astype(BF16), mt, preferred_element_type=F32)
            cs_end = csr[:, q - 1:q]
            yt_ref[rows, :] = yt_ref[rows, :] * jnp.exp(csr) + ydt + xsh * dsk_ref[rows, :]
            xde_ref[e * SSM_HEAD_DIM:(e + 1) * SSM_HEAD_DIM, :] = (
                xdt * jnp.exp(cs_end - csr)).astype(BF16)
            h_ref[rows, :] = h_ref[rows, :] * jnp.exp(cs_end)
        h_ref[g * gp:(g + 1) * gp, :] = h_ref[g * gp:(g + 1) * gp, :] + jnp.dot(
            xde_ref[...], bm, preferred_element_type=F32)

    ssq = jnp.zeros((q, 1), F32)
    for k in range(SSM_D_INNER // LANES):
        yk = yt_ref[k * LANES:(k + 1) * LANES, :].T * _silu(z_ref[0, :, k * LANES:(k + 1) * LANES])
        yn_ref[:, k * LANES:(k + 1) * LANES] = yk
        ssq = ssq + jnp.sum(yk * yk, axis=-1, keepdims=True)
    scale = lax.rsqrt(ssq * (1.0 / SSM_D_INNER) + EPS)
    y_ref[0] = (yn_ref[...] * scale * ng_ref[...]).astype(BF16)

    @pl.when(c == pl.num_programs(1) - 1)
    def _():
        tail_out_ref[0] = tail_ref[...]
        st_ref[0] = h_ref[...]


def ssd_prompt(proj, dt_raw, conv_w, conv_b, dt_bias, a_log, d_rows, norm_g, nb, seq):
    nc = seq // SSD_CHUNK
    q = SSD_CHUNK
    const2 = lambda b, c: (0, 0)
    return pl.pallas_call(
        _ssd_kernel, grid=(nb, nc),
        in_specs=[
            pl.BlockSpec((1, q, SSM_D_INNER), lambda b, c: (b, c, 0)),
            pl.BlockSpec((1, q, SSM_D_INNER), lambda b, c: (b, c, 1)),
            pl.BlockSpec((1, q, SSM_BC_DIM), lambda b, c: (b, c, 2 * SSM_D_INNER // SSM_BC_DIM)),
            pl.BlockSpec((1, q, LANES), lambda b, c: (b, c, 0)),
            pl.BlockSpec((CONV_WIDTH, SSM_CONV_DIM), const2),
            pl.BlockSpec((1, SSM_CONV_DIM), const2),
            pl.BlockSpec((1, LANES), const2),
            pl.BlockSpec((1, LANES), const2),
            pl.BlockSpec((SSM_D_INNER, LANES), const2),
            pl.BlockSpec((1, SSM_D_INNER), const2),
        ],
        out_specs=[
            pl.BlockSpec((1, q, SSM_D_INNER), lambda b, c: (b, c, 0)),
            pl.BlockSpec((1, SUBLANES, SSM_CONV_DIM), lambda b, c: (b, 0, 0)),
            pl.BlockSpec((1, SSM_D_INNER, SSM_D_STATE), lambda b, c: (b, 0, 0)),
        ],
        out_shape=[
            jax.ShapeDtypeStruct((nb, seq, SSM_D_INNER), BF16),
            jax.ShapeDtypeStruct((nb, SUBLANES, SSM_CONV_DIM), F32),
            jax.ShapeDtypeStruct((nb, SSM_D_INNER, SSM_D_STATE), F32),
        ],
        scratch_shapes=[
            pltpu.VMEM((SUBLANES, SSM_CONV_DIM), F32),
            pltpu.VMEM((SSM_D_INNER, SSM_D_STATE), F32),
            pltpu.VMEM((q, SSM_CONV_DIM), F32),
            pltpu.VMEM((SSM_D_INNER, q), F32),
            pltpu.VMEM((SSM_D_INNER, q), F32),
            pltpu.VMEM((LANES, q), F32),
            pltpu.VMEM((LANES, q), F32),
            pltpu.VMEM((SSM_HEADS_PER_GROUP * SSM_HEAD_DIM, q), BF16),
            pltpu.VMEM((q, SSM_D_INNER), F32),
        ],
        compiler_params=_cparams(("parallel", "arbitrary")),
        name="ssd_prompt")(proj, proj, proj, dt_raw, conv_w, conv_b, dt_bias, a_log, d_rows, norm_g)


def _pad_lanes(v, n=LANES):
    return jnp.pad(v.astype(F32), (0, n - v.shape[0])).reshape(1, n)


def _ssm_weights(w_in, dtype):
    n_main = SSM_D_INNER + SSM_CONV_DIM
    w_main = w_in[:, :n_main].astype(dtype)
    w_dt = jnp.pad(w_in[:, n_main:], ((0, 0), (0, LANES - SSM_HEADS))).astype(dtype)
    return w_main, w_dt


def _mamba_prompt(x, norm_g, w_in, conv_w, conv_b, dt_bias, a_log, d_skip, ssm_norm, w_out):
    nb, seq, d = x.shape
    t = nb * seq
    w_main, w_dt = _ssm_weights(w_in, BF16)
    tm = min(t, 1024)
    proj, dt_raw = norm_matmul(x.reshape(t, d), norm_g.reshape(1, d), w_main, w_dt, tm=tm, tn=1024)
    d_rows = jnp.broadcast_to(jnp.repeat(d_skip.astype(F32), SSM_HEAD_DIM)[:, None], (SSM_D_INNER, LANES))
    y, tail, st = ssd_prompt(
        proj.reshape(nb, seq, -1), dt_raw.reshape(nb, seq, LANES),
        conv_w, conv_b.reshape(1, -1), _pad_lanes(dt_bias), _pad_lanes(a_log), d_rows,
        ssm_norm.reshape(1, -1), nb, seq)
    x_new = matmul_res(y.reshape(t, SSM_D_INNER), w_out.astype(BF16), x.reshape(t, d), tm=tm)
    conv_state = tail[:, SUBLANES - (CONV_WIDTH - 1):]
    ssm_state = st.reshape(nb, SSM_HEADS, SSM_HEAD_DIM, SSM_D_STATE)
    return x_new.reshape(nb, seq, d), conv_state, ssm_state


_DEC_HEADS = 4


def _ssd_decode_kernel(zt_ref, xbct_ref, convt_ref, dtt_raw_ref, cwt_ref, cbt_ref, dtbt_ref,
                       alogt_ref, dsk_ref, ngt_ref, wout_ref, res_ref, st_ref,
                       so_ref, out_ref,
                       xct_ref, dtt_ref, dect_ref, bcn_ref, yt_ref, *, nb):
    step = pl.program_id(0)
    rb = 512

    @pl.when(step == 0)
    def _():
        for k in range(SSM_CONV_DIM // rb):
            rows = slice(k * rb, (k + 1) * rb)
            acc = xbct_ref[rows, :] * cwt_ref[rows, CONV_WIDTH - 1:CONV_WIDTH] + cbt_ref[rows, 0:1]
            for j in range(CONV_WIDTH - 1):
                acc = acc + convt_ref[j, rows, :] * cwt_ref[rows, j:j + 1]
            xct_ref[rows, :] = _silu(acc)
        dt = _softplus(dtt_raw_ref[...] + dtbt_ref[:, 0:1])
        dtt_ref[...] = dt
        dect_ref[...] = jnp.exp(dt * (-jnp.exp(alogt_ref[:, 0:1])))
        for k in range(SSM_BC_DIM // LANES):
            bcn_ref[:, k * LANES:(k + 1) * LANES] = xct_ref[
                SSM_D_INNER + k * LANES:SSM_D_INNER + (k + 1) * LANES, :].T
        yt_ref[...] = jnp.zeros_like(yt_ref)

    for hh in range(_DEC_HEADS):
        hd = step * _DEC_HEADS + hh
        r0 = pl.multiple_of(hd * SSM_HEAD_DIM, SSM_HEAD_DIM)
        grp = hd // SSM_HEADS_PER_GROUP
        b_lane = pl.multiple_of(grp * SSM_D_STATE, SSM_D_STATE)
        c_lane = pl.multiple_of(SSM_GROUPS * SSM_D_STATE + grp * SSM_D_STATE, SSM_D_STATE)
        xh = xct_ref[pl.ds(r0, SSM_HEAD_DIM), :]
        dth = dtt_ref[pl.ds(hd, 1), :]
        dech = dect_ref[pl.ds(hd, 1), :]
        xdt = xh * dth
        for b in range(nb):
            brow = bcn_ref[b:b + 1, pl.ds(b_lane, SSM_D_STATE)]
            crow = bcn_ref[b:b + 1, pl.ds(c_lane, SSM_D_STATE)]
            hn = st_ref[b, hh] * dech[:, b:b + 1] + xdt[:, b:b + 1] * brow
            so_ref[b, hh] = hn
            yt_ref[pl.ds(r0, SSM_HEAD_DIM), b:b + 1] = jnp.sum(hn * crow, axis=-1, keepdims=True)

    @pl.when(step == pl.num_programs(0) - 1)
    def _():
        ssq = jnp.zeros((1, LANES), F32)
        for k in range(SSM_D_INNER // rb):
            rows = slice(k * rb, (k + 1) * rb)
            y = (yt_ref[rows, :] + xct_ref[rows, :] * dsk_ref[rows, :]) * _silu(zt_ref[rows, :])
            yt_ref[rows, :] = y
            ssq = ssq + jnp.sum(y * y, axis=0, keepdims=True)
        scale = lax.rsqrt(ssq * (1.0 / SSM_D_INNER) + EPS)
        acc = res_ref[...]
        for k in range(SSM_D_INNER // LANES):
            rows = slice(k * LANES, (k + 1) * LANES)
            yn = (yt_ref[rows, :] * scale * ngt_ref[rows, :]).T
            acc = acc + _mm(yn, wout_ref[rows, :])
        out_ref[...] = acc


def ssd_decode(zt, xbct, convt, dtt_raw, cwt, cbt, dtbt, alogt, d_rows, ngt, w_out, res, state):
    nb = state.shape[0]
    n_steps = SSM_HEADS // _DEC_HEADS
    full = lambda a: pl.BlockSpec(a.shape, lambda s, n=a.ndim: (0,) * n)
    st_spec = pl.BlockSpec((nb, _DEC_HEADS, SSM_HEAD_DIM, SSM_D_STATE), lambda s: (0, s, 0, 0))
    small = (zt, xbct, convt, dtt_raw, cwt, cbt, dtbt, alogt, d_rows, ngt, w_out, res)
    return pl.pallas_call(
        functools.partial(_ssd_decode_kernel, nb=nb), grid=(n_steps,),
        in_specs=[full(a) for a in small] + [st_spec],
        out_specs=[st_spec, full(res)],
        out_shape=[jax.ShapeDtypeStruct(state.shape, F32), jax.ShapeDtypeStruct(res.shape, F32)],
        scratch_shapes=[
            pltpu.VMEM((SSM_CONV_DIM, LANES), F32),
            pltpu.VMEM((LANES, LANES), F32),
            pltpu.VMEM((LANES, LANES), F32),
            pltpu.VMEM((LANES, SSM_BC_DIM), F32),
            pltpu.VMEM((SSM_D_INNER, LANES), F32),
        ],
        compiler_params=_cparams(("arbitrary",)),
        name="ssd_decode")(*small, state)


def _to_lanes(a, n=LANES):
    return jnp.pad(a.T, ((0, 0), (0, n - a.shape[0])))


def _mamba_sample(x, norm_g, conv_state, ssm_state, w_in, conv_w, conv_b, dt_bias, a_log, d_skip,
                  ssm_norm, w_out):
    nb, _, d = x.shape
    assert nb <= LANES and nb % SUBLANES == 0
    w_main, w_dt = _ssm_weights(w_in, BF16)
    proj, dt_raw = norm_matmul(x.reshape(nb, d), norm_g.reshape(1, d), w_main, w_dt, tm=nb, tn=1024)
    xbc = proj[:, SSM_D_INNER:]
    col = lambda v: jnp.broadcast_to(v.astype(F32)[:, None], (v.shape[0], LANES))
    convt = jnp.pad(jnp.transpose(conv_state, (1, 2, 0)), ((0, 0), (0, 0), (0, LANES - nb)))
    res = jnp.pad(x.reshape(nb, d), ((0, LANES - nb), (0, 0)))
    st, out = ssd_decode(
        _to_lanes(proj[:, :SSM_D_INNER]), _to_lanes(xbc), convt, _to_lanes(dt_raw),
        jnp.pad(conv_w.T, ((0, 0), (0, LANES - CONV_WIDTH))), col(conv_b),
        col(_pad_lanes(dt_bias)[0]), col(_pad_lanes(a_log)[0]),
        col(jnp.repeat(d_skip, SSM_HEAD_DIM)), col(ssm_norm), w_out.astype(BF16), res, ssm_state)
    new_conv = jnp.concatenate([conv_state[:, 1:], xbc[:, None, :]], axis=1)
    return out[:nb].reshape(nb, 1, d), new_conv, st


def _attn_block(q_ref, k_ref, v_ref, o_scr, lam_scr, start, dil, has_prev):
    nq = ATTN_BLOCK
    rows = pl.ds(start, nq, stride=dil) if dil > 1 else pl.ds(start, nq)
    hd = ATTN_HEAD_DIM
    q2 = q_ref[rows, :]
    lane = lax.broadcasted_iota(I32, (nq, LANES), 1)
    qblk = jnp.concatenate([jnp.where(lane < hd, q2, 0.0), jnp.where(lane >= hd, q2, 0.0)],
                           axis=0).astype(BF16)
    kj = lax.broadcasted_iota(I32, (nq, 2 * nq), 0)
    qi = lax.broadcasted_iota(I32, (nq, 2 * nq), 1) % nq
    nt = (((1,), (1,)), ((), ()))
    s = lax.dot_general(k_ref[rows, :].astype(BF16), qblk, nt,
                        preferred_element_type=F32) * ATTN_SCALE
    s = jnp.where(kj <= qi, s, -jnp.inf)
    vt = v_ref[rows, :].T.astype(BF16)
    if has_prev:
        prows = (pl.ds(start - dil * nq, nq, stride=dil) if dil > 1 else pl.ds(start - nq, nq))
        sp = lax.dot_general(k_ref[prows, :].astype(BF16), qblk, nt,
                             preferred_element_type=F32) * ATTN_SCALE
        s = jnp.concatenate([jnp.where(kj >= qi, sp, -jnp.inf), s], axis=0)
        vt = jnp.concatenate([v_ref[prows, :].T.astype(BF16), vt], axis=1)
    m = jnp.max(s, axis=0, keepdims=True)
    p = jnp.exp(s - m)
    l = jnp.sum(p, axis=0, keepdims=True)
    ot = jnp.dot(vt, p.astype(BF16), preferred_element_type=F32) * (1.0 / l)
    lam = jnp.broadcast_to(m + jnp.log(l), (hd, 2 * nq))
    o_scr[rows, :] = jnp.concatenate([ot[:hd, :nq], ot[hd:, nq:]], axis=0).T
    lam_scr[rows, :] = jnp.concatenate([lam[:, :nq], lam[:, nq:]], axis=0).T


def _attn_prompt_kernel(*refs, seq):
    qkv = refs[:9]
    y_ref = refs[9]
    kvt_refs = refs[10:13]
    o_scrs, lam_scrs = refs[13:16], refs[16:19]
    nq = ATTN_BLOCK
    for g, (window, dil) in enumerate(ATTN_PATTERNS):
        q_ref, k_ref, v_ref = qkv[3 * g:3 * g + 3]
        o_scr, lam_scr = o_scrs[g], lam_scrs[g]
        keep = min(window, seq)
        for c in range(keep // LANES):
            src = slice(seq - keep + c * LANES, seq - keep + (c + 1) * LANES)
            kvt_refs[g][0, :, c * LANES:(c + 1) * LANES] = k_ref[src, :].T
            kvt_refs[g][1, :, c * LANES:(c + 1) * LANES] = v_ref[src, :].T
        n_blk = seq // (dil * nq)
        for r in range(dil):
            for n in range(n_blk):
                _attn_block(q_ref, k_ref, v_ref, o_scr, lam_scr, r + dil * nq * n, dil, n > 0)

    rows_per = 256

    def merge(i, carry):
        rows = pl.ds(pl.multiple_of(i * rows_per, rows_per), rows_per)
        l0, l1, l2 = lam_scrs[0][rows, :], lam_scrs[1][rows, :], lam_scrs[2][rows, :]
        mx = jnp.maximum(jnp.maximum(l0, l1), l2)
        e0, e1, e2 = jnp.exp(l0 - mx), jnp.exp(l1 - mx), jnp.exp(l2 - mx)
        num = e0 * o_scrs[0][rows, :] + e1 * o_scrs[1][rows, :] + e2 * o_scrs[2][rows, :]
        y_ref[rows, :] = (num / (e0 + e1 + e2)).astype(BF16)
        return carry

    lax.fori_loop(0, seq // rows_per, merge, 0)


def attn_prompt(qkv, nb, seq):
    for window, dil in ATTN_PATTERNS:
        assert window // dil == ATTN_BLOCK and seq % (dil * ATTN_BLOCK) == 0
    n_pairs = ATTN_WIDTH // LANES
    keeps = [min(window, seq) for window, _ in ATTN_PATTERNS]
    in_specs = []
    for g in range(len(ATTN_PATTERNS)):
        for j in range(3):
            in_specs.append(pl.BlockSpec(
                (None, seq, LANES), lambda b, hp, g=g, j=j: (b, 0, (g * 3 + j) * n_pairs + hp)))
    outs = pl.pallas_call(
        functools.partial(_attn_prompt_kernel, seq=seq),
        grid=(nb, n_pairs),
        in_specs=in_specs,
        out_specs=[pl.BlockSpec((None, seq, LANES), lambda b, hp: (b, 0, hp))] + [
            pl.BlockSpec((None, 2, LANES, keep), lambda b, hp: (b, 0, hp, 0)) for keep in keeps],
        out_shape=[jax.ShapeDtypeStruct((nb, seq, ATTN_WIDTH), BF16)] + [
            jax.ShapeDtypeStruct((nb, 2, ATTN_WIDTH, keep), F32) for keep in keeps],
        scratch_shapes=[pltpu.VMEM((seq, LANES), F32)] * 6,
        compiler_params=_cparams(("parallel", "parallel")),
        name="attn_prompt")(*([qkv] * 9))
    kv = [jnp.transpose(kvt.reshape(nb, 2, ATTN_HEADS, ATTN_HEAD_DIM, keep), (0, 4, 1, 2, 3))
          for kvt, keep in zip(outs[1:], keeps)]
    return outs[0], kv


def _attn_decode_kernel(qkv_ref, c0_ref, c1_ref, c2_ref, y_ref):
    caches = (c0_ref, c1_ref, c2_ref)
    outs, lams = [], []
    rnd = lambda a: a.astype(BF16).astype(F32)
    for g in range(len(ATTN_PATTERNS)):
        q = rnd(qkv_ref[0, g, 0])
        k_new = rnd(qkv_ref[0, g, 1])
        v_new = rnd(qkv_ref[0, g, 2])
        kc = rnd(caches[g][:, 0])
        vc = rnd(caches[g][:, 1])
        s = jnp.sum(kc * q[None], axis=-1, keepdims=True) * ATTN_SCALE
        s_new = jnp.sum(k_new * q, axis=-1, keepdims=True) * ATTN_SCALE
        m = jnp.maximum(jnp.max(s, axis=0), s_new)
        p = jnp.exp(s - m[None])
        p_new = jnp.exp(s_new - m)
        l = jnp.sum(p, axis=0) + p_new
        o = (jnp.sum(rnd(p) * vc, axis=0) + rnd(p_new) * v_new) / l
        outs.append(o)
        lams.append(m + jnp.log(l))
    mx = jnp.maximum(jnp.maximum(lams[0], lams[1]), lams[2])
    es = [jnp.exp(lm - mx) for lm in lams]
    num = es[0] * outs[0] + es[1] * outs[1] + es[2] * outs[2]
    y_ref[0] = num / (es[0] + es[1] + es[2])


def attn_decode(qkv_s, caches):
    nb = qkv_s.shape[0]
    in_specs = [pl.BlockSpec((1, 3, 3, ATTN_HEADS, ATTN_HEAD_DIM), lambda b: (b, 0, 0, 0, 0))]
    args = [qkv_s]
    for c, (window, dil) in zip(caches, ATTN_PATTERNS):
        assert c.shape[1] == window, "cache must hold a full window"
        n_back = window // dil
        args.append(c.reshape(nb, n_back, dil, 2, ATTN_HEADS, ATTN_HEAD_DIM))
        in_specs.append(pl.BlockSpec((None, n_back, None, 2, ATTN_HEADS, ATTN_HEAD_DIM),
                                     lambda b: (b, 0, 0, 0, 0, 0)))
    return pl.pallas_call(
        _attn_decode_kernel, grid=(nb,),
        in_specs=in_specs,
        out_specs=pl.BlockSpec((1, ATTN_HEADS, ATTN_HEAD_DIM), lambda b: (b, 0, 0)),
        out_shape=jax.ShapeDtypeStruct((nb, ATTN_HEADS, ATTN_HEAD_DIM), F32),
        compiler_params=_cparams(("parallel",)),
        name="attn_decode")(*args)


_R_E1, _R_E2, _R_RANK1, _R_RANK2, _R_W1, _R_W2 = range(6)


def _router_kernel(x_ref, g_ref, wr_ref, info_ref, cnt_ref, carry_ref):
    tm = x_ref.shape[0]

    @pl.when(pl.program_id(0) == 0)
    def _():
        carry_ref[...] = jnp.zeros_like(carry_ref)

    xn = _rms(x_ref[...], g_ref[...])
    logits = _mm(xn, wr_ref[...])
    lane = lax.broadcasted_iota(I32, (tm, LANES), 1)
    lanef = lane.astype(F32)
    big = float(LANES)
    neg = -jnp.inf

    is_g = (lane >= GROUP_LOGIT_LANE) & (lane < GROUP_LOGIT_LANE + N_EXPERT_GROUPS)
    gl = jnp.where(is_g, logits, neg)
    gmax = jnp.max(gl, axis=-1, keepdims=True)
    grp = jnp.min(jnp.where(gl == gmax, lanef, big), axis=-1, keepdims=True) - GROUP_LOGIT_LANE
    g_gate = 1.0 / jnp.sum(jnp.exp(gl - gmax), axis=-1, keepdims=True)

    lo = grp * EXPERTS_PER_GROUP
    in_grp = (lanef >= lo) & (lanef < lo + EXPERTS_PER_GROUP)
    el = jnp.where(in_grp, logits, neg)
    m1 = jnp.max(el, axis=-1, keepdims=True)
    i1 = jnp.min(jnp.where(el == m1, lanef, big), axis=-1, keepdims=True)
    el2 = jnp.where(lanef == i1, neg, el)
    m2 = jnp.max(el2, axis=-1, keepdims=True)
    i2 = jnp.min(jnp.where(el2 == m2, lanef, big), axis=-1, keepdims=True)
    ratio = jnp.exp(m2 - m1)
    w1 = g_gate / (1.0 + ratio)
    w2 = g_gate * ratio / (1.0 + ratio)

    oh1 = lanef == i1
    oh2 = lanef == i2
    a = (oh1 | oh2).astype(BF16)
    ri = lax.broadcasted_iota(I32, (tm, tm), 0)
    ci = lax.broadcasted_iota(I32, (tm, tm), 1)
    before = (ci < ri).astype(BF16)
    pref = jnp.dot(before, a, preferred_element_type=F32) + carry_ref[...]
    rank1 = jnp.sum(jnp.where(oh1, pref, 0.0), axis=-1, keepdims=True)
    rank2 = jnp.sum(jnp.where(oh2, pref, 0.0), axis=-1, keepdims=True)
    carry_ref[...] = carry_ref[...] + jnp.sum(a.astype(F32), axis=0, keepdims=True)

    info = jnp.zeros((tm, LANES), F32)
    for ln, val in ((_R_E1, i1), (_R_E2, i2), (_R_RANK1, rank1), (_R_RANK2, rank2),
                    (_R_W1, w1), (_R_W2, w2)):
        info = jnp.where(lane == ln, val, info)
    info_ref[...] = info
    cnt_ref[...] = carry_ref[...]


def moe_router(x, g, wr, *, tm):
    t, d = x.shape
    return pl.pallas_call(
        _router_kernel, grid=(t // tm,),
        in_specs=[pl.BlockSpec((tm, d), lambda i: (i, 0)),
                  pl.BlockSpec((1, d), lambda i: (0, 0)),
                  pl.BlockSpec((d, LANES), lambda i: (0, 0))],
        out_specs=[pl.BlockSpec((tm, LANES), lambda i: (i, 0)),
                   pl.BlockSpec((1, LANES), lambda i: (0, 0))],
        out_shape=[jax.ShapeDtypeStruct((t, LANES), F32), jax.ShapeDtypeStruct((1, LANES), F32)],
        scratch_shapes=[pltpu.VMEM((1, LANES), F32)],
        compiler_params=_cparams(("arbitrary",)),
        name="moe_router")(x, g, wr)


_ROW_UNROLL = 8


def _row_copies(n_rows, make_copies):
    def body(i, carry):
        for u in range(_ROW_UNROLL):
            for cp in make_copies(i * _ROW_UNROLL + u):
                cp.start()
        return carry
    lax.fori_loop(0, n_rows // _ROW_UNROLL, body, 0)


def _dispatch_kernel(pos_ref, x_ref, g_ref, xs_in_ref, xs_ref, xn_ref, sem):
    del xs_in_ref
    tm = x_ref.shape[0]
    xn_ref[...] = _rms(x_ref[...], g_ref[...])

    def copies(r):
        src = xn_ref.at[pl.ds(r, 1)]
        return [pltpu.make_async_copy(src, xs_ref.at[pl.ds(pos_ref[0, 0, k * tm + r], 1)], sem)
                for k in range(2)]

    _row_copies(tm, copies)
    for _ in range(2):
        pltpu.make_async_copy(xn_ref, xs_ref.at[pl.ds(0, tm)], sem).wait()


def moe_dispatch(x, g, pos, xs_init, *, tm):
    t, d = x.shape
    n = t // tm
    pos_blk = pos.reshape(n, tm, 2).transpose(0, 2, 1).reshape(n, 1, 2 * tm)
    return pl.pallas_call(
        _dispatch_kernel, grid=(n,),
        in_specs=[pl.BlockSpec((1, 1, 2 * tm), lambda i: (i, 0, 0), memory_space=pltpu.SMEM),
                  pl.BlockSpec((tm, d), lambda i: (i, 0)),
                  pl.BlockSpec((1, d), lambda i: (0, 0)),
                  pl.BlockSpec(memory_space=pl.ANY)],
        out_specs=pl.BlockSpec(memory_space=pl.ANY),
        out_shape=jax.ShapeDtypeStruct(xs_init.shape, F32),
        scratch_shapes=[pltpu.VMEM((tm, d), F32), pltpu.SemaphoreType.DMA(())],
        input_output_aliases={3: 0},
        compiler_params=_cparams(("arbitrary",)),
        name="moe_dispatch")(pos_blk, x, g, xs_init)


def _expert_kernel(te_ref, tv_ref, xs_ref, wgu_ref, wdn_ref, ys_ref, wgu_bf, wdn_bf):
    t = pl.program_id(0)
    changed = jnp.logical_or(t == 0, te_ref[t] != te_ref[jnp.maximum(t - 1, 0)])

    @pl.when(changed)
    def _():
        wgu_bf[...] = wgu_ref[...].astype(BF16)
        wdn_bf[...] = wdn_ref[...].astype(BF16)

    @pl.when(tv_ref[t] > 0)
    def _():
        gu = jnp.dot(xs_ref[...].astype(BF16), wgu_bf[...], preferred_element_type=F32)
        act = _silu(gu[:, :D_EXPERT]) * gu[:, D_EXPERT:]
        ys_ref[...] = jnp.dot(act.astype(BF16), wdn_bf[...], preferred_element_type=F32)

    @pl.when(tv_ref[t] == 0)
    def _():
        ys_ref[...] = jnp.zeros_like(ys_ref)


def moe_experts(xs, w_gate_up, w_down, layer, tile_expert, tile_valid, *, tmx):
    r, d = xs.shape
    n_tiles = r // tmx
    grid_spec = pltpu.PrefetchScalarGridSpec(
        num_scalar_prefetch=2, grid=(n_tiles,),
        in_specs=[pl.BlockSpec((tmx, d), lambda t, te, tv: (t, 0)),
                  pl.BlockSpec((None, None, d, 2 * D_EXPERT), lambda t, te, tv: (layer, te[t], 0, 0)),
                  pl.BlockSpec((None, None, D_EXPERT, d), lambda t, te, tv: (layer, te[t], 0, 0))],
        out_specs=pl.BlockSpec((tmx, d), lambda t, te, tv: (t, 0)),
        scratch_shapes=[pltpu.VMEM((d, 2 * D_EXPERT), BF16), pltpu.VMEM((D_EXPERT, d), BF16)])
    return pl.pallas_call(
        _expert_kernel, grid_spec=grid_spec,
        out_shape=jax.ShapeDtypeStruct((r, d), F32),
        compiler_params=_cparams(("arbitrary",)),
        name="moe_experts")(tile_expert, tile_valid, xs, w_gate_up, w_down)


def _combine_kernel(pos_ref, x_ref, info_ref, g_ref, ys_ref, o_ref, buf_ref, sem, *, final_norm):
    tm = x_ref.shape[0]

    def copies(r):
        return [pltpu.make_async_copy(ys_ref.at[pl.ds(pos_ref[0, 0, k * tm + r], 1)],
                                      buf_ref.at[k, pl.ds(r, 1)], sem)
                for k in range(2)]

    _row_copies(tm, copies)
    for k in range(2):
        pltpu.make_async_copy(ys_ref.at[pl.ds(0, tm)], buf_ref.at[k], sem).wait()
    info = info_ref[...]
    out = (x_ref[...] + info[:, _R_W1:_R_W1 + 1] * buf_ref[0]
           + info[:, _R_W2:_R_W2 + 1] * buf_ref[1])
    if final_norm:
        out = _rms(out, g_ref[...])
    o_ref[...] = out


def moe_combine(x, info, pos, ys, g_final, *, tm, final_norm):
    t, d = x.shape
    n = t // tm
    pos_blk = pos.reshape(n, tm, 2).transpose(0, 2, 1).reshape(n, 1, 2 * tm)
    return pl.pallas_call(
        functools.partial(_combine_kernel, final_norm=final_norm), grid=(n,),
        in_specs=[pl.BlockSpec((1, 1, 2 * tm), lambda i: (i, 0, 0), memory_space=pltpu.SMEM),
                  pl.BlockSpec((tm, d), lambda i: (i, 0)),
                  pl.BlockSpec((tm, LANES), lambda i: (i, 0)),
                  pl.BlockSpec((1, d), lambda i: (0, 0)),
                  pl.BlockSpec(memory_space=pl.ANY)],
        out_specs=pl.BlockSpec((tm, d), lambda i: (i, 0)),
        out_shape=jax.ShapeDtypeStruct((t, d), F32),
        scratch_shapes=[pltpu.VMEM((2, tm, d), F32), pltpu.SemaphoreType.DMA(())],
        compiler_params=_cparams(("arbitrary",)),
        name="moe_combine")(pos_blk, x, info, g_final, ys)


def _router_weight(w_group, w_expert):
    d = w_group.shape[0]
    we = jnp.transpose(w_expert, (1, 0, 2)).reshape(d, N_EXPERTS)
    w = jnp.concatenate([we, w_group], axis=1)
    return jnp.pad(w, ((0, 0), (0, LANES - w.shape[1]))).astype(BF16)


def hier_moe_layer(x, norm_g, w_group, w_expert, w_gate_up, w_down, layer, g_final, *, tm, tmx,
                   final_norm):
    t, d = x.shape
    g = norm_g.reshape(1, d)
    info, counts = moe_router(x, g, _router_weight(w_group, w_expert), tm=tm)

    counts = counts[0, :N_EXPERTS].astype(I32)
    padded = ((counts + tmx - 1) // tmx) * tmx
    ends = jnp.cumsum(padded)
    offsets = ends - padded
    e12 = info[:, _R_E1:_R_E2 + 1].astype(I32)
    rank12 = info[:, _R_RANK1:_R_RANK2 + 1].astype(I32)
    pos = offsets[e12] + rank12
    n_rows = 2 * t + N_EXPERTS * tmx
    n_tiles = n_rows // tmx
    tile_start = jnp.arange(n_tiles, dtype=I32) * tmx
    tile_valid = (tile_start < ends[-1]).astype(I32)
    probe = jnp.minimum(tile_start, jnp.maximum(ends[-1] - 1, 0))
    tile_expert = jnp.sum((probe[:, None] >= ends[None, :]).astype(I32), axis=1)
    tile_expert = jnp.minimum(tile_expert, N_EXPERTS - 1)

    depth = w_gate_up.shape[0]
    xs = moe_dispatch(x, g, pos, jnp.zeros((n_rows, d), F32), tm=tm)
    ys = moe_experts(xs, w_gate_up.reshape(depth, N_EXPERTS, d, 2 * D_EXPERT),
                     w_down.reshape(depth, N_EXPERTS, D_EXPERT, d), layer,
                     tile_expert, tile_valid, tmx=tmx)
    return moe_combine(x, info, pos, ys, g_final.reshape(1, d), tm=tm, final_norm=final_norm)


def kernel(x_prompt, x_sample, state_conv, state_ssm, cache_kv_w128, cache_kv_w512, cache_kv_w2048, norm_mix, norm_ffn, norm_final, ssm_in_proj, ssm_conv_w, ssm_conv_b, ssm_dt_bias, ssm_a_log, ssm_d, ssm_norm, ssm_out_proj, attn_qkv, attn_out, moe_route_group, moe_route_expert, moe_w_gate_up, moe_w_down):
    nb, seq, d = x_prompt.shape
    ns = x_sample.shape[0]
    t = nb * seq
    tm = min(t, 1024)
    n_groups = len(ATTN_PATTERNS)
    kv_caches = (cache_kv_w128, cache_kv_w512, cache_kv_w2048)

    ssm_params = (ssm_in_proj[0], ssm_conv_w[0], ssm_conv_b[0], ssm_dt_bias[0], ssm_a_log[0],
                  ssm_d[0], ssm_norm[0], ssm_out_proj[0])
    xp, conv_p, ssm_p = _mamba_prompt(x_prompt, norm_mix[0], *ssm_params)
    xs, conv_s, ssm_s = _mamba_sample(x_sample, norm_mix[0], state_conv[0], state_ssm[0], *ssm_params)

    def moe(x2d, i, tm_moe, tmx, final_norm):
        return hier_moe_layer(x2d, norm_ffn[i], moe_route_group[i], moe_route_expert[i],
                              moe_w_gate_up, moe_w_down, i, norm_final,
                              tm=tm_moe, tmx=tmx, final_norm=final_norm)

    xp = moe(xp.reshape(t, d), 0, 256, 256, False)
    xs = moe(xs.reshape(ns, d), 0, ns, 16, False)

    g1 = norm_mix[1].reshape(1, d)
    w_qkv = attn_qkv[0].astype(BF16)
    w_o = attn_out[0].astype(BF16)
    qkv_p = norm_matmul(xp, g1, w_qkv, tm=tm, tn=1024)
    y_p, kv_p = attn_prompt(qkv_p.reshape(nb, seq, -1), nb, seq)
    xp = matmul_res(y_p.reshape(t, ATTN_WIDTH), w_o, xp, tm=tm)

    qkv_s = norm_matmul(xs, g1, w_qkv, tm=ns, tn=1024)
    qkv_s5 = qkv_s.reshape(ns, n_groups, 3, ATTN_HEADS, ATTN_HEAD_DIM)
    y_s = attn_decode(qkv_s5, [c[0] for c in kv_caches])
    xs = matmul_res(y_s.reshape(ns, ATTN_WIDTH), w_o, xs, tm=ns)
    kv_s = [qkv_s5[:, g, 1:3][:, None] for g in range(n_groups)]

    y_prompt = moe(xp, 1, 256, 256, True).reshape(nb, seq, d)
    y_sample = moe(xs, 1, ns, 16, True).reshape(ns, 1, d)

    return (y_prompt, y_sample, conv_p[None], conv_s[None], ssm_p[None], ssm_s[None],
            kv_p[0][None], kv_s[0][None], kv_p[1][None], kv_s[1][None], kv_p[2][None], kv_s[2][None])
```

```python
import functools
import math

import jax
import jax.numpy as jnp
from jax import lax
from jax.experimental import pallas as pl
from jax.experimental.pallas import tpu as pltpu

F32 = jnp.float32
BF16 = jnp.bfloat16
I32 = jnp.int32

EPS = 1e-6
D_MODEL = 1024
LANES = 128
SUBLANES = 8

SSM_D_INNER = 2048
SSM_HEAD_DIM = 64
SSM_HEADS = 32
SSM_GROUPS = 4
SSM_HEADS_PER_GROUP = 8
SSM_D_STATE = 128
CONV_WIDTH = 4
SSD_CHUNK = 128
SSM_BC_DIM = 2 * SSM_GROUPS * SSM_D_STATE
SSM_CONV_DIM = SSM_D_INNER + SSM_BC_DIM

ATTN_PATTERNS = ((128, 1), (512, 4), (2048, 16))
ATTN_HEAD_DIM = 64
ATTN_HEADS = 16
ATTN_WIDTH = 1024
ATTN_BLOCK = 128
ATTN_SCALE = ATTN_HEAD_DIM ** -0.5

N_EXPERT_GROUPS = 4
EXPERTS_PER_GROUP = 8
N_EXPERTS = N_EXPERT_GROUPS * EXPERTS_PER_GROUP
D_EXPERT = 512
GROUP_LOGIT_LANE = N_EXPERTS

VMEM_LIMIT = 56 * 1024 * 1024


def _cparams(sem):
    return pltpu.CompilerParams(dimension_semantics=sem, vmem_limit_bytes=VMEM_LIMIT)


def _rms(x, g):
    ms = jnp.mean(x * x, axis=-1, keepdims=True)
    return x * lax.rsqrt(ms + EPS) * g


def _silu(x):
    return x * jax.nn.sigmoid(x)


def _softplus(x):
    return jnp.maximum(x, 0.0) + jnp.log1p(jnp.exp(-jnp.abs(x)))


def _mm(a, b):
    prec = lax.Precision.HIGHEST if b.dtype == F32 else None
    return jnp.dot(a.astype(b.dtype), b, preferred_element_type=F32, precision=prec)


def _norm_matmul_kernel(x_ref, g_ref, w_ref, o_ref, xn_ref):
    @pl.when(pl.program_id(1) == 0)
    def _():
        xn_ref[...] = _rms(x_ref[...], g_ref[...]).astype(xn_ref.dtype)

    o_ref[...] = _mm(xn_ref[...], w_ref[...])


def _norm_matmul2_kernel(x_ref, g_ref, w_ref, w2_ref, o_ref, o2_ref, xn_ref):
    @pl.when(pl.program_id(1) == 0)
    def _():
        xn = _rms(x_ref[...], g_ref[...]).astype(xn_ref.dtype)
        xn_ref[...] = xn
        o2_ref[...] = _mm(xn, w2_ref[...])

    o_ref[...] = _mm(xn_ref[...], w_ref[...])


def norm_matmul(x, g, w, w2=None, *, tm, tn):
    t, d = x.shape
    n = w.shape[1]
    grid = (t // tm, n // tn)
    x_spec = pl.BlockSpec((tm, d), lambda i, j: (i, 0))
    g_spec = pl.BlockSpec((1, d), lambda i, j: (0, 0))
    w_spec = pl.BlockSpec((d, tn), lambda i, j: (0, j))
    o_spec = pl.BlockSpec((tm, tn), lambda i, j: (i, j))
    scratch = [pltpu.VMEM((tm, d), w.dtype)]
    if w2 is None:
        return pl.pallas_call(
            _norm_matmul_kernel, grid=grid,
            in_specs=[x_spec, g_spec, w_spec], out_specs=o_spec,
            out_shape=jax.ShapeDtypeStruct((t, n), F32),
            scratch_shapes=scratch,
            compiler_params=_cparams(("parallel", "arbitrary")),
            name="norm_matmul")(x, g, w)
    n2 = w2.shape[1]
    return pl.pallas_call(
        _norm_matmul2_kernel, grid=grid,
        in_specs=[x_spec, g_spec, w_spec, pl.BlockSpec((d, n2), lambda i, j: (0, 0))],
        out_specs=[o_spec, pl.BlockSpec((tm, n2), lambda i, j: (i, 0))],
        out_shape=[jax.ShapeDtypeStruct((t, n), F32), jax.ShapeDtypeStruct((t, n2), F32)],
        scratch_shapes=scratch,
        compiler_params=_cparams(("parallel", "arbitrary")),
        name="norm_matmul2")(x, g, w, w2)


def _matmul_res_kernel(a_ref, w_ref, r_ref, o_ref):
    o_ref[...] = r_ref[...] + _mm(a_ref[...], w_ref[...])


def matmul_res(a, w, res, *, tm):
    t, k = a.shape
    n = w.shape[1]
    return pl.pallas_call(
        _matmul_res_kernel, grid=(t // tm,),
        in_specs=[pl.BlockSpec((tm, k), lambda i: (i, 0)),
                  pl.BlockSpec((k, n), lambda i: (0, 0)),
                  pl.BlockSpec((tm, n), lambda i: (i, 0))],
        out_specs=pl.BlockSpec((tm, n), lambda i: (i, 0)),
        out_shape=jax.ShapeDtypeStruct((t, n), F32),
        compiler_params=_cparams(("parallel",)),
        name="matmul_res")(a, w, res)


_CONV_COLS = 512


def _ssd_kernel(z_ref, xs_ref, bc_ref, dt_ref, cw_ref, cb_ref, dtb_ref, alog_ref,
                dsk_ref, ng_ref,
                y_ref, tail_out_ref, st_ref,
                tail_ref, h_ref, xc_ref, xst_ref, yt_ref, dtt_ref, cst_ref, xde_ref, yn_ref):
    q = SSD_CHUNK
    c = pl.program_id(1)

    @pl.when(c == 0)
    def _():
        tail_ref[...] = jnp.zeros_like(tail_ref)
        h_ref[...] = jnp.zeros_like(h_ref)

    row = lax.broadcasted_iota(I32, (q, _CONV_COLS), 0)
    for k in range(SSM_CONV_DIM // _CONV_COLS):
        lo = k * _CONV_COLS
        if lo < SSM_D_INNER:
            src = xs_ref[0, :, lo:lo + _CONV_COLS]
        else:
            src = bc_ref[0, :, lo - SSM_D_INNER:lo - SSM_D_INNER + _CONV_COLS]
        tl = tail_ref[:, lo:lo + _CONV_COLS]
        w = cw_ref[:, lo:lo + _CONV_COLS]
        acc = src * w[CONV_WIDTH - 1:CONV_WIDTH] + cb_ref[:, lo:lo + _CONV_COLS]
        for s in range(1, CONV_WIDTH):
            cur = pltpu.roll(src, s, axis=0)
            prev = jnp.tile(pltpu.roll(tl, s, axis=0), (q // SUBLANES, 1))
            shifted = jnp.where(row < s, prev, cur)
            acc = acc + shifted * w[CONV_WIDTH - 1 - s:CONV_WIDTH - s]
        xc_ref[:, lo:lo + _CONV_COLS] = _silu(acc)
        tail_ref[:, lo:lo + _CONV_COLS] = src[q - SUBLANES:, :]

    for k in range(SSM_D_INNER // LANES):
        xst_ref[k * LANES:(k + 1) * LANES, :] = xc_ref[:, k * LANES:(k + 1) * LANES].T

    dt = _softplus(dt_ref[0] + dtb_ref[...])
    da = dt * (-jnp.exp(alog_ref[...]))
    ri = lax.broadcasted_iota(I32, (q, q), 0)
    ci = lax.broadcasted_iota(I32, (q, q), 1)
    tril = (ci <= ri).astype(F32)
    cs = jnp.dot(tril, da, preferred_element_type=F32, precision=lax.Precision.HIGHEST)
    dtt_ref[...] = dt.T
    cst_ref[...] = cs.T

    causal_t = ci >= ri
    hpg = SSM_HEADS_PER_GROUP
    gp = hpg * SSM_HEAD_DIM
    for g in range(SSM_GROUPS):
        b0 = SSM_D_INNER + g * SSM_D_STATE
        c0 = SSM_D_INNER + SSM_GROUPS * SSM_D_STATE + g * SSM_D_STATE
        bm = xc_ref[:, b0:b0 + SSM_D_STATE].astype(BF16)
        cm = xc_ref[:, c0:c0 + SSM_D_STATE].astype(BF16)
        cbt = lax.dot_general(bm, cm, (((1,), (1,)), ((), ())), preferred_element_type=F32)
        hg = h_ref[g * gp:(g + 1) * gp, :].astype(BF16)
        yt_ref[g * gp:(g + 1) * gp, :] = lax.dot_general(
            hg, cm, (((1,), (1,)), ((), ())), preferred_element_type=F32)

        for e in range(hpg):
            hd = g * hpg + e
            rows = slice(hd * SSM_HEAD_DIM, (hd + 1) * SSM_HEAD_DIM)
            csr = cst_ref[hd:hd + 1, :]
            rowb = jnp.broadcast_to(csr, (q, q))
            colb = rowb.T
            dec = jnp.exp(jnp.where(causal_t, rowb - colb, -jnp.inf))
            mt = (cbt * dec).astype(BF16)
            dtr = dtt_ref[hd:hd + 1, :]
            xsh = xst_ref[rows, :]
            xdt = xsh * dtr
            ydt = jnp.dot(xdt.astype(BF16), mt, preferred_element_type=F32)
            cs_end = csr[:, q - 1:q]
            yt_ref[rows, :] = yt_ref[rows, :] * jnp.exp(csr) + ydt + xsh * dsk_ref[rows, :]
            xde_ref[e * SSM_HEAD_DIM:(e + 1) * SSM_HEAD_DIM, :] = (
                xdt * jnp.exp(cs_end - csr)).astype(BF16)
            h_ref[rows, :] = h_ref[rows, :] * jnp.exp(cs_end)
        h_ref[g * gp:(g + 1) * gp, :] = h_ref[g * gp:(g + 1) * gp, :] + jnp.dot(
            xde_ref[...], bm, preferred_element_type=F32)

    ssq = jnp.zeros((q, 1), F32)
    for k in range(SSM_D_INNER // LANES):
        yk = yt_ref[k * LANES:(k + 1) * LANES, :].T * _silu(z_ref[0, :, k * LANES:(k + 1) * LANES])
        yn_ref[:, k * LANES:(k + 1) * LANES] = yk
        ssq = ssq + jnp.sum(yk * yk, axis=-1, keepdims=True)
    scale = lax.rsqrt(ssq * (1.0 / SSM_D_INNER) + EPS)
    y_ref[0] = (yn_ref[...] * scale * ng_ref[...]).astype(BF16)

    @pl.when(c == pl.num_programs(1) - 1)
    def _():
        tail_out_ref[0] = tail_ref[...]
        st_ref[0] = h_ref[...]


def ssd_prompt(proj, dt_raw, conv_w, conv_b, dt_bias, a_log, d_rows, norm_g, nb, seq):
    nc = seq // SSD_CHUNK
    q = SSD_CHUNK
    const2 = lambda b, c: (0, 0)
    return pl.pallas_call(
        _ssd_kernel, grid=(nb, nc),
        in_specs=[
            pl.BlockSpec((1, q, SSM_D_INNER), lambda b, c: (b, c, 0)),
            pl.BlockSpec((1, q, SSM_D_INNER), lambda b, c: (b, c, 1)),
            pl.BlockSpec((1, q, SSM_BC_DIM), lambda b, c: (b, c, 2 * SSM_D_INNER // SSM_BC_DIM)),
            pl.BlockSpec((1, q, LANES), lambda b, c: (b, c, 0)),
            pl.BlockSpec((CONV_WIDTH, SSM_CONV_DIM), const2),
            pl.BlockSpec((1, SSM_CONV_DIM), const2),
            pl.BlockSpec((1, LANES), const2),
            pl.BlockSpec((1, LANES), const2),
            pl.BlockSpec((SSM_D_INNER, LANES), const2),
            pl.BlockSpec((1, SSM_D_INNER), const2),
        ],
        out_specs=[
            pl.BlockSpec((1, q, SSM_D_INNER), lambda b, c: (b, c, 0)),
            pl.BlockSpec((1, SUBLANES, SSM_CONV_DIM), lambda b, c: (b, 0, 0)),
            pl.BlockSpec((1, SSM_D_INNER, SSM_D_STATE), lambda b, c: (b, 0, 0)),
        ],
        out_shape=[
            jax.ShapeDtypeStruct((nb, seq, SSM_D_INNER), BF16),
            jax.ShapeDtypeStruct((nb, SUBLANES, SSM_CONV_DIM), F32),
            jax.ShapeDtypeStruct((nb, SSM_D_INNER, SSM_D_STATE), F32),
        ],
        scratch_shapes=[
            pltpu.VMEM((SUBLANES, SSM_CONV_DIM), F32),
            pltpu.VMEM((SSM_D_INNER, SSM_D_STATE), F32),
            pltpu.VMEM((q, SSM_CONV_DIM), F32),
            pltpu.VMEM((SSM_D_INNER, q), F32),
            pltpu.VMEM((SSM_D_INNER, q), F32),
            pltpu.VMEM((LANES, q), F32),
            pltpu.VMEM((LANES, q), F32),
            pltpu.VMEM((SSM_HEADS_PER_GROUP * SSM_HEAD_DIM, q), BF16),
            pltpu.VMEM((q, SSM_D_INNER), F32),
        ],
        compiler_params=_cparams(("parallel", "arbitrary")),
        name="ssd_prompt")(proj, proj, proj, dt_raw, conv_w, conv_b, dt_bias, a_log, d_rows, norm_g)


def _pad_lanes(v, n=LANES):
    return jnp.pad(v.astype(F32), (0, n - v.shape[0])).reshape(1, n)


def _ssm_weights(w_in, dtype):
    n_main = SSM_D_INNER + SSM_CONV_DIM
    w_main = w_in[:, :n_main].astype(dtype)
    w_dt = jnp.pad(w_in[:, n_main:], ((0, 0), (0, LANES - SSM_HEADS))).astype(dtype)
    return w_main, w_dt


def _mamba_prompt(x, norm_g, w_in, conv_w, conv_b, dt_bias, a_log, d_skip, ssm_norm, w_out):
    nb, seq, d = x.shape
    t = nb * seq
    w_main, w_dt = _ssm_weights(w_in, BF16)
    tm = min(t, 1024)
    proj, dt_raw = norm_matmul(x.reshape(t, d), norm_g.reshape(1, d), w_main, w_dt, tm=tm, tn=1024)
    d_rows = jnp.broadcast_to(jnp.repeat(d_skip.astype(F32), SSM_HEAD_DIM)[:, None], (SSM_D_INNER, LANES))
    y, tail, st = ssd_prompt(
        proj.reshape(nb, seq, -1), dt_raw.reshape(nb, seq, LANES),
        conv_w, conv_b.reshape(1, -1), _pad_lanes(dt_bias), _pad_lanes(a_log), d_rows,
        ssm_norm.reshape(1, -1), nb, seq)
    x_new = matmul_res(y.reshape(t, SSM_D_INNER), w_out.astype(BF16), x.reshape(t, d), tm=tm)
    conv_state = tail[:, SUBLANES - (CONV_WIDTH - 1):]
    ssm_state = st.reshape(nb, SSM_HEADS, SSM_HEAD_DIM, SSM_D_STATE)
    return x_new.reshape(nb, seq, d), conv_state, ssm_state


_DEC_HEADS = 4


def _ssd_decode_kernel(zt_ref, xbct_ref, convt_ref, dtt_raw_ref, cwt_ref, cbt_ref, dtbt_ref,
                       alogt_ref, dsk_ref, ngt_ref, wout_ref, res_ref, st_ref,
                       so_ref, out_ref,
                       xct_ref, dtt_ref, dect_ref, bcn_ref, yt_ref, *, nb):
    step = pl.program_id(0)
    rb = 512

    @pl.when(step == 0)
    def _():
        for k in range(SSM_CONV_DIM // rb):
            rows = slice(k * rb, (k + 1) * rb)
            acc = xbct_ref[rows, :] * cwt_ref[rows, CONV_WIDTH - 1:CONV_WIDTH] + cbt_ref[rows, 0:1]
            for j in range(CONV_WIDTH - 1):
                acc = acc + convt_ref[j, rows, :] * cwt_ref[rows, j:j + 1]
            xct_ref[rows, :] = _silu(acc)
        dt = _softplus(dtt_raw_ref[...] + dtbt_ref[:, 0:1])
        dtt_ref[...] = dt
        dect_ref[...] = jnp.exp(dt * (-jnp.exp(alogt_ref[:, 0:1])))
        for k in range(SSM_BC_DIM // LANES):
            bcn_ref[:, k * LANES:(k + 1) * LANES] = xct_ref[
                SSM_D_INNER + k * LANES:SSM_D_INNER + (k + 1) * LANES, :].T
        yt_ref[...] = jnp.zeros_like(yt_ref)

    for hh in range(_DEC_HEADS):
        hd = step * _DEC_HEADS + hh
        r0 = pl.multiple_of(hd * SSM_HEAD_DIM, SSM_HEAD_DIM)
        grp = hd // SSM_HEADS_PER_GROUP
        b_lane = pl.multiple_of(grp * SSM_D_STATE, SSM_D_STATE)
        c_lane = pl.multiple_of(SSM_GROUPS * SSM_D_STATE + grp * SSM_D_STATE, SSM_D_STATE)
        xh = xct_ref[pl.ds(r0, SSM_HEAD_DIM), :]
        dth = dtt_ref[pl.ds(hd, 1), :]
        dech = dect_ref[pl.ds(hd, 1), :]
        xdt = xh * dth
        for b in range(nb):
            brow = bcn_ref[b:b + 1, pl.ds(b_lane, SSM_D_STATE)]
            crow = bcn_ref[b:b + 1, pl.ds(c_lane, SSM_D_STATE)]
            hn = st_ref[b, hh] * dech[:, b:b + 1] + xdt[:, b:b + 1] * brow
            so_ref[b, hh] = hn
            yt_ref[pl.ds(r0, SSM_HEAD_DIM), b:b + 1] = jnp.sum(hn * crow, axis=-1, keepdims=True)

    @pl.when(step == pl.num_programs(0) - 1)
    def _():
        ssq = jnp.zeros((1, LANES), F32)
        for k in range(SSM_D_INNER // rb):
            rows = slice(k * rb, (k + 1) * rb)
            y = (yt_ref[rows, :] + xct_ref[rows, :] * dsk_ref[rows, :]) * _silu(zt_ref[rows, :])
            yt_ref[rows, :] = y
            ssq = ssq + jnp.sum(y * y, axis=0, keepdims=True)
        scale = lax.rsqrt(ssq * (1.0 / SSM_D_INNER) + EPS)
        acc = res_ref[...]
        for k in range(SSM_D_INNER // LANES):
            rows = slice(k * LANES, (k + 1) * LANES)
            yn = (yt_ref[rows, :] * scale * ngt_ref[rows, :]).T
            acc = acc + _mm(yn, wout_ref[rows, :])
        out_ref[...] = acc


def ssd_decode(zt, xbct, convt, dtt_raw, cwt, cbt, dtbt, alogt, d_rows, ngt, w_out, res, state):
    nb = state.shape[0]
    n_steps = SSM_HEADS // _DEC_HEADS
    full = lambda a: pl.BlockSpec(a.shape, lambda s, n=a.ndim: (0,) * n)
    st_spec = pl.BlockSpec((nb, _DEC_HEADS, SSM_HEAD_DIM, SSM_D_STATE), lambda s: (0, s, 0, 0))
    small = (zt, xbct, convt, dtt_raw, cwt, cbt, dtbt, alogt, d_rows, ngt, w_out, res)
    return pl.pallas_call(
        functools.partial(_ssd_decode_kernel, nb=nb), grid=(n_steps,),
        in_specs=[full(a) for a in small] + [st_spec],
        out_specs=[st_spec, full(res)],
        out_shape=[jax.ShapeDtypeStruct(state.shape, F32), jax.ShapeDtypeStruct(res.shape, F32)],
        scratch_shapes=[
            pltpu.VMEM((SSM_CONV_DIM, LANES), F32),
            pltpu.VMEM((LANES, LANES), F32),
            pltpu.VMEM((LANES, LANES), F32),
            pltpu.VMEM((LANES, SSM_BC_DIM), F32),
            pltpu.VMEM((SSM_D_INNER, LANES), F32),
        ],
        compiler_params=_cparams(("arbitrary",)),
        name="ssd_decode")(*small, state)


def _to_lanes(a, n=LANES):
    return jnp.pad(a.T, ((0, 0), (0, n - a.shape[0])))


def _mamba_sample(x, norm_g, conv_state, ssm_state, w_in, conv_w, conv_b, dt_bias, a_log, d_skip,
                  ssm_norm, w_out):
    nb, _, d = x.shape
    assert nb <= LANES and nb % SUBLANES == 0
    w_main, w_dt = _ssm_weights(w_in, BF16)
    proj, dt_raw = norm_matmul(x.reshape(nb, d), norm_g.reshape(1, d), w_main, w_dt, tm=nb, tn=1024)
    xbc = proj[:, SSM_D_INNER:]
    col = lambda v: jnp.broadcast_to(v.astype(F32)[:, None], (v.shape[0], LANES))
    convt = jnp.pad(jnp.transpose(conv_state, (1, 2, 0)), ((0, 0), (0, 0), (0, LANES - nb)))
    res = jnp.pad(x.reshape(nb, d), ((0, LANES - nb), (0, 0)))
    st, out = ssd_decode(
        _to_lanes(proj[:, :SSM_D_INNER]), _to_lanes(xbc), convt, _to_lanes(dt_raw),
        jnp.pad(conv_w.T, ((0, 0), (0, LANES - CONV_WIDTH))), col(conv_b),
        col(_pad_lanes(dt_bias)[0]), col(_pad_lanes(a_log)[0]),
        col(jnp.repeat(d_skip, SSM_HEAD_DIM)), col(ssm_norm), w_out.astype(BF16), res, ssm_state)
    new_conv = jnp.concatenate([conv_state[:, 1:], xbc[:, None, :]], axis=1)
    return out[:nb].reshape(nb, 1, d), new_conv, st


def _attn_block(q_ref, k_ref, v_ref, o_scr, lam_scr, start, dil, has_prev):
    nq = ATTN_BLOCK
    rows = pl.ds(start, nq, stride=dil) if dil > 1 else pl.ds(start, nq)
    hd = ATTN_HEAD_DIM
    q2 = q_ref[rows, :]
    lane = lax.broadcasted_iota(I32, (nq, LANES), 1)
    qblk = jnp.concatenate([jnp.where(lane < hd, q2, 0.0), jnp.where(lane >= hd, q2, 0.0)],
                           axis=0).astype(BF16)
    kj = lax.broadcasted_iota(I32, (nq, 2 * nq), 0)
    qi = lax.broadcasted_iota(I32, (nq, 2 * nq), 1) % nq
    nt = (((1,), (1,)), ((), ()))
    s = lax.dot_general(k_ref[rows, :].astype(BF16), qblk, nt,
                        preferred_element_type=F32) * ATTN_SCALE
    s = jnp.where(kj <= qi, s, -jnp.inf)
    vt = v_ref[rows, :].T.astype(BF16)
    if has_prev:
        prows = (pl.ds(start - dil * nq, nq, stride=dil) if dil > 1 else pl.ds(start - nq, nq))
        sp = lax.dot_general(k_ref[prows, :].astype(BF16), qblk, nt,
                             preferred_element_type=F32) * ATTN_SCALE
        s = jnp.concatenate([jnp.where(kj >= qi, sp, -jnp.inf), s], axis=0)
        vt = jnp.concatenate([v_ref[prows, :].T.astype(BF16), vt], axis=1)
    m = jnp.max(s, axis=0, keepdims=True)
    p = jnp.exp(s - m)
    l = jnp.sum(p, axis=0, keepdims=True)
    ot = jnp.dot(vt, p.astype(BF16), preferred_element_type=F32) * (1.0 / l)
    lam = jnp.broadcast_to(m + jnp.log(l), (hd, 2 * nq))
    o_scr[rows, :] = jnp.concatenate([ot[:hd, :nq], ot[hd:, nq:]], axis=0).T
    lam_scr[rows, :] = jnp.concatenate([lam[:, :nq], lam[:, nq:]], axis=0).T


def _attn_prompt_kernel(*refs, seq):
    qkv = refs[:9]
    y_ref = refs[9]
    kvt_refs = refs[10:13]
    o_scrs, lam_scrs = refs[13:16], refs[16:19]
    nq = ATTN_BLOCK
    for g, (window, dil) in enumerate(ATTN_PATTERNS):
        q_ref, k_ref, v_ref = qkv[3 * g:3 * g + 3]
        o_scr, lam_scr = o_scrs[g], lam_scrs[g]
        keep = min(window, seq)
        for c in range(keep // LANES):
            src = slice(seq - keep + c * LANES, seq - keep + (c + 1) * LANES)
            kvt_refs[g][0, :, c * LANES:(c + 1) * LANES] = k_ref[src, :].T
            kvt_refs[g][1, :, c * LANES:(c + 1) * LANES] = v_ref[src, :].T
        n_blk = seq // (dil * nq)
        for r in range(dil):
            for n in range(n_blk):
                _attn_block(q_ref, k_ref, v_ref, o_scr, lam_scr, r + dil * nq * n, dil, n > 0)

    rows_per = 256

    def merge(i, carry):
        rows = pl.ds(pl.multiple_of(i * rows_per, rows_per), rows_per)
        l0, l1, l2 = lam_scrs[0][rows, :], lam_scrs[1][rows, :], lam_scrs[2][rows, :]
        mx = jnp.maximum(jnp.maximum(l0, l1), l2)
        e0, e1, e2 = jnp.exp(l0 - mx), jnp.exp(l1 - mx), jnp.exp(l2 - mx)
        num = e0 * o_scrs[0][rows, :] + e1 * o_scrs[1][rows, :] + e2 * o_scrs[2][rows, :]
        y_ref[rows, :] = (num / (e0 + e1 + e2)).astype(BF16)
        return carry

    lax.fori_loop(0, seq // rows_per, merge, 0)


def attn_prompt(qkv, nb, seq):
    for window, dil in ATTN_PATTERNS:
        assert window // dil == ATTN_BLOCK and seq % (dil * ATTN_BLOCK) == 0
    n_pairs = ATTN_WIDTH // LANES
    keeps = [min(window, seq) for window, _ in ATTN_PATTERNS]
    in_specs = []
    for g in range(len(ATTN_PATTERNS)):
        for j in range(3):
            in_specs.append(pl.BlockSpec(
                (None, seq, LANES), lambda b, hp, g=g, j=j: (b, 0, (g * 3 + j) * n_pairs + hp)))
    outs = pl.pallas_call(
        functools.partial(_attn_prompt_kernel, seq=seq),
        grid=(nb, n_pairs),
        in_specs=in_specs,
        out_specs=[pl.BlockSpec((None, seq, LANES), lambda b, hp: (b, 0, hp))] + [
            pl.BlockSpec((None, 2, LANES, keep), lambda b, hp: (b, 0, hp, 0)) for keep in keeps],
        out_shape=[jax.ShapeDtypeStruct((nb, seq, ATTN_WIDTH), BF16)] + [
            jax.ShapeDtypeStruct((nb, 2, ATTN_WIDTH, keep), F32) for keep in keeps],
        scratch_shapes=[pltpu.VMEM((seq, LANES), F32)] * 6,
        compiler_params=_cparams(("parallel", "parallel")),
        name="attn_prompt")(*([qkv] * 9))
    kv = [jnp.transpose(kvt.reshape(nb, 2, ATTN_HEADS, ATTN_HEAD_DIM, keep), (0, 4, 1, 2, 3))
          for kvt, keep in zip(outs[1:], keeps)]
    return outs[0], kv


_DEC_ATTN_HEADS = 4


def _attn_decode_kernel(qkv_ref, c0_ref, c1_ref, c2_ref, y_ref, pad_ref):
    caches = (c0_ref, c1_ref, c2_ref)
    hg = pl.program_id(1)
    hd = ATTN_HEAD_DIM
    rnd = lambda a: a.astype(BF16).astype(F32)
    lane = lax.broadcasted_iota(I32, (hd, LANES), 1)

    def head_columns(a):
        pad_ref[...] = jnp.zeros_like(pad_ref)
        pad_ref[0:ATTN_HEADS, 0:hd] = a
        return pad_ref[...].T[0:hd, :]

    cols = [[rnd(head_columns(qkv_ref[g, j])) for j in range(3)] for g in range(len(ATTN_PATTERNS))]
    acc = jnp.zeros((hd, LANES), F32)
    for hh in range(_DEC_ATTN_HEADS):
        h = hg * _DEC_ATTN_HEADS + hh
        pick = lambda a: jnp.sum(jnp.where(lane == h, a, 0.0), axis=-1, keepdims=True)
        outs, lams = [], []
        for g, (window, dil) in enumerate(ATTN_PATTERNS):
            qc, knc, vnc = (pick(a) for a in cols[g])
            kt = rnd(caches[g][0, hh])
            vt = rnd(caches[g][1, hh])
            s = jnp.sum(kt * qc, axis=0, keepdims=True) * ATTN_SCALE
            if dil > 1:
                tok = lax.broadcasted_iota(I32, (1, window), 1)
                s = jnp.where(tok % dil == 0, s, -jnp.inf)
            s_new = jnp.sum(knc * qc, axis=0, keepdims=True) * ATTN_SCALE
            m = jnp.maximum(jnp.max(s, axis=-1, keepdims=True), s_new)
            p = jnp.exp(s - m)
            p_new = jnp.exp(s_new - m)
            l = jnp.sum(p, axis=-1, keepdims=True) + p_new
            o = (jnp.sum(vt * rnd(p), axis=-1, keepdims=True) + rnd(p_new) * vnc) / l
            outs.append(o)
            lams.append(m + jnp.log(l))
        mx = jnp.maximum(jnp.maximum(lams[0], lams[1]), lams[2])
        es = [jnp.exp(lm - mx) for lm in lams]
        y = (es[0] * outs[0] + es[1] * outs[1] + es[2] * outs[2]) / (es[0] + es[1] + es[2])
        acc = jnp.where(lane == hh, y, acc)
    y_ref[...] = acc


def attn_decode(qkv_s, caches):
    nb = qkv_s.shape[0]
    hpb = _DEC_ATTN_HEADS
    in_specs = [pl.BlockSpec((None, 3, 3, ATTN_HEADS, ATTN_HEAD_DIM), lambda b, g: (b, 0, 0, 0, 0))]
    args = [qkv_s]
    for c, (window, dil) in zip(caches, ATTN_PATTERNS):
        assert c.shape[1] == window and window % dil == 0, "cache must hold a full window"
        args.append(jnp.transpose(c, (0, 2, 3, 4, 1)))
        in_specs.append(pl.BlockSpec((None, 2, hpb, ATTN_HEAD_DIM, window),
                                     lambda b, g: (b, 0, g, 0, 0)))
    y = pl.pallas_call(
        _attn_decode_kernel, grid=(nb, ATTN_HEADS // hpb),
        in_specs=in_specs,
        out_specs=pl.BlockSpec((None, None, ATTN_HEAD_DIM, LANES), lambda b, g: (b, g, 0, 0)),
        out_shape=jax.ShapeDtypeStruct((nb, ATTN_HEADS // hpb, ATTN_HEAD_DIM, LANES), F32),
        scratch_shapes=[pltpu.VMEM((LANES, LANES), F32)],
        compiler_params=_cparams(("parallel", "arbitrary")),
        name="attn_decode")(*args)
    return jnp.transpose(y[..., :hpb], (0, 1, 3, 2)).reshape(nb, ATTN_WIDTH)


_R_E1, _R_E2, _R_RANK1, _R_RANK2, _R_W1, _R_W2 = range(6)


def _router_kernel(x_ref, g_ref, wr_ref, info_ref, cnt_ref, carry_ref):
    tm = x_ref.shape[0]

    @pl.when(pl.program_id(0) == 0)
    def _():
        carry_ref[...] = jnp.zeros_like(carry_ref)

    xn = _rms(x_ref[...], g_ref[...])
    logits = _mm(xn, wr_ref[...])
    lane = lax.broadcasted_iota(I32, (tm, LANES), 1)
    lanef = lane.astype(F32)
    big = float(LANES)
    neg = -jnp.inf

    is_g = (lane >= GROUP_LOGIT_LANE) & (lane < GROUP_LOGIT_LANE + N_EXPERT_GROUPS)
    gl = jnp.where(is_g, logits, neg)
    gmax = jnp.max(gl, axis=-1, keepdims=True)
    grp = jnp.min(jnp.where(gl == gmax, lanef, big), axis=-1, keepdims=True) - GROUP_LOGIT_LANE
    g_gate = 1.0 / jnp.sum(jnp.exp(gl - gmax), axis=-1, keepdims=True)

    lo = grp * EXPERTS_PER_GROUP
    in_grp = (lanef >= lo) & (lanef < lo + EXPERTS_PER_GROUP)
    el = jnp.where(in_grp, logits, neg)
    m1 = jnp.max(el, axis=-1, keepdims=True)
    i1 = jnp.min(jnp.where(el == m1, lanef, big), axis=-1, keepdims=True)
    el2 = jnp.where(lanef == i1, neg, el)
    m2 = jnp.max(el2, axis=-1, keepdims=True)
    i2 = jnp.min(jnp.where(el2 == m2, lanef, big), axis=-1, keepdims=True)
    ratio = jnp.exp(m2 - m1)
    w1 = g_gate / (1.0 + ratio)
    w2 = g_gate * ratio / (1.0 + ratio)

    oh1 = lanef == i1
    oh2 = lanef == i2
    a = (oh1 | oh2).astype(BF16)
    ri = lax.broadcasted_iota(I32, (tm, tm), 0)
    ci = lax.broadcasted_iota(I32, (tm, tm), 1)
    before = (ci < ri).astype(BF16)
    pref = jnp.dot(before, a, preferred_element_type=F32) + carry_ref[...]
    rank1 = jnp.sum(jnp.where(oh1, pref, 0.0), axis=-1, keepdims=True)
    rank2 = jnp.sum(jnp.where(oh2, pref, 0.0), axis=-1, keepdims=True)
    carry_ref[...] = carry_ref[...] + jnp.sum(a.astype(F32), axis=0, keepdims=True)

    info = jnp.zeros((tm, LANES), F32)
    for ln, val in ((_R_E1, i1), (_R_E2, i2), (_R_RANK1, rank1), (_R_RANK2, rank2),
                    (_R_W1, w1), (_R_W2, w2)):
        info = jnp.where(lane == ln, val, info)
    info_ref[...] = info
    cnt_ref[...] = carry_ref[...]


def moe_router(x, g, wr, *, tm):
    t, d = x.shape
    return pl.pallas_call(
        _router_kernel, grid=(t // tm,),
        in_specs=[pl.BlockSpec((tm, d), lambda i: (i, 0)),
                  pl.BlockSpec((1, d), lambda i: (0, 0)),
                  pl.BlockSpec((d, LANES), lambda i: (0, 0))],
        out_specs=[pl.BlockSpec((tm, LANES), lambda i: (i, 0)),
                   pl.BlockSpec((1, LANES), lambda i: (0, 0))],
        out_shape=[jax.ShapeDtypeStruct((t, LANES), F32), jax.ShapeDtypeStruct((1, LANES), F32)],
        scratch_shapes=[pltpu.VMEM((1, LANES), F32)],
        compiler_params=_cparams(("arbitrary",)),
        name="moe_router")(x, g, wr)


_ROW_UNROLL = 8


def _row_copies(n_rows, make_copies):
    def body(i, carry):
        for u in range(_ROW_UNROLL):
            for cp in make_copies(i * _ROW_UNROLL + u):
                cp.start()
        return carry
    lax.fori_loop(0, n_rows // _ROW_UNROLL, body, 0)


def _dispatch_kernel(pos_ref, x_ref, g_ref, xs_in_ref, xs_ref, xn_ref, sem):
    del xs_in_ref
    tm = x_ref.shape[0]
    xn_ref[...] = _rms(x_ref[...], g_ref[...])

    def copies(r):
        src = xn_ref.at[pl.ds(r, 1)]
        return [pltpu.make_async_copy(src, xs_ref.at[pl.ds(pos_ref[0, 0, k * tm + r], 1)], sem)
                for k in range(2)]

    _row_copies(tm, copies)
    for _ in range(2):
        pltpu.make_async_copy(xn_ref, xs_ref.at[pl.ds(0, tm)], sem).wait()


def moe_dispatch(x, g, pos, xs_init, *, tm):
    t, d = x.shape
    n = t // tm
    pos_blk = pos.reshape(n, tm, 2).transpose(0, 2, 1).reshape(n, 1, 2 * tm)
    return pl.pallas_call(
        _dispatch_kernel, grid=(n,),
        in_specs=[pl.BlockSpec((1, 1, 2 * tm), lambda i: (i, 0, 0), memory_space=pltpu.SMEM),
                  pl.BlockSpec((tm, d), lambda i: (i, 0)),
                  pl.BlockSpec((1, d), lambda i: (0, 0)),
                  pl.BlockSpec(memory_space=pl.ANY)],
        out_specs=pl.BlockSpec(memory_space=pl.ANY),
        out_shape=jax.ShapeDtypeStruct(xs_init.shape, F32),
        scratch_shapes=[pltpu.VMEM((tm, d), F32), pltpu.SemaphoreType.DMA(())],
        input_output_aliases={3: 0},
        compiler_params=_cparams(("arbitrary",)),
        name="moe_dispatch")(pos_blk, x, g, xs_init)


def _expert_kernel(te_ref, tv_ref, xs_ref, wgu_ref, wdn_ref, ys_ref, wgu_bf, wdn_bf):
    t = pl.program_id(0)
    changed = jnp.logical_or(t == 0, te_ref[t] != te_ref[jnp.maximum(t - 1, 0)])

    @pl.when(changed)
    def _():
        wgu_bf[...] = wgu_ref[...].astype(BF16)
        wdn_bf[...] = wdn_ref[...].astype(BF16)

    @pl.when(tv_ref[t] > 0)
    def _():
        gu = jnp.dot(xs_ref[...].astype(BF16), wgu_bf[...], preferred_element_type=F32)
        act = _silu(gu[:, :D_EXPERT]) * gu[:, D_EXPERT:]
        ys_ref[...] = jnp.dot(act.astype(BF16), wdn_bf[...], preferred_element_type=F32)

    @pl.when(tv_ref[t] == 0)
    def _():
        ys_ref[...] = jnp.zeros_like(ys_ref)


def moe_experts(xs, w_gate_up, w_down, layer, tile_expert, tile_valid, *, tmx):
    r, d = xs.shape
    n_tiles = r // tmx
    grid_spec = pltpu.PrefetchScalarGridSpec(
        num_scalar_prefetch=2, grid=(n_tiles,),
        in_specs=[pl.BlockSpec((tmx, d), lambda t, te, tv: (t, 0)),
                  pl.BlockSpec((None, None, d, 2 * D_EXPERT), lambda t, te, tv: (layer, te[t], 0, 0)),
                  pl.BlockSpec((None, None, D_EXPERT, d), lambda t, te, tv: (layer, te[t], 0, 0))],
        out_specs=pl.BlockSpec((tmx, d), lambda t, te, tv: (t, 0)),
        scratch_shapes=[pltpu.VMEM((d, 2 * D_EXPERT), BF16), pltpu.VMEM((D_EXPERT, d), BF16)])
    return pl.pallas_call(
        _expert_kernel, grid_spec=grid_spec,
        out_shape=jax.ShapeDtypeStruct((r, d), F32),
        compiler_params=_cparams(("arbitrary",)),
        name="moe_experts")(tile_expert, tile_valid, xs, w_gate_up, w_down)


def _combine_kernel(pos_ref, x_ref, info_ref, g_ref, ys_ref, o_ref, buf_ref, sem, *, final_norm):
    tm = x_ref.shape[0]

    def copies(r):
        return [pltpu.make_async_copy(ys_ref.at[pl.ds(pos_ref[0, 0, k * tm + r], 1)],
                                      buf_ref.at[k, pl.ds(r, 1)], sem)
                for k in range(2)]

    _row_copies(tm, copies)
    for k in range(2):
        pltpu.make_async_copy(ys_ref.at[pl.ds(0, tm)], buf_ref.at[k], sem).wait()
    info = info_ref[...]
    out = (x_ref[...] + info[:, _R_W1:_R_W1 + 1] * buf_ref[0]
           + info[:, _R_W2:_R_W2 + 1] * buf_ref[1])
    if final_norm:
        out = _rms(out, g_ref[...])
    o_ref[...] = out


def moe_combine(x, info, pos, ys, g_final, *, tm, final_norm):
    t, d = x.shape
    n = t // tm
    pos_blk = pos.reshape(n, tm, 2).transpose(0, 2, 1).reshape(n, 1, 2 * tm)
    return pl.pallas_call(
        functools.partial(_combine_kernel, final_norm=final_norm), grid=(n,),
        in_specs=[pl.BlockSpec((1, 1, 2 * tm), lambda i: (i, 0, 0), memory_space=pltpu.SMEM),
                  pl.BlockSpec((tm, d), lambda i: (i, 0)),
                  pl.BlockSpec((tm, LANES), lambda i: (i, 0)),
                  pl.BlockSpec((1, d), lambda i: (0, 0)),
                  pl.BlockSpec(memory_space=pl.ANY)],
        out_specs=pl.BlockSpec((tm, d), lambda i: (i, 0)),
        out_shape=jax.ShapeDtypeStruct((t, d), F32),
        scratch_shapes=[pltpu.VMEM((2, tm, d), F32), pltpu.SemaphoreType.DMA(())],
        compiler_params=_cparams(("arbitrary",)),
        name="moe_combine")(pos_blk, x, info, g_final, ys)


def _router_weight(w_group, w_expert):
    d = w_group.shape[0]
    we = jnp.transpose(w_expert, (1, 0, 2)).reshape(d, N_EXPERTS)
    w = jnp.concatenate([we, w_group], axis=1)
    return jnp.pad(w, ((0, 0), (0, LANES - w.shape[1]))).astype(BF16)


def hier_moe_layer(x, norm_g, w_group, w_expert, w_gate_up, w_down, layer, g_final, *, tm, tmx,
                   final_norm):
    t, d = x.shape
    g = norm_g.reshape(1, d)
    info, counts = moe_router(x, g, _router_weight(w_group, w_expert), tm=tm)

    counts = counts[0, :N_EXPERTS].astype(I32)
    padded = ((counts + tmx - 1) // tmx) * tmx
    ends = jnp.cumsum(padded)
    offsets = ends - padded
    e12 = info[:, _R_E1:_R_E2 + 1].astype(I32)
    rank12 = info[:, _R_RANK1:_R_RANK2 + 1].astype(I32)
    pos = offsets[e12] + rank12
    n_rows = 2 * t + N_EXPERTS * tmx
    n_tiles = n_rows // tmx
    tile_start = jnp.arange(n_tiles, dtype=I32) * tmx
    tile_valid = (tile_start < ends[-1]).astype(I32)
    probe = jnp.minimum(tile_start, jnp.maximum(ends[-1] - 1, 0))
    tile_expert = jnp.sum((probe[:, None] >= ends[None, :]).astype(I32), axis=1)
    tile_expert = jnp.minimum(tile_expert, N_EXPERTS - 1)

    depth = w_gate_up.shape[0]
    xs = moe_dispatch(x, g, pos, jnp.zeros((n_rows, d), F32), tm=tm)
    ys = moe_experts(xs, w_gate_up.reshape(depth, N_EXPERTS, d, 2 * D_EXPERT),
                     w_down.reshape(depth, N_EXPERTS, D_EXPERT, d), layer,
                     tile_expert, tile_valid, tmx=tmx)
    return moe_combine(x, info, pos, ys, g_final.reshape(1, d), tm=tm, final_norm=final_norm)


def kernel(x_prompt, x_sample, state_conv, state_ssm, cache_kv_w128, cache_kv_w512, cache_kv_w2048, norm_mix, norm_ffn, norm_final, ssm_in_proj, ssm_conv_w, ssm_conv_b, ssm_dt_bias, ssm_a_log, ssm_d, ssm_norm, ssm_out_proj, attn_qkv, attn_out, moe_route_group, moe_route_expert, moe_w_gate_up, moe_w_down):
    nb, seq, d = x_prompt.shape
    ns = x_sample.shape[0]
    t = nb * seq
    tm = min(t, 1024)
    n_groups = len(ATTN_PATTERNS)
    kv_caches = (cache_kv_w128, cache_kv_w512, cache_kv_w2048)

    ssm_params = (ssm_in_proj[0], ssm_conv_w[0], ssm_conv_b[0], ssm_dt_bias[0], ssm_a_log[0],
                  ssm_d[0], ssm_norm[0], ssm_out_proj[0])
    xp, conv_p, ssm_p = _mamba_prompt(x_prompt, norm_mix[0], *ssm_params)
    xs, conv_s, ssm_s = _mamba_sample(x_sample, norm_mix[0], state_conv[0], state_ssm[0], *ssm_params)

    def moe(x2d, i, tm_moe, tmx, final_norm):
        return hier_moe_layer(x2d, norm_ffn[i], moe_route_group[i], moe_route_expert[i],
                              moe_w_gate_up, moe_w_down, i, norm_final,
                              tm=tm_moe, tmx=tmx, final_norm=final_norm)

    xp = moe(xp.reshape(t, d), 0, 256, 256, False)
    xs = moe(xs.reshape(ns, d), 0, ns, 16, False)

    g1 = norm_mix[1].reshape(1, d)
    w_qkv = attn_qkv[0].astype(BF16)
    w_o = attn_out[0].astype(BF16)
    qkv_p = norm_matmul(xp, g1, w_qkv, tm=tm, tn=1024)
    y_p, kv_p = attn_prompt(qkv_p.reshape(nb, seq, -1), nb, seq)
    xp = matmul_res(y_p.reshape(t, ATTN_WIDTH), w_o, xp, tm=tm)

    qkv_s = norm_matmul(xs, g1, w_qkv, tm=ns, tn=1024)
    qkv_s5 = qkv_s.reshape(ns, n_groups, 3, ATTN_HEADS, ATTN_HEAD_DIM)
    y_s = attn_decode(qkv_s5, [c[0] for c in kv_caches])
    xs = matmul_res(y_s.reshape(ns, ATTN_WIDTH), w_o, xs, tm=ns)
    kv_s = [qkv_s5[:, g, 1:3][:, None] for g in range(n_groups)]

    y_prompt = moe(xp, 1, 256, 256, True).reshape(nb, seq, d)
    y_sample = moe(xs, 1, ns, 16, True).reshape(ns, 1, d)

    return (y_prompt, y_sample, conv_p[None], conv_s[None], ssm_p[None], ssm_s[None],
            kv_p[0][None], kv_s[0][None], kv_p[1][None], kv_s[1][None], kv_p[2][None], kv_s[2][None])
```

```python
import functools
import math

import jax
import jax.numpy as jnp
from jax import lax
from jax.experimental import pallas as pl
from jax.experimental.pallas import tpu as pltpu

F32 = jnp.float32
BF16 = jnp.bfloat16
I32 = jnp.int32

EPS = 1e-6
D_MODEL = 1024
LANES = 128
SUBLANES = 8

SSM_D_INNER = 2048
SSM_HEAD_DIM = 64
SSM_HEADS = 32
SSM_GROUPS = 4
SSM_HEADS_PER_GROUP = 8
SSM_D_STATE = 128
CONV_WIDTH = 4
SSD_CHUNK = 128
SSM_BC_DIM = 2 * SSM_GROUPS * SSM_D_STATE
SSM_CONV_DIM = SSM_D_INNER + SSM_BC_DIM

ATTN_PATTERNS = ((128, 1), (512, 4), (2048, 16))
ATTN_HEAD_DIM = 64
ATTN_HEADS = 16
ATTN_WIDTH = 1024
ATTN_BLOCK = 128
ATTN_SCALE = ATTN_HEAD_DIM ** -0.5

N_EXPERT_GROUPS = 4
EXPERTS_PER_GROUP = 8
N_EXPERTS = N_EXPERT_GROUPS * EXPERTS_PER_GROUP
D_EXPERT = 512
GROUP_LOGIT_LANE = N_EXPERTS

VMEM_LIMIT = 56 * 1024 * 1024


def _cparams(sem):
    return pltpu.CompilerParams(dimension_semantics=sem, vmem_limit_bytes=VMEM_LIMIT)


def _rms(x, g):
    ms = jnp.mean(x * x, axis=-1, keepdims=True)
    return x * lax.rsqrt(ms + EPS) * g


def _silu(x):
    return x * jax.nn.sigmoid(x)


def _softplus(x):
    return jnp.maximum(x, 0.0) + jnp.log1p(jnp.exp(-jnp.abs(x)))


def _mm(a, b):
    prec = lax.Precision.HIGHEST if b.dtype == F32 else None
    return jnp.dot(a.astype(b.dtype), b, preferred_element_type=F32, precision=prec)


def _norm_matmul_kernel(x_ref, g_ref, w_ref, o_ref, xn_ref):
    @pl.when(pl.program_id(1) == 0)
    def _():
        xn_ref[...] = _rms(x_ref[...], g_ref[...]).astype(xn_ref.dtype)

    o_ref[...] = _mm(xn_ref[...], w_ref[...])


def _norm_matmul2_kernel(x_ref, g_ref, w_ref, w2_ref, o_ref, o2_ref, xn_ref):
    @pl.when(pl.program_id(1) == 0)
    def _():
        xn = _rms(x_ref[...], g_ref[...]).astype(xn_ref.dtype)
        xn_ref[...] = xn
        o2_ref[...] = _mm(xn, w2_ref[...])

    o_ref[...] = _mm(xn_ref[...], w_ref[...])


def norm_matmul(x, g, w, w2=None, *, tm, tn):
    t, d = x.shape
    n = w.shape[1]
    grid = (t // tm, n // tn)
    x_spec = pl.BlockSpec((tm, d), lambda i, j: (i, 0))
    g_spec = pl.BlockSpec((1, d), lambda i, j: (0, 0))
    w_spec = pl.BlockSpec((d, tn), lambda i, j: (0, j))
    o_spec = pl.BlockSpec((tm, tn), lambda i, j: (i, j))
    scratch = [pltpu.VMEM((tm, d), w.dtype)]
    if w2 is None:
        return pl.pallas_call(
            _norm_matmul_kernel, grid=grid,
            in_specs=[x_spec, g_spec, w_spec], out_specs=o_spec,
            out_shape=jax.ShapeDtypeStruct((t, n), F32),
            scratch_shapes=scratch,
            compiler_params=_cparams(("parallel", "arbitrary")),
            name="norm_matmul")(x, g, w)
    n2 = w2.shape[1]
    return pl.pallas_call(
        _norm_matmul2_kernel, grid=grid,
        in_specs=[x_spec, g_spec, w_spec, pl.BlockSpec((d, n2), lambda i, j: (0, 0))],
        out_specs=[o_spec, pl.BlockSpec((tm, n2), lambda i, j: (i, 0))],
        out_shape=[jax.ShapeDtypeStruct((t, n), F32), jax.ShapeDtypeStruct((t, n2), F32)],
        scratch_shapes=scratch,
        compiler_params=_cparams(("parallel", "arbitrary")),
        name="norm_matmul2")(x, g, w, w2)


def _matmul_res_kernel(a_ref, w_ref, r_ref, o_ref):
    o_ref[...] = r_ref[...] + _mm(a_ref[...], w_ref[...])


def matmul_res(a, w, res, *, tm):
    t, k = a.shape
    n = w.shape[1]
    return pl.pallas_call(
        _matmul_res_kernel, grid=(t // tm,),
        in_specs=[pl.BlockSpec((tm, k), lambda i: (i, 0)),
                  pl.BlockSpec((k, n), lambda i: (0, 0)),
                  pl.BlockSpec((tm, n), lambda i: (i, 0))],
        out_specs=pl.BlockSpec((tm, n), lambda i: (i, 0)),
        out_shape=jax.ShapeDtypeStruct((t, n), F32),
        compiler_params=_cparams(("parallel",)),
        name="matmul_res")(a, w, res)


_CONV_COLS = 512


def _ssd_kernel(z_ref, xs_ref, bc_ref, dt_ref, cw_ref, cb_ref, dtb_ref, alog_ref,
                dsk_ref, ng_ref,
                y_ref, tail_out_ref, st_ref,
                tail_ref, h_ref, xc_ref, xst_ref, yt_ref, dtt_ref, cst_ref, xde_ref, yn_ref):
    q = SSD_CHUNK
    c = pl.program_id(1)

    @pl.when(c == 0)
    def _():
        tail_ref[...] = jnp.zeros_like(tail_ref)
        h_ref[...] = jnp.zeros_like(h_ref)

    row = lax.broadcasted_iota(I32, (q, _CONV_COLS), 0)
    for k in range(SSM_CONV_DIM // _CONV_COLS):
        lo = k * _CONV_COLS
        if lo < SSM_D_INNER:
            src = xs_ref[0, :, lo:lo + _CONV_COLS]
        else:
            src = bc_ref[0, :, lo - SSM_D_INNER:lo - SSM_D_INNER + _CONV_COLS]
        tl = tail_ref[:, lo:lo + _CONV_COLS]
        w = cw_ref[:, lo:lo + _CONV_COLS]
        acc = src * w[CONV_WIDTH - 1:CONV_WIDTH] + cb_ref[:, lo:lo + _CONV_COLS]
        for s in range(1, CONV_WIDTH):
            cur = pltpu.roll(src, s, axis=0)
            prev = jnp.tile(pltpu.roll(tl, s, axis=0), (q // SUBLANES, 1))
            shifted = jnp.where(row < s, prev, cur)
            acc = acc + shifted * w[CONV_WIDTH - 1 - s:CONV_WIDTH - s]
        xc_ref[:, lo:lo + _CONV_COLS] = _silu(acc)
        tail_ref[:, lo:lo + _CONV_COLS] = src[q - SUBLANES:, :]

    for k in range(SSM_D_INNER // LANES):
        xst_ref[k * LANES:(k + 1) * LANES, :] = xc_ref[:, k * LANES:(k + 1) * LANES].T

    dt = _softplus(dt_ref[0] + dtb_ref[...])
    da = dt * (-jnp.exp(alog_ref[...]))
    ri = lax.broadcasted_iota(I32, (q, q), 0)
    ci = lax.broadcasted_iota(I32, (q, q), 1)
    tril = (ci <= ri).astype(F32)
    cs = jnp.dot(tril, da, preferred_element_type=F32, precision=lax.Precision.HIGHEST)
    dtt_ref[...] = dt.T
    cst_ref[...] = cs.T

    causal_t = ci >= ri
    hpg = SSM_HEADS_PER_GROUP
    gp = hpg * SSM_HEAD_DIM
    for g in range(SSM_GROUPS):
        b0 = SSM_D_INNER + g * SSM_D_STATE
        c0 = SSM_D_INNER + SSM_GROUPS * SSM_D_STATE + g * SSM_D_STATE
        bm = xc_ref[:, b0:b0 + SSM_D_STATE].astype(BF16)
        cm = xc_ref[:, c0:c0 + SSM_D_STATE].astype(BF16)
        cbt = lax.dot_general(bm, cm, (((1,), (1,)), ((), ())), preferred_element_type=F32)
        hg = h_ref[g * gp:(g + 1) * gp, :].astype(BF16)
        yt_ref[g * gp:(g + 1) * gp, :] = lax.dot_general(
            hg, cm, (((1,), (1,)), ((), ())), preferred_element_type=F32)

        for e in range(hpg):
            hd = g * hpg + e
            rows = slice(hd * SSM_HEAD_DIM, (hd + 1) * SSM_HEAD_DIM)
            csr = cst_ref[hd:hd + 1, :]
            rowb = jnp.broadcast_to(csr, (q, q))
            colb = rowb.T
            dec = jnp.exp(jnp.where(causal_t, rowb - colb, -jnp.inf))
            mt = (cbt * dec).astype(BF16)
            dtr = dtt_ref[hd:hd + 1, :]
            xsh = xst_ref[rows, :]
            xdt = xsh * dtr
            ydt = jnp.dot(xdt.astype(BF16), mt, preferred_element_type=F32)
            cs_end = csr[:, q - 1:q]
            yt_ref[rows, :] = yt_ref[rows, :] * jnp.exp(csr) + ydt + xsh * dsk_ref[rows, :]
            xde_ref[e * SSM_HEAD_DIM:(e + 1) * SSM_HEAD_DIM, :] = (
                xdt * jnp.exp(cs_end - csr)).astype(BF16)
            h_ref[rows, :] = h_ref[rows, :] * jnp.exp(cs_end)
        h_ref[g * gp:(g + 1) * gp, :] = h_ref[g * gp:(g + 1) * gp, :] + jnp.dot(
            xde_ref[...], bm, preferred_element_type=F32)

    ssq = jnp.zeros((q, 1), F32)
    for k in range(SSM_D_INNER // LANES):
        yk = yt_ref[k * LANES:(k + 1) * LANES, :].T * _silu(z_ref[0, :, k * LANES:(k + 1) * LANES])
        yn_ref[:, k * LANES:(k + 1) * LANES] = yk
        ssq = ssq + jnp.sum(yk * yk, axis=-1, keepdims=True)
    scale = lax.rsqrt(ssq * (1.0 / SSM_D_INNER) + EPS)
    y_ref[0] = (yn_ref[...] * scale * ng_ref[...]).astype(BF16)

    @pl.when(c == pl.num_programs(1) - 1)
    def _():
        tail_out_ref[0] = tail_ref[...]
        st_ref[0] = h_ref[...]


def ssd_prompt(proj, dt_raw, conv_w, conv_b, dt_bias, a_log, d_rows, norm_g, nb, seq):
    nc = seq // SSD_CHUNK
    q = SSD_CHUNK
    const2 = lambda b, c: (0, 0)
    return pl.pallas_call(
        _ssd_kernel, grid=(nb, nc),
        in_specs=[
            pl.BlockSpec((1, q, SSM_D_INNER), lambda b, c: (b, c, 0)),
            pl.BlockSpec((1, q, SSM_D_INNER), lambda b, c: (b, c, 1)),
            pl.BlockSpec((1, q, SSM_BC_DIM), lambda b, c: (b, c, 2 * SSM_D_INNER // SSM_BC_DIM)),
            pl.BlockSpec((1, q, LANES), lambda b, c: (b, c, 0)),
            pl.BlockSpec((CONV_WIDTH, SSM_CONV_DIM), const2),
            pl.BlockSpec((1, SSM_CONV_DIM), const2),
            pl.BlockSpec((1, LANES), const2),
            pl.BlockSpec((1, LANES), const2),
            pl.BlockSpec((SSM_D_INNER, LANES), const2),
            pl.BlockSpec((1, SSM_D_INNER), const2),
        ],
        out_specs=[
            pl.BlockSpec((1, q, SSM_D_INNER), lambda b, c: (b, c, 0)),
            pl.BlockSpec((1, SUBLANES, SSM_CONV_DIM), lambda b, c: (b, 0, 0)),
            pl.BlockSpec((1, SSM_D_INNER, SSM_D_STATE), lambda b, c: (b, 0, 0)),
        ],
        out_shape=[
            jax.ShapeDtypeStruct((nb, seq, SSM_D_INNER), BF16),
            jax.ShapeDtypeStruct((nb, SUBLANES, SSM_CONV_DIM), F32),
            jax.ShapeDtypeStruct((nb, SSM_D_INNER, SSM_D_STATE), F32),
        ],
        scratch_shapes=[
            pltpu.VMEM((SUBLANES, SSM_CONV_DIM), F32),
            pltpu.VMEM((SSM_D_INNER, SSM_D_STATE), F32),
            pltpu.VMEM((q, SSM_CONV_DIM), F32),
            pltpu.VMEM((SSM_D_INNER, q), F32),
            pltpu.VMEM((SSM_D_INNER, q), F32),
            pltpu.VMEM((LANES, q), F32),
            pltpu.VMEM((LANES, q), F32),
            pltpu.VMEM((SSM_HEADS_PER_GROUP * SSM_HEAD_DIM, q), BF16),
            pltpu.VMEM((q, SSM_D_INNER), F32),
        ],
        compiler_params=_cparams(("parallel", "arbitrary")),
        name="ssd_prompt")(proj, proj, proj, dt_raw, conv_w, conv_b, dt_bias, a_log, d_rows, norm_g)


def _pad_lanes(v, n=LANES):
    return jnp.pad(v.astype(F32), (0, n - v.shape[0])).reshape(1, n)


def _ssm_weights(w_in, dtype):
    n_main = SSM_D_INNER + SSM_CONV_DIM
    w_main = w_in[:, :n_main].astype(dtype)
    w_dt = jnp.pad(w_in[:, n_main:], ((0, 0), (0, LANES - SSM_HEADS))).astype(dtype)
    return w_main, w_dt


def _mamba_prompt(x, norm_g, w_in, conv_w, conv_b, dt_bias, a_log, d_skip, ssm_norm, w_out):
    nb, seq, d = x.shape
    t = nb * seq
    w_main, w_dt = _ssm_weights(w_in, BF16)
    tm = min(t, 1024)
    proj, dt_raw = norm_matmul(x.reshape(t, d), norm_g.reshape(1, d), w_main, w_dt, tm=tm, tn=1024)
    d_rows = jnp.broadcast_to(jnp.repeat(d_skip.astype(F32), SSM_HEAD_DIM)[:, None], (SSM_D_INNER, LANES))
    y, tail, st = ssd_prompt(
        proj.reshape(nb, seq, -1), dt_raw.reshape(nb, seq, LANES),
        conv_w, conv_b.reshape(1, -1), _pad_lanes(dt_bias), _pad_lanes(a_log), d_rows,
        ssm_norm.reshape(1, -1), nb, seq)
    x_new = matmul_res(y.reshape(t, SSM_D_INNER), w_out.astype(BF16), x.reshape(t, d), tm=tm)
    conv_state = tail[:, SUBLANES - (CONV_WIDTH - 1):]
    ssm_state = st.reshape(nb, SSM_HEADS, SSM_HEAD_DIM, SSM_D_STATE)
    return x_new.reshape(nb, seq, d), conv_state, ssm_state


_DEC_HEADS = 4


def _ssd_decode_kernel(zt_ref, xbct_ref, convt_ref, dtt_raw_ref, cwt_ref, cbt_ref, dtbt_ref,
                       alogt_ref, dsk_ref, ngt_ref, wout_ref, res_ref, st_ref,
                       so_ref, out_ref,
                       xct_ref, dtt_ref, dect_ref, bcn_ref, yt_ref, *, nb):
    step = pl.program_id(0)
    rb = 512

    @pl.when(step == 0)
    def _():
        for k in range(SSM_CONV_DIM // rb):
            rows = slice(k * rb, (k + 1) * rb)
            acc = xbct_ref[rows, :] * cwt_ref[rows, CONV_WIDTH - 1:CONV_WIDTH] + cbt_ref[rows, 0:1]
            for j in range(CONV_WIDTH - 1):
                acc = acc + convt_ref[j, rows, :] * cwt_ref[rows, j:j + 1]
            xct_ref[rows, :] = _silu(acc)
        dt = _softplus(dtt_raw_ref[...] + dtbt_ref[:, 0:1])
        dtt_ref[...] = dt
        dect_ref[...] = jnp.exp(dt * (-jnp.exp(alogt_ref[:, 0:1])))
        for k in range(SSM_BC_DIM // LANES):
            bcn_ref[:, k * LANES:(k + 1) * LANES] = xct_ref[
                SSM_D_INNER + k * LANES:SSM_D_INNER + (k + 1) * LANES, :].T
        yt_ref[...] = jnp.zeros_like(yt_ref)

    for hh in range(_DEC_HEADS):
        hd = step * _DEC_HEADS + hh
        r0 = pl.multiple_of(hd * SSM_HEAD_DIM, SSM_HEAD_DIM)
        grp = hd // SSM_HEADS_PER_GROUP
        b_lane = pl.multiple_of(grp * SSM_D_STATE, SSM_D_STATE)
        c_lane = pl.multiple_of(SSM_GROUPS * SSM_D_STATE + grp * SSM_D_STATE, SSM_D_STATE)
        xh = xct_ref[pl.ds(r0, SSM_HEAD_DIM), :]
        dth = dtt_ref[pl.ds(hd, 1), :]
        dech = dect_ref[pl.ds(hd, 1), :]
        xdt = xh * dth
        for b in range(nb):
            brow = bcn_ref[b:b + 1, pl.ds(b_lane, SSM_D_STATE)]
            crow = bcn_ref[b:b + 1, pl.ds(c_lane, SSM_D_STATE)]
            hn = st_ref[b, hh] * dech[:, b:b + 1] + xdt[:, b:b + 1] * brow
            so_ref[b, hh] = hn
            yt_ref[pl.ds(r0, SSM_HEAD_DIM), b:b + 1] = jnp.sum(hn * crow, axis=-1, keepdims=True)

    @pl.when(step == pl.num_programs(0) - 1)
    def _():
        ssq = jnp.zeros((1, LANES), F32)
        for k in range(SSM_D_INNER // rb):
            rows = slice(k * rb, (k + 1) * rb)
            y = (yt_ref[rows, :] + xct_ref[rows, :] * dsk_ref[rows, :]) * _silu(zt_ref[rows, :])
            yt_ref[rows, :] = y
            ssq = ssq + jnp.sum(y * y, axis=0, keepdims=True)
        scale = lax.rsqrt(ssq * (1.0 / SSM_D_INNER) + EPS)
        acc = res_ref[...]
        for k in range(SSM_D_INNER // LANES):
            rows = slice(k * LANES, (k + 1) * LANES)
            yn = (yt_ref[rows, :] * scale * ngt_ref[rows, :]).T
            acc = acc + _mm(yn, wout_ref[rows, :])
        out_ref[...] = acc


def ssd_decode(zt, xbct, convt, dtt_raw, cwt, cbt, dtbt, alogt, d_rows, ngt, w_out, res, state):
    nb = state.shape[0]
    n_steps = SSM_HEADS // _DEC_HEADS
    full = lambda a: pl.BlockSpec(a.shape, lambda s, n=a.ndim: (0,) * n)
    st_spec = pl.BlockSpec((nb, _DEC_HEADS, SSM_HEAD_DIM, SSM_D_STATE), lambda s: (0, s, 0, 0))
    small = (zt, xbct, convt, dtt_raw, cwt, cbt, dtbt, alogt, d_rows, ngt, w_out, res)
    return pl.pallas_call(
        functools.partial(_ssd_decode_kernel, nb=nb), grid=(n_steps,),
        in_specs=[full(a) for a in small] + [st_spec],
        out_specs=[st_spec, full(res)],
        out_shape=[jax.ShapeDtypeStruct(state.shape, F32), jax.ShapeDtypeStruct(res.shape, F32)],
        scratch_shapes=[
            pltpu.VMEM((SSM_CONV_DIM, LANES), F32),
            pltpu.VMEM((LANES, LANES), F32),
            pltpu.VMEM((LANES, LANES), F32),
            pltpu.VMEM((LANES, SSM_BC_DIM), F32),
            pltpu.VMEM((SSM_D_INNER, LANES), F32),
        ],
        compiler_params=_cparams(("arbitrary",)),
        name="ssd_decode")(*small, state)


def _to_lanes(a, n=LANES):
    return jnp.pad(a.T, ((0, 0), (0, n - a.shape[0])))


def _mamba_sample(x, norm_g, conv_state, ssm_state, w_in, conv_w, conv_b, dt_bias, a_log, d_skip,
                  ssm_norm, w_out):
    nb, _, d = x.shape
    assert nb <= LANES and nb % SUBLANES == 0
    w_main, w_dt = _ssm_weights(w_in, BF16)
    proj, dt_raw = norm_matmul(x.reshape(nb, d), norm_g.reshape(1, d), w_main, w_dt, tm=nb, tn=1024)
    xbc = proj[:, SSM_D_INNER:]
    col = lambda v: jnp.broadcast_to(v.astype(F32)[:, None], (v.shape[0], LANES))
    convt = jnp.pad(jnp.transpose(conv_state, (1, 2, 0)), ((0, 0), (0, 0), (0, LANES - nb)))
    res = jnp.pad(x.reshape(nb, d), ((0, LANES - nb), (0, 0)))
    st, out = ssd_decode(
        _to_lanes(proj[:, :SSM_D_INNER]), _to_lanes(xbc), convt, _to_lanes(dt_raw),
        jnp.pad(conv_w.T, ((0, 0), (0, LANES - CONV_WIDTH))), col(conv_b),
        col(_pad_lanes(dt_bias)[0]), col(_pad_lanes(a_log)[0]),
        col(jnp.repeat(d_skip, SSM_HEAD_DIM)), col(ssm_norm), w_out.astype(BF16), res, ssm_state)
    new_conv = jnp.concatenate([conv_state[:, 1:], xbc[:, None, :]], axis=1)
    return out[:nb].reshape(nb, 1, d), new_conv, st


def _attn_block(q_ref, k_ref, v_ref, o_scr, lam_scr, start, dil, has_prev):
    nq = ATTN_BLOCK
    rows = pl.ds(start, nq, stride=dil) if dil > 1 else pl.ds(start, nq)
    hd = ATTN_HEAD_DIM
    q2 = q_ref[rows, :]
    lane = lax.broadcasted_iota(I32, (nq, LANES), 1)
    qblk = jnp.concatenate([jnp.where(lane < hd, q2, 0.0), jnp.where(lane >= hd, q2, 0.0)],
                           axis=0).astype(BF16)
    kj = lax.broadcasted_iota(I32, (nq, 2 * nq), 0)
    qi = lax.broadcasted_iota(I32, (nq, 2 * nq), 1) % nq
    nt = (((1,), (1,)), ((), ()))
    s = lax.dot_general(k_ref[rows, :].astype(BF16), qblk, nt,
                        preferred_element_type=F32) * ATTN_SCALE
    s = jnp.where(kj <= qi, s, -jnp.inf)
    vt = v_ref[rows, :].T.astype(BF16)
    if has_prev:
        prows = (pl.ds(start - dil * nq, nq, stride=dil) if dil > 1 else pl.ds(start - nq, nq))
        sp = lax.dot_general(k_ref[prows, :].astype(BF16), qblk, nt,
                             preferred_element_type=F32) * ATTN_SCALE
        s = jnp.concatenate([jnp.where(kj >= qi, sp, -jnp.inf), s], axis=0)
        vt = jnp.concatenate([v_ref[prows, :].T.astype(BF16), vt], axis=1)
    m = jnp.max(s, axis=0, keepdims=True)
    p = jnp.exp(s - m)
    l = jnp.sum(p, axis=0, keepdims=True)
    ot = jnp.dot(vt, p.astype(BF16), preferred_element_type=F32) * (1.0 / l)
    lam = jnp.broadcast_to(m + jnp.log(l), (hd, 2 * nq))
    o_scr[rows, :] = jnp.concatenate([ot[:hd, :nq], ot[hd:, nq:]], axis=0).T
    lam_scr[rows, :] = jnp.concatenate([lam[:, :nq], lam[:, nq:]], axis=0).T


def _attn_prompt_kernel(*refs, seq):
    qkv = refs[:9]
    y_ref = refs[9]
    kvt_refs = refs[10:13]
    o_scrs, lam_scrs = refs[13:16], refs[16:19]
    nq = ATTN_BLOCK
    for g, (window, dil) in enumerate(ATTN_PATTERNS):
        q_ref, k_ref, v_ref = qkv[3 * g:3 * g + 3]
        o_scr, lam_scr = o_scrs[g], lam_scrs[g]
        keep = min(window, seq)
        for c in range(keep // LANES):
            src = slice(seq - keep + c * LANES, seq - keep + (c + 1) * LANES)
            kvt_refs[g][0, :, c * LANES:(c + 1) * LANES] = k_ref[src, :].T
            kvt_refs[g][1, :, c * LANES:(c + 1) * LANES] = v_ref[src, :].T
        n_blk = seq // (dil * nq)
        for r in range(dil):
            for n in range(n_blk):
                _attn_block(q_ref, k_ref, v_ref, o_scr, lam_scr, r + dil * nq * n, dil, n > 0)

    rows_per = 256

    def merge(i, carry):
        rows = pl.ds(pl.multiple_of(i * rows_per, rows_per), rows_per)
        l0, l1, l2 = lam_scrs[0][rows, :], lam_scrs[1][rows, :], lam_scrs[2][rows, :]
        mx = jnp.maximum(jnp.maximum(l0, l1), l2)
        e0, e1, e2 = jnp.exp(l0 - mx), jnp.exp(l1 - mx), jnp.exp(l2 - mx)
        num = e0 * o_scrs[0][rows, :] + e1 * o_scrs[1][rows, :] + e2 * o_scrs[2][rows, :]
        y_ref[rows, :] = (num / (e0 + e1 + e2)).astype(BF16)
        return carry

    lax.fori_loop(0, seq // rows_per, merge, 0)


def attn_prompt(qkv, nb, seq):
    for window, dil in ATTN_PATTERNS:
        assert window // dil == ATTN_BLOCK and seq % (dil * ATTN_BLOCK) == 0
    n_pairs = ATTN_WIDTH // LANES
    keeps = [min(window, seq) for window, _ in ATTN_PATTERNS]
    in_specs = []
    for g in range(len(ATTN_PATTERNS)):
        for j in range(3):
            in_specs.append(pl.BlockSpec(
                (None, seq, LANES), lambda b, hp, g=g, j=j: (b, 0, (g * 3 + j) * n_pairs + hp)))
    outs = pl.pallas_call(
        functools.partial(_attn_prompt_kernel, seq=seq),
        grid=(nb, n_pairs),
        in_specs=in_specs,
        out_specs=[pl.BlockSpec((None, seq, LANES), lambda b, hp: (b, 0, hp))] + [
            pl.BlockSpec((None, 2, LANES, keep), lambda b, hp: (b, 0, hp, 0)) for keep in keeps],
        out_shape=[jax.ShapeDtypeStruct((nb, seq, ATTN_WIDTH), BF16)] + [
            jax.ShapeDtypeStruct((nb, 2, ATTN_WIDTH, keep), F32) for keep in keeps],
        scratch_shapes=[pltpu.VMEM((seq, LANES), F32)] * 6,
        compiler_params=_cparams(("parallel", "parallel")),
        name="attn_prompt")(*([qkv] * 9))
    kv = [jnp.transpose(kvt.reshape(nb, 2, ATTN_HEADS, ATTN_HEAD_DIM, keep), (0, 4, 1, 2, 3))
          for kvt, keep in zip(outs[1:], keeps)]
    return outs[0], kv


_DEC_ATTN_HEADS = 4


def _attn_decode_kernel(qkv_ref, c0_ref, c1_ref, c2_ref, y_ref, pad_ref):
    caches = (c0_ref, c1_ref, c2_ref)
    hg = pl.program_id(1)
    hd = ATTN_HEAD_DIM
    rnd = lambda a: a.astype(BF16).astype(F32)
    lane = lax.broadcasted_iota(I32, (hd, LANES), 1)

    def head_columns(a):
        pad_ref[...] = jnp.zeros_like(pad_ref)
        pad_ref[0:ATTN_HEADS, 0:hd] = a
        return pad_ref[...].T[0:hd, :]

    cols = [[rnd(head_columns(qkv_ref[g, j])) for j in range(3)] for g in range(len(ATTN_PATTERNS))]
    acc = jnp.zeros((hd, LANES), F32)
    for hh in range(_DEC_ATTN_HEADS):
        h = hg * _DEC_ATTN_HEADS + hh
        pick = lambda a: jnp.sum(jnp.where(lane == h, a, 0.0), axis=-1, keepdims=True)
        outs, lams = [], []
        for g, (window, dil) in enumerate(ATTN_PATTERNS):
            qc, knc, vnc = (pick(a) for a in cols[g])
            kt = rnd(caches[g][0, hh])
            vt = rnd(caches[g][1, hh])
            s = jnp.sum(kt * qc, axis=0, keepdims=True) * ATTN_SCALE
            if dil > 1:
                tok = lax.broadcasted_iota(I32, (1, window), 1)
                s = jnp.where(tok % dil == 0, s, -jnp.inf)
            s_new = jnp.sum(knc * qc, axis=0, keepdims=True) * ATTN_SCALE
            m = jnp.maximum(jnp.max(s, axis=-1, keepdims=True), s_new)
            p = jnp.exp(s - m)
            p_new = jnp.exp(s_new - m)
            l = jnp.sum(p, axis=-1, keepdims=True) + p_new
            o = (jnp.sum(vt * rnd(p), axis=-1, keepdims=True) + rnd(p_new) * vnc) / l
            outs.append(o)
            lams.append(m + jnp.log(l))
        mx = jnp.maximum(jnp.maximum(lams[0], lams[1]), lams[2])
        es = [jnp.exp(lm - mx) for lm in lams]
        y = (es[0] * outs[0] + es[1] * outs[1] + es[2] * outs[2]) / (es[0] + es[1] + es[2])
        acc = jnp.where(lane == hh, y, acc)
    y_ref[...] = acc


def attn_decode(qkv_s, caches):
    nb = qkv_s.shape[0]
    hpb = _DEC_ATTN_HEADS
    in_specs = [pl.BlockSpec((None, 3, 3, ATTN_HEADS, ATTN_HEAD_DIM), lambda b, g: (b, 0, 0, 0, 0))]
    args = [qkv_s]
    for c, (window, dil) in zip(caches, ATTN_PATTERNS):
        assert c.shape[1] == window and window % dil == 0, "cache must hold a full window"
        args.append(jnp.transpose(c, (0, 2, 3, 4, 1)))
        in_specs.append(pl.BlockSpec((None, 2, hpb, ATTN_HEAD_DIM, window),
                                     lambda b, g: (b, 0, g, 0, 0)))
    y = pl.pallas_call(
        _attn_decode_kernel, grid=(nb, ATTN_HEADS // hpb),
        in_specs=in_specs,
        out_specs=pl.BlockSpec((None, None, ATTN_HEAD_DIM, LANES), lambda b, g: (b, g, 0, 0)),
        out_shape=jax.ShapeDtypeStruct((nb, ATTN_HEADS // hpb, ATTN_HEAD_DIM, LANES), F32),
        scratch_shapes=[pltpu.VMEM((LANES, LANES), F32)],
        compiler_params=_cparams(("parallel", "arbitrary")),
        name="attn_decode")(*args)
    return jnp.transpose(y[..., :hpb], (0, 1, 3, 2)).reshape(nb, ATTN_WIDTH)


_R_E1, _R_E2, _R_RANK1, _R_RANK2, _R_W1, _R_W2 = range(6)


def _router_kernel(x_ref, g_ref, wr_ref, info_ref, cnt_ref, carry_ref):
    tm = x_ref.shape[0]

    @pl.when(pl.program_id(0) == 0)
    def _():
        carry_ref[...] = jnp.zeros_like(carry_ref)

    xn = _rms(x_ref[...], g_ref[...])
    logits = _mm(xn, wr_ref[...])
    lane = lax.broadcasted_iota(I32, (tm, LANES), 1)
    lanef = lane.astype(F32)
    big = float(LANES)
    neg = -jnp.inf

    is_g = (lane >= GROUP_LOGIT_LANE) & (lane < GROUP_LOGIT_LANE + N_EXPERT_GROUPS)
    gl = jnp.where(is_g, logits, neg)
    gmax = jnp.max(gl, axis=-1, keepdims=True)
    grp = jnp.min(jnp.where(gl == gmax, lanef, big), axis=-1, keepdims=True) - GROUP_LOGIT_LANE
    g_gate = 1.0 / jnp.sum(jnp.exp(gl - gmax), axis=-1, keepdims=True)

    lo = grp * EXPERTS_PER_GROUP
    in_grp = (lanef >= lo) & (lanef < lo + EXPERTS_PER_GROUP)
    el = jnp.where(in_grp, logits, neg)
    m1 = jnp.max(el, axis=-1, keepdims=True)
    i1 = jnp.min(jnp.where(el == m1, lanef, big), axis=-1, keepdims=True)
    el2 = jnp.where(lanef == i1, neg, el)
    m2 = jnp.max(el2, axis=-1, keepdims=True)
    i2 = jnp.min(jnp.where(el2 == m2, lanef, big), axis=-1, keepdims=True)
    ratio = jnp.exp(m2 - m1)
    w1 = g_gate / (1.0 + ratio)
    w2 = g_gate * ratio / (1.0 + ratio)

    oh1 = lanef == i1
    oh2 = lanef == i2
    a = (oh1 | oh2).astype(BF16)
    ri = lax.broadcasted_iota(I32, (tm, tm), 0)
    ci = lax.broadcasted_iota(I32, (tm, tm), 1)
    before = (ci < ri).astype(BF16)
    pref = jnp.dot(before, a, preferred_element_type=F32) + carry_ref[...]
    rank1 = jnp.sum(jnp.where(oh1, pref, 0.0), axis=-1, keepdims=True)
    rank2 = jnp.sum(jnp.where(oh2, pref, 0.0), axis=-1, keepdims=True)
    carry_ref[...] = carry_ref[...] + jnp.sum(a.astype(F32), axis=0, keepdims=True)

    info = jnp.zeros((tm, LANES), F32)
    for ln, val in ((_R_E1, i1), (_R_E2, i2), (_R_RANK1, rank1), (_R_RANK2, rank2),
                    (_R_W1, w1), (_R_W2, w2)):
        info = jnp.where(lane == ln, val, info)
    info_ref[...] = info
    cnt_ref[...] = carry_ref[...]


def moe_router(x, g, wr, *, tm):
    t, d = x.shape
    return pl.pallas_call(
        _router_kernel, grid=(t // tm,),
        in_specs=[pl.BlockSpec((tm, d), lambda i: (i, 0)),
                  pl.BlockSpec((1, d), lambda i: (0, 0)),
                  pl.BlockSpec((d, LANES), lambda i: (0, 0))],
        out_specs=[pl.BlockSpec((tm, LANES), lambda i: (i, 0)),
                   pl.BlockSpec((1, LANES), lambda i: (0, 0))],
        out_shape=[jax.ShapeDtypeStruct((t, LANES), F32), jax.ShapeDtypeStruct((1, LANES), F32)],
        scratch_shapes=[pltpu.VMEM((1, LANES), F32)],
        compiler_params=_cparams(("arbitrary",)),
        name="moe_router")(x, g, wr)


_ROW_UNROLL = 8
_ROW_TILES = D_MODEL // LANES
assert _ROW_TILES == SUBLANES


def _rows_to_tiles(ref, val):
    n = val.shape[0]
    for j in range(_ROW_TILES):
        ref[pl.ds(j, n, stride=_ROW_TILES), :] = val[:, j * LANES:(j + 1) * LANES]


def _tiles_to_rows(ref, n):
    return jnp.concatenate([ref[pl.ds(j, n, stride=_ROW_TILES), :] for j in range(_ROW_TILES)], axis=1)


def _tile_at(ref, row8):
    return ref.at[pl.ds(pl.multiple_of(row8, _ROW_TILES), _ROW_TILES)]


def _pos_blocks(pos, tm):
    n = pos.shape[0] // tm
    return (pos * _ROW_TILES).reshape(n, tm, 2).transpose(0, 2, 1).reshape(n, 1, 2 * tm)


def _row_copies(n_rows, make_copies):
    def body(i, carry):
        for u in range(_ROW_UNROLL):
            for cp in make_copies(i * _ROW_UNROLL + u):
                cp.start()
        return carry
    lax.fori_loop(0, n_rows // _ROW_UNROLL, body, 0)


def _dispatch_kernel(pos_ref, x_ref, g_ref, xs_in_ref, xs_ref, xn_ref, sem):
    del xs_in_ref
    tm = x_ref.shape[0]
    _rows_to_tiles(xn_ref, _rms(x_ref[...], g_ref[...]))

    def copies(r):
        src = _tile_at(xn_ref, r * _ROW_TILES)
        return [pltpu.make_async_copy(src, _tile_at(xs_ref, pos_ref[0, 0, k * tm + r]), sem)
                for k in range(2)]

    _row_copies(tm, copies)
    for _ in range(2):
        pltpu.make_async_copy(xn_ref, xs_ref.at[pl.ds(0, tm * _ROW_TILES)], sem).wait()


def moe_dispatch(x, g, pos, xs_init, *, tm):
    t, d = x.shape
    n = t // tm
    pos_blk = _pos_blocks(pos, tm)
    return pl.pallas_call(
        _dispatch_kernel, grid=(n,),
        in_specs=[pl.BlockSpec((1, 1, 2 * tm), lambda i: (i, 0, 0), memory_space=pltpu.SMEM),
                  pl.BlockSpec((tm, d), lambda i: (i, 0)),
                  pl.BlockSpec((1, d), lambda i: (0, 0)),
                  pl.BlockSpec(memory_space=pl.ANY)],
        out_specs=pl.BlockSpec(memory_space=pl.ANY),
        out_shape=jax.ShapeDtypeStruct(xs_init.shape, F32),
        scratch_shapes=[pltpu.VMEM((tm * _ROW_TILES, LANES), F32), pltpu.SemaphoreType.DMA(())],
        input_output_aliases={3: 0},
        compiler_params=_cparams(("arbitrary",)),
        name="moe_dispatch")(pos_blk, x, g, xs_init)


def _expert_kernel(te_ref, tv_ref, xs_ref, wgu_ref, wdn_ref, ys_ref, wgu_bf, wdn_bf):
    t = pl.program_id(0)
    tmx = xs_ref.shape[0] // _ROW_TILES
    changed = jnp.logical_or(t == 0, te_ref[t] != te_ref[jnp.maximum(t - 1, 0)])

    @pl.when(changed)
    def _():
        wgu_bf[...] = wgu_ref[...].astype(BF16)
        wdn_bf[...] = wdn_ref[...].astype(BF16)

    @pl.when(tv_ref[t] > 0)
    def _():
        x = _tiles_to_rows(xs_ref, tmx).astype(BF16)
        gu = jnp.dot(x, wgu_bf[...], preferred_element_type=F32)
        act = _silu(gu[:, :D_EXPERT]) * gu[:, D_EXPERT:]
        _rows_to_tiles(ys_ref, jnp.dot(act.astype(BF16), wdn_bf[...], preferred_element_type=F32))

    @pl.when(tv_ref[t] == 0)
    def _():
        ys_ref[...] = jnp.zeros_like(ys_ref)


def moe_experts(xs, w_gate_up, w_down, layer, tile_expert, tile_valid, *, tmx):
    d = w_gate_up.shape[2]
    blk = tmx * _ROW_TILES
    n_tiles = xs.shape[0] // blk
    grid_spec = pltpu.PrefetchScalarGridSpec(
        num_scalar_prefetch=2, grid=(n_tiles,),
        in_specs=[pl.BlockSpec((blk, LANES), lambda t, te, tv: (t, 0)),
                  pl.BlockSpec((None, None, d, 2 * D_EXPERT), lambda t, te, tv: (layer, te[t], 0, 0)),
                  pl.BlockSpec((None, None, D_EXPERT, d), lambda t, te, tv: (layer, te[t], 0, 0))],
        out_specs=pl.BlockSpec((blk, LANES), lambda t, te, tv: (t, 0)),
        scratch_shapes=[pltpu.VMEM((d, 2 * D_EXPERT), BF16), pltpu.VMEM((D_EXPERT, d), BF16)])
    return pl.pallas_call(
        _expert_kernel, grid_spec=grid_spec,
        out_shape=jax.ShapeDtypeStruct(xs.shape, F32),
        compiler_params=_cparams(("arbitrary",)),
        name="moe_experts")(tile_expert, tile_valid, xs, w_gate_up, w_down)


def _combine_kernel(pos_ref, x_ref, info_ref, g_ref, ys_ref, o_ref, buf0_ref, buf1_ref, sem, *,
                    final_norm):
    tm = x_ref.shape[0]
    bufs = (buf0_ref, buf1_ref)

    def copies(r):
        return [pltpu.make_async_copy(_tile_at(ys_ref, pos_ref[0, 0, k * tm + r]),
                                      _tile_at(bufs[k], r * _ROW_TILES), sem)
                for k in range(2)]

    _row_copies(tm, copies)
    for k in range(2):
        pltpu.make_async_copy(ys_ref.at[pl.ds(0, tm * _ROW_TILES)], bufs[k], sem).wait()
    info = info_ref[...]
    out = (x_ref[...] + info[:, _R_W1:_R_W1 + 1] * _tiles_to_rows(buf0_ref, tm)
           + info[:, _R_W2:_R_W2 + 1] * _tiles_to_rows(buf1_ref, tm))
    if final_norm:
        out = _rms(out, g_ref[...])
    o_ref[...] = out


def moe_combine(x, info, pos, ys, g_final, *, tm, final_norm):
    t, d = x.shape
    n = t // tm
    pos_blk = _pos_blocks(pos, tm)
    return pl.pallas_call(
        functools.partial(_combine_kernel, final_norm=final_norm), grid=(n,),
        in_specs=[pl.BlockSpec((1, 1, 2 * tm), lambda i: (i, 0, 0), memory_space=pltpu.SMEM),
                  pl.BlockSpec((tm, d), lambda i: (i, 0)),
                  pl.BlockSpec((tm, LANES), lambda i: (i, 0)),
                  pl.BlockSpec((1, d), lambda i: (0, 0)),
                  pl.BlockSpec(memory_space=pl.ANY)],
        out_specs=pl.BlockSpec((tm, d), lambda i: (i, 0)),
        out_shape=jax.ShapeDtypeStruct((t, d), F32),
        scratch_shapes=[pltpu.VMEM((tm * _ROW_TILES, LANES), F32)] * 2 + [pltpu.SemaphoreType.DMA(())],
        compiler_params=_cparams(("arbitrary",)),
        name="moe_combine")(pos_blk, x, info, g_final, ys)


def _router_weight(w_group, w_expert):
    d = w_group.shape[0]
    we = jnp.transpose(w_expert, (1, 0, 2)).reshape(d, N_EXPERTS)
    w = jnp.concatenate([we, w_group], axis=1)
    return jnp.pad(w, ((0, 0), (0, LANES - w.shape[1]))).astype(BF16)


def hier_moe_layer(x, norm_g, w_group, w_expert, w_gate_up, w_down, layer, g_final, *, tm, tmx,
                   final_norm):
    t, d = x.shape
    g = norm_g.reshape(1, d)
    info, counts = moe_router(x, g, _router_weight(w_group, w_expert), tm=tm)

    counts = counts[0, :N_EXPERTS].astype(I32)
    padded = ((counts + tmx - 1) // tmx) * tmx
    ends = jnp.cumsum(padded)
    offsets = ends - padded
    e12 = info[:, _R_E1:_R_E2 + 1].astype(I32)
    rank12 = info[:, _R_RANK1:_R_RANK2 + 1].astype(I32)
    pos = offsets[e12] + rank12
    n_rows = 2 * t + N_EXPERTS * tmx
    n_tiles = n_rows // tmx
    tile_start = jnp.arange(n_tiles, dtype=I32) * tmx
    tile_valid = (tile_start < ends[-1]).astype(I32)
    probe = jnp.minimum(tile_start, jnp.maximum(ends[-1] - 1, 0))
    tile_expert = jnp.sum((probe[:, None] >= ends[None, :]).astype(I32), axis=1)
    tile_expert = jnp.minimum(tile_expert, N_EXPERTS - 1)

    depth = w_gate_up.shape[0]
    xs = moe_dispatch(x, g, pos, jnp.zeros((n_rows * _ROW_TILES, LANES), F32), tm=tm)
    ys = moe_experts(xs, w_gate_up.reshape(depth, N_EXPERTS, d, 2 * D_EXPERT),
                     w_down.reshape(depth, N_EXPERTS, D_EXPERT, d), layer,
                     tile_expert, tile_valid, tmx=tmx)
    return moe_combine(x, info, pos, ys, g_final.reshape(1, d), tm=tm, final_norm=final_norm)


def kernel(x_prompt, x_sample, state_conv, state_ssm, cache_kv_w128, cache_kv_w512, cache_kv_w2048, norm_mix, norm_ffn, norm_final, ssm_in_proj, ssm_conv_w, ssm_conv_b, ssm_dt_bias, ssm_a_log, ssm_d, ssm_norm, ssm_out_proj, attn_qkv, attn_out, moe_route_group, moe_route_expert, moe_w_gate_up, moe_w_down):
    nb, seq, d = x_prompt.shape
    ns = x_sample.shape[0]
    t = nb * seq
    tm = min(t, 1024)
    n_groups = len(ATTN_PATTERNS)
    kv_caches = (cache_kv_w128, cache_kv_w512, cache_kv_w2048)

    ssm_params = (ssm_in_proj[0], ssm_conv_w[0], ssm_conv_b[0], ssm_dt_bias[0], ssm_a_log[0],
                  ssm_d[0], ssm_norm[0], ssm_out_proj[0])
    xp, conv_p, ssm_p = _mamba_prompt(x_prompt, norm_mix[0], *ssm_params)
    xs, conv_s, ssm_s = _mamba_sample(x_sample, norm_mix[0], state_conv[0], state_ssm[0], *ssm_params)

    def moe(x2d, i, tm_moe, tmx, final_norm):
        return hier_moe_layer(x2d, norm_ffn[i], moe_route_group[i], moe_route_expert[i],
                              moe_w_gate_up, moe_w_down, i, norm_final,
                              tm=tm_moe, tmx=tmx, final_norm=final_norm)

    xp = moe(xp.reshape(t, d), 0, 256, 256, False)
    xs = moe(xs.reshape(ns, d), 0, ns, 16, False)

    g1 = norm_mix[1].reshape(1, d)
    w_qkv = attn_qkv[0].astype(BF16)
    w_o = attn_out[0].astype(BF16)
    qkv_p = norm_matmul(xp, g1, w_qkv, tm=tm, tn=1024)
    y_p, kv_p = attn_prompt(qkv_p.reshape(nb, seq, -1), nb, seq)
    xp = matmul_res(y_p.reshape(t, ATTN_WIDTH), w_o, xp, tm=tm)

    qkv_s = norm_matmul(xs, g1, w_qkv, tm=ns, tn=1024)
    qkv_s5 = qkv_s.reshape(ns, n_groups, 3, ATTN_HEADS, ATTN_HEAD_DIM)
    y_s = attn_decode(qkv_s5, [c[0] for c in kv_caches])
    xs = matmul_res(y_s.reshape(ns, ATTN_WIDTH), w_o, xs, tm=ns)
    kv_s = [qkv_s5[:, g, 1:3][:, None] for g in range(n_groups)]

    y_prompt = moe(xp, 1, 256, 256, True).reshape(nb, seq, d)
    y_sample = moe(xs, 1, ns, 16, True).reshape(ns, 1, d)

    return (y_prompt, y_sample, conv_p[None], conv_s[None], ssm_p[None], ssm_s[None],
            kv_p[0][None], kv_s[0][None], kv_p[1][None], kv_s[1][None], kv_p[2][None], kv_s[2][None])
```

```python
import functools
import math

import jax
import jax.numpy as jnp
from jax import lax
from jax.experimental import pallas as pl
from jax.experimental.pallas import tpu as pltpu
from jax.experimental.pallas import tpu_sc as plsc

F32 = jnp.float32
BF16 = jnp.bfloat16
I32 = jnp.int32

EPS = 1e-6
D_MODEL = 1024
LANES = 128
SUBLANES = 8

SSM_D_INNER = 2048
SSM_HEAD_DIM = 64
SSM_HEADS = 32
SSM_GROUPS = 4
SSM_HEADS_PER_GROUP = 8
SSM_D_STATE = 128
CONV_WIDTH = 4
SSD_CHUNK = 128
SSM_BC_DIM = 2 * SSM_GROUPS * SSM_D_STATE
SSM_CONV_DIM = SSM_D_INNER + SSM_BC_DIM

ATTN_PATTERNS = ((128, 1), (512, 4), (2048, 16))
ATTN_HEAD_DIM = 64
ATTN_HEADS = 16
ATTN_WIDTH = 1024
ATTN_BLOCK = 128
ATTN_SCALE = ATTN_HEAD_DIM ** -0.5

N_EXPERT_GROUPS = 4
EXPERTS_PER_GROUP = 8
N_EXPERTS = N_EXPERT_GROUPS * EXPERTS_PER_GROUP
D_EXPERT = 512
GROUP_LOGIT_LANE = N_EXPERTS

VMEM_LIMIT = 56 * 1024 * 1024


def _cparams(sem):
    return pltpu.CompilerParams(dimension_semantics=sem, vmem_limit_bytes=VMEM_LIMIT)


def _rms(x, g):
    ms = jnp.mean(x * x, axis=-1, keepdims=True)
    return x * lax.rsqrt(ms + EPS) * g


def _silu(x):
    return x * jax.nn.sigmoid(x)


def _softplus(x):
    return jnp.maximum(x, 0.0) + jnp.log1p(jnp.exp(-jnp.abs(x)))


def _mm(a, b):
    prec = lax.Precision.HIGHEST if b.dtype == F32 else None
    return jnp.dot(a.astype(b.dtype), b, preferred_element_type=F32, precision=prec)


def _norm_matmul_kernel(x_ref, g_ref, w_ref, o_ref, xn_ref):
    @pl.when(pl.program_id(1) == 0)
    def _():
        xn_ref[...] = _rms(x_ref[...], g_ref[...]).astype(xn_ref.dtype)

    o_ref[...] = _mm(xn_ref[...], w_ref[...])


def _norm_matmul2_kernel(x_ref, g_ref, w_ref, w2_ref, o_ref, o2_ref, xn_ref):
    @pl.when(pl.program_id(1) == 0)
    def _():
        xn = _rms(x_ref[...], g_ref[...]).astype(xn_ref.dtype)
        xn_ref[...] = xn
        o2_ref[...] = _mm(xn, w2_ref[...])

    o_ref[...] = _mm(xn_ref[...], w_ref[...])


def norm_matmul(x, g, w, w2=None, *, tm, tn):
    t, d = x.shape
    n = w.shape[1]
    grid = (t // tm, n // tn)
    x_spec = pl.BlockSpec((tm, d), lambda i, j: (i, 0))
    g_spec = pl.BlockSpec((1, d), lambda i, j: (0, 0))
    w_spec = pl.BlockSpec((d, tn), lambda i, j: (0, j))
    o_spec = pl.BlockSpec((tm, tn), lambda i, j: (i, j))
    scratch = [pltpu.VMEM((tm, d), w.dtype)]
    if w2 is None:
        return pl.pallas_call(
            _norm_matmul_kernel, grid=grid,
            in_specs=[x_spec, g_spec, w_spec], out_specs=o_spec,
            out_shape=jax.ShapeDtypeStruct((t, n), F32),
            scratch_shapes=scratch,
            compiler_params=_cparams(("parallel", "arbitrary")),
            name="norm_matmul")(x, g, w)
    n2 = w2.shape[1]
    return pl.pallas_call(
        _norm_matmul2_kernel, grid=grid,
        in_specs=[x_spec, g_spec, w_spec, pl.BlockSpec((d, n2), lambda i, j: (0, 0))],
        out_specs=[o_spec, pl.BlockSpec((tm, n2), lambda i, j: (i, 0))],
        out_shape=[jax.ShapeDtypeStruct((t, n), F32), jax.ShapeDtypeStruct((t, n2), F32)],
        scratch_shapes=scratch,
        compiler_params=_cparams(("parallel", "arbitrary")),
        name="norm_matmul2")(x, g, w, w2)


def _matmul_res_kernel(a_ref, w_ref, r_ref, o_ref):
    o_ref[...] = r_ref[...] + _mm(a_ref[...], w_ref[...])


def matmul_res(a, w, res, *, tm):
    t, k = a.shape
    n = w.shape[1]
    return pl.pallas_call(
        _matmul_res_kernel, grid=(t // tm,),
        in_specs=[pl.BlockSpec((tm, k), lambda i: (i, 0)),
                  pl.BlockSpec((k, n), lambda i: (0, 0)),
                  pl.BlockSpec((tm, n), lambda i: (i, 0))],
        out_specs=pl.BlockSpec((tm, n), lambda i: (i, 0)),
        out_shape=jax.ShapeDtypeStruct((t, n), F32),
        compiler_params=_cparams(("parallel",)),
        name="matmul_res")(a, w, res)


_CONV_COLS = 512


def _ssd_kernel(z_ref, xs_ref, bc_ref, dt_ref, cw_ref, cb_ref, dtb_ref, alog_ref,
                dsk_ref, ng_ref,
                y_ref, tail_out_ref, st_ref,
                tail_ref, h_ref, xc_ref, xst_ref, yt_ref, dtt_ref, cst_ref, xde_ref, yn_ref):
    q = SSD_CHUNK
    c = pl.program_id(1)

    @pl.when(c == 0)
    def _():
        tail_ref[...] = jnp.zeros_like(tail_ref)
        h_ref[...] = jnp.zeros_like(h_ref)

    row = lax.broadcasted_iota(I32, (q, _CONV_COLS), 0)
    for k in range(SSM_CONV_DIM // _CONV_COLS):
        lo = k * _CONV_COLS
        if lo < SSM_D_INNER:
            src = xs_ref[0, :, lo:lo + _CONV_COLS]
        else:
            src = bc_ref[0, :, lo - SSM_D_INNER:lo - SSM_D_INNER + _CONV_COLS]
        tl = tail_ref[:, lo:lo + _CONV_COLS]
        w = cw_ref[:, lo:lo + _CONV_COLS]
        acc = src * w[CONV_WIDTH - 1:CONV_WIDTH] + cb_ref[:, lo:lo + _CONV_COLS]
        for s in range(1, CONV_WIDTH):
            cur = pltpu.roll(src, s, axis=0)
            prev = jnp.tile(pltpu.roll(tl, s, axis=0), (q // SUBLANES, 1))
            shifted = jnp.where(row < s, prev, cur)
            acc = acc + shifted * w[CONV_WIDTH - 1 - s:CONV_WIDTH - s]
        xc_ref[:, lo:lo + _CONV_COLS] = _silu(acc)
        tail_ref[:, lo:lo + _CONV_COLS] = src[q - SUBLANES:, :]

    for k in range(SSM_D_INNER // LANES):
        xst_ref[k * LANES:(k + 1) * LANES, :] = xc_ref[:, k * LANES:(k + 1) * LANES].T

    dt = _softplus(dt_ref[0] + dtb_ref[...])
    da = dt * (-jnp.exp(alog_ref[...]))
    ri = lax.broadcasted_iota(I32, (q, q), 0)
    ci = lax.broadcasted_iota(I32, (q, q), 1)
    tril = (ci <= ri).astype(F32)
    cs = jnp.dot(tril, da, preferred_element_type=F32, precision=lax.Precision.HIGHEST)
    dtt_ref[...] = dt.T
    cst_ref[...] = cs.T

    causal_t = ci >= ri
    hpg = SSM_HEADS_PER_GROUP
    gp = hpg * SSM_HEAD_DIM
    for g in range(SSM_GROUPS):
        b0 = SSM_D_INNER + g * SSM_D_STATE
        c0 = SSM_D_INNER + SSM_GROUPS * SSM_D_STATE + g * SSM_D_STATE
        bm = xc_ref[:, b0:b0 + SSM_D_STATE].astype(BF16)
        cm = xc_ref[:, c0:c0 + SSM_D_STATE].astype(BF16)
        cbt = lax.dot_general(bm, cm, (((1,), (1,)), ((), ())), preferred_element_type=F32)
        hg = h_ref[g * gp:(g + 1) * gp, :].astype(BF16)
        yt_ref[g * gp:(g + 1) * gp, :] = lax.dot_general(
            hg, cm, (((1,), (1,)), ((), ())), preferred_element_type=F32)

        for e in range(hpg):
            hd = g * hpg + e
            rows = slice(hd * SSM_HEAD_DIM, (hd + 1) * SSM_HEAD_DIM)
            csr = cst_ref[hd:hd + 1, :]
            rowb = jnp.broadcast_to(csr, (q, q))
            colb = rowb.T
            dec = jnp.exp(jnp.where(causal_t, rowb - colb, -jnp.inf))
            mt = (cbt * dec).astype(BF16)
            dtr = dtt_ref[hd:hd + 1, :]
            xsh = xst_ref[rows, :]
            xdt = xsh * dtr
            ydt = jnp.dot(xdt.astype(BF16), mt, preferred_element_type=F32)
            cs_end = csr[:, q - 1:q]
            yt_ref[rows, :] = yt_ref[rows, :] * jnp.exp(csr) + ydt + xsh * dsk_ref[rows, :]
            xde_ref[e * SSM_HEAD_DIM:(e + 1) * SSM_HEAD_DIM, :] = (
                xdt * jnp.exp(cs_end - csr)).astype(BF16)
            h_ref[rows, :] = h_ref[rows, :] * jnp.exp(cs_end)
        h_ref[g * gp:(g + 1) * gp, :] = h_ref[g * gp:(g + 1) * gp, :] + jnp.dot(
            xde_ref[...], bm, preferred_element_type=F32)

    ssq = jnp.zeros((q, 1), F32)
    for k in range(SSM_D_INNER // LANES):
        yk = yt_ref[k * LANES:(k + 1) * LANES, :].T * _silu(z_ref[0, :, k * LANES:(k + 1) * LANES])
        yn_ref[:, k * LANES:(k + 1) * LANES] = yk
        ssq = ssq + jnp.sum(yk * yk, axis=-1, keepdims=True)
    scale = lax.rsqrt(ssq * (1.0 / SSM_D_INNER) + EPS)
    y_ref[0] = (yn_ref[...] * scale * ng_ref[...]).astype(BF16)

    @pl.when(c == pl.num_programs(1) - 1)
    def _():
        tail_out_ref[0] = tail_ref[...]
        st_ref[0] = h_ref[...]


def ssd_prompt(proj, dt_raw, conv_w, conv_b, dt_bias, a_log, d_rows, norm_g, nb, seq):
    nc = seq // SSD_CHUNK
    q = SSD_CHUNK
    const2 = lambda b, c: (0, 0)
    return pl.pallas_call(
        _ssd_kernel, grid=(nb, nc),
        in_specs=[
            pl.BlockSpec((1, q, SSM_D_INNER), lambda b, c: (b, c, 0)),
            pl.BlockSpec((1, q, SSM_D_INNER), lambda b, c: (b, c, 1)),
            pl.BlockSpec((1, q, SSM_BC_DIM), lambda b, c: (b, c, 2 * SSM_D_INNER // SSM_BC_DIM)),
            pl.BlockSpec((1, q, LANES), lambda b, c: (b, c, 0)),
            pl.BlockSpec((CONV_WIDTH, SSM_CONV_DIM), const2),
            pl.BlockSpec((1, SSM_CONV_DIM), const2),
            pl.BlockSpec((1, LANES), const2),
            pl.BlockSpec((1, LANES), const2),
            pl.BlockSpec((SSM_D_INNER, LANES), const2),
            pl.BlockSpec((1, SSM_D_INNER), const2),
        ],
        out_specs=[
            pl.BlockSpec((1, q, SSM_D_INNER), lambda b, c: (b, c, 0)),
            pl.BlockSpec((1, SUBLANES, SSM_CONV_DIM), lambda b, c: (b, 0, 0)),
            pl.BlockSpec((1, SSM_D_INNER, SSM_D_STATE), lambda b, c: (b, 0, 0)),
        ],
        out_shape=[
            jax.ShapeDtypeStruct((nb, seq, SSM_D_INNER), BF16),
            jax.ShapeDtypeStruct((nb, SUBLANES, SSM_CONV_DIM), F32),
            jax.ShapeDtypeStruct((nb, SSM_D_INNER, SSM_D_STATE), F32),
        ],
        scratch_shapes=[
            pltpu.VMEM((SUBLANES, SSM_CONV_DIM), F32),
            pltpu.VMEM((SSM_D_INNER, SSM_D_STATE), F32),
            pltpu.VMEM((q, SSM_CONV_DIM), F32),
            pltpu.VMEM((SSM_D_INNER, q), F32),
            pltpu.VMEM((SSM_D_INNER, q), F32),
            pltpu.VMEM((LANES, q), F32),
            pltpu.VMEM((LANES, q), F32),
            pltpu.VMEM((SSM_HEADS_PER_GROUP * SSM_HEAD_DIM, q), BF16),
            pltpu.VMEM((q, SSM_D_INNER), F32),
        ],
        compiler_params=_cparams(("parallel", "arbitrary")),
        name="ssd_prompt")(proj, proj, proj, dt_raw, conv_w, conv_b, dt_bias, a_log, d_rows, norm_g)


def _pad_lanes(v, n=LANES):
    return jnp.pad(v.astype(F32), (0, n - v.shape[0])).reshape(1, n)


def _ssm_weights(w_in, dtype):
    n_main = SSM_D_INNER + SSM_CONV_DIM
    w_main = w_in[:, :n_main].astype(dtype)
    w_dt = jnp.pad(w_in[:, n_main:], ((0, 0), (0, LANES - SSM_HEADS))).astype(dtype)
    return w_main, w_dt


def _mamba_prompt(x, norm_g, w_in, conv_w, conv_b, dt_bias, a_log, d_skip, ssm_norm, w_out):
    nb, seq, d = x.shape
    t = nb * seq
    w_main, w_dt = _ssm_weights(w_in, BF16)
    tm = min(t, 1024)
    proj, dt_raw = norm_matmul(x.reshape(t, d), norm_g.reshape(1, d), w_main, w_dt, tm=tm, tn=1024)
    d_rows = jnp.broadcast_to(jnp.repeat(d_skip.astype(F32), SSM_HEAD_DIM)[:, None], (SSM_D_INNER, LANES))
    y, tail, st = ssd_prompt(
        proj.reshape(nb, seq, -1), dt_raw.reshape(nb, seq, LANES),
        conv_w, conv_b.reshape(1, -1), _pad_lanes(dt_bias), _pad_lanes(a_log), d_rows,
        ssm_norm.reshape(1, -1), nb, seq)
    x_new = matmul_res(y.reshape(t, SSM_D_INNER), w_out.astype(BF16), x.reshape(t, d), tm=tm)
    conv_state = tail[:, SUBLANES - (CONV_WIDTH - 1):]
    ssm_state = st.reshape(nb, SSM_HEADS, SSM_HEAD_DIM, SSM_D_STATE)
    return x_new.reshape(nb, seq, d), conv_state, ssm_state


_DEC_HEADS = 4


def _ssd_decode_kernel(zt_ref, xbct_ref, convt_ref, dtt_raw_ref, cwt_ref, cbt_ref, dtbt_ref,
                       alogt_ref, dsk_ref, ngt_ref, wout_ref, res_ref, st_ref,
                       so_ref, out_ref,
                       xct_ref, dtt_ref, dect_ref, bcn_ref, yt_ref, *, nb):
    step = pl.program_id(0)
    rb = 512

    @pl.when(step == 0)
    def _():
        for k in range(SSM_CONV_DIM // rb):
            rows = slice(k * rb, (k + 1) * rb)
            acc = xbct_ref[rows, :] * cwt_ref[rows, CONV_WIDTH - 1:CONV_WIDTH] + cbt_ref[rows, 0:1]
            for j in range(CONV_WIDTH - 1):
                acc = acc + convt_ref[j, rows, :] * cwt_ref[rows, j:j + 1]
            xct_ref[rows, :] = _silu(acc)
        dt = _softplus(dtt_raw_ref[...] + dtbt_ref[:, 0:1])
        dtt_ref[...] = dt
        dect_ref[...] = jnp.exp(dt * (-jnp.exp(alogt_ref[:, 0:1])))
        for k in range(SSM_BC_DIM // LANES):
            bcn_ref[:, k * LANES:(k + 1) * LANES] = xct_ref[
                SSM_D_INNER + k * LANES:SSM_D_INNER + (k + 1) * LANES, :].T
        yt_ref[...] = jnp.zeros_like(yt_ref)

    for hh in range(_DEC_HEADS):
        hd = step * _DEC_HEADS + hh
        r0 = pl.multiple_of(hd * SSM_HEAD_DIM, SSM_HEAD_DIM)
        grp = hd // SSM_HEADS_PER_GROUP
        b_lane = pl.multiple_of(grp * SSM_D_STATE, SSM_D_STATE)
        c_lane = pl.multiple_of(SSM_GROUPS * SSM_D_STATE + grp * SSM_D_STATE, SSM_D_STATE)
        xh = xct_ref[pl.ds(r0, SSM_HEAD_DIM), :]
        dth = dtt_ref[pl.ds(hd, 1), :]
        dech = dect_ref[pl.ds(hd, 1), :]
        xdt = xh * dth
        for b in range(nb):
            brow = bcn_ref[b:b + 1, pl.ds(b_lane, SSM_D_STATE)]
            crow = bcn_ref[b:b + 1, pl.ds(c_lane, SSM_D_STATE)]
            hn = st_ref[b, hh] * dech[:, b:b + 1] + xdt[:, b:b + 1] * brow
            so_ref[b, hh] = hn
            yt_ref[pl.ds(r0, SSM_HEAD_DIM), b:b + 1] = jnp.sum(hn * crow, axis=-1, keepdims=True)

    @pl.when(step == pl.num_programs(0) - 1)
    def _():
        ssq = jnp.zeros((1, LANES), F32)
        for k in range(SSM_D_INNER // rb):
            rows = slice(k * rb, (k + 1) * rb)
            y = (yt_ref[rows, :] + xct_ref[rows, :] * dsk_ref[rows, :]) * _silu(zt_ref[rows, :])
            yt_ref[rows, :] = y
            ssq = ssq + jnp.sum(y * y, axis=0, keepdims=True)
        scale = lax.rsqrt(ssq * (1.0 / SSM_D_INNER) + EPS)
        acc = res_ref[...]
        for k in range(SSM_D_INNER // LANES):
            rows = slice(k * LANES, (k + 1) * LANES)
            yn = (yt_ref[rows, :] * scale * ngt_ref[rows, :]).T
            acc = acc + _mm(yn, wout_ref[rows, :])
        out_ref[...] = acc


def ssd_decode(zt, xbct, convt, dtt_raw, cwt, cbt, dtbt, alogt, d_rows, ngt, w_out, res, state):
    nb = state.shape[0]
    n_steps = SSM_HEADS // _DEC_HEADS
    full = lambda a: pl.BlockSpec(a.shape, lambda s, n=a.ndim: (0,) * n)
    st_spec = pl.BlockSpec((nb, _DEC_HEADS, SSM_HEAD_DIM, SSM_D_STATE), lambda s: (0, s, 0, 0))
    small = (zt, xbct, convt, dtt_raw, cwt, cbt, dtbt, alogt, d_rows, ngt, w_out, res)
    return pl.pallas_call(
        functools.partial(_ssd_decode_kernel, nb=nb), grid=(n_steps,),
        in_specs=[full(a) for a in small] + [st_spec],
        out_specs=[st_spec, full(res)],
        out_shape=[jax.ShapeDtypeStruct(state.shape, F32), jax.ShapeDtypeStruct(res.shape, F32)],
        scratch_shapes=[
            pltpu.VMEM((SSM_CONV_DIM, LANES), F32),
            pltpu.VMEM((LANES, LANES), F32),
            pltpu.VMEM((LANES, LANES), F32),
            pltpu.VMEM((LANES, SSM_BC_DIM), F32),
            pltpu.VMEM((SSM_D_INNER, LANES), F32),
        ],
        compiler_params=_cparams(("arbitrary",)),
        name="ssd_decode")(*small, state)


def _to_lanes(a, n=LANES):
    return jnp.pad(a.T, ((0, 0), (0, n - a.shape[0])))


def _mamba_sample(x, norm_g, conv_state, ssm_state, w_in, conv_w, conv_b, dt_bias, a_log, d_skip,
                  ssm_norm, w_out):
    nb, _, d = x.shape
    assert nb <= LANES and nb % SUBLANES == 0
    w_main, w_dt = _ssm_weights(w_in, BF16)
    proj, dt_raw = norm_matmul(x.reshape(nb, d), norm_g.reshape(1, d), w_main, w_dt, tm=nb, tn=1024)
    xbc = proj[:, SSM_D_INNER:]
    col = lambda v: jnp.broadcast_to(v.astype(F32)[:, None], (v.shape[0], LANES))
    convt = jnp.pad(jnp.transpose(conv_state, (1, 2, 0)), ((0, 0), (0, 0), (0, LANES - nb)))
    res = jnp.pad(x.reshape(nb, d), ((0, LANES - nb), (0, 0)))
    st, out = ssd_decode(
        _to_lanes(proj[:, :SSM_D_INNER]), _to_lanes(xbc), convt, _to_lanes(dt_raw),
        jnp.pad(conv_w.T, ((0, 0), (0, LANES - CONV_WIDTH))), col(conv_b),
        col(_pad_lanes(dt_bias)[0]), col(_pad_lanes(a_log)[0]),
        col(jnp.repeat(d_skip, SSM_HEAD_DIM)), col(ssm_norm), w_out.astype(BF16), res, ssm_state)
    new_conv = jnp.concatenate([conv_state[:, 1:], xbc[:, None, :]], axis=1)
    return out[:nb].reshape(nb, 1, d), new_conv, st


def _attn_block(q_ref, k_ref, v_ref, o_scr, lam_scr, start, dil, has_prev):
    nq = ATTN_BLOCK
    rows = pl.ds(start, nq, stride=dil) if dil > 1 else pl.ds(start, nq)
    hd = ATTN_HEAD_DIM
    q2 = q_ref[rows, :]
    lane = lax.broadcasted_iota(I32, (nq, LANES), 1)
    qblk = jnp.concatenate([jnp.where(lane < hd, q2, 0.0), jnp.where(lane >= hd, q2, 0.0)],
                           axis=0).astype(BF16)
    kj = lax.broadcasted_iota(I32, (nq, 2 * nq), 0)
    qi = lax.broadcasted_iota(I32, (nq, 2 * nq), 1) % nq
    nt = (((1,), (1,)), ((), ()))
    s = lax.dot_general(k_ref[rows, :].astype(BF16), qblk, nt,
                        preferred_element_type=F32) * ATTN_SCALE
    s = jnp.where(kj <= qi, s, -jnp.inf)
    vt = v_ref[rows, :].T.astype(BF16)
    if has_prev:
        prows = (pl.ds(start - dil * nq, nq, stride=dil) if dil > 1 else pl.ds(start - nq, nq))
        sp = lax.dot_general(k_ref[prows, :].astype(BF16), qblk, nt,
                             preferred_element_type=F32) * ATTN_SCALE
        s = jnp.concatenate([jnp.where(kj >= qi, sp, -jnp.inf), s], axis=0)
        vt = jnp.concatenate([v_ref[prows, :].T.astype(BF16), vt], axis=1)
    m = jnp.max(s, axis=0, keepdims=True)
    p = jnp.exp(s - m)
    l = jnp.sum(p, axis=0, keepdims=True)
    ot = jnp.dot(vt, p.astype(BF16), preferred_element_type=F32) * (1.0 / l)
    lam = jnp.broadcast_to(m + jnp.log(l), (hd, 2 * nq))
    o_scr[rows, :] = jnp.concatenate([ot[:hd, :nq], ot[hd:, nq:]], axis=0).T
    lam_scr[rows, :] = jnp.concatenate([lam[:, :nq], lam[:, nq:]], axis=0).T


def _attn_prompt_kernel(*refs, seq):
    qkv = refs[:9]
    y_ref = refs[9]
    kvt_refs = refs[10:13]
    o_scrs, lam_scrs = refs[13:16], refs[16:19]
    nq = ATTN_BLOCK
    for g, (window, dil) in enumerate(ATTN_PATTERNS):
        q_ref, k_ref, v_ref = qkv[3 * g:3 * g + 3]
        o_scr, lam_scr = o_scrs[g], lam_scrs[g]
        keep = min(window, seq)
        for c in range(keep // LANES):
            src = slice(seq - keep + c * LANES, seq - keep + (c + 1) * LANES)
            kvt_refs[g][0, :, c * LANES:(c + 1) * LANES] = k_ref[src, :].T
            kvt_refs[g][1, :, c * LANES:(c + 1) * LANES] = v_ref[src, :].T
        n_blk = seq // (dil * nq)
        for r in range(dil):
            for n in range(n_blk):
                _attn_block(q_ref, k_ref, v_ref, o_scr, lam_scr, r + dil * nq * n, dil, n > 0)

    rows_per = 256

    def merge(i, carry):
        rows = pl.ds(pl.multiple_of(i * rows_per, rows_per), rows_per)
        l0, l1, l2 = lam_scrs[0][rows, :], lam_scrs[1][rows, :], lam_scrs[2][rows, :]
        mx = jnp.maximum(jnp.maximum(l0, l1), l2)
        e0, e1, e2 = jnp.exp(l0 - mx), jnp.exp(l1 - mx), jnp.exp(l2 - mx)
        num = e0 * o_scrs[0][rows, :] + e1 * o_scrs[1][rows, :] + e2 * o_scrs[2][rows, :]
        y_ref[rows, :] = (num / (e0 + e1 + e2)).astype(BF16)
        return carry

    lax.fori_loop(0, seq // rows_per, merge, 0)


def attn_prompt(qkv, nb, seq):
    for window, dil in ATTN_PATTERNS:
        assert window // dil == ATTN_BLOCK and seq % (dil * ATTN_BLOCK) == 0
    n_pairs = ATTN_WIDTH // LANES
    keeps = [min(window, seq) for window, _ in ATTN_PATTERNS]
    in_specs = []
    for g in range(len(ATTN_PATTERNS)):
        for j in range(3):
            in_specs.append(pl.BlockSpec(
                (None, seq, LANES), lambda b, hp, g=g, j=j: (b, 0, (g * 3 + j) * n_pairs + hp)))
    outs = pl.pallas_call(
        functools.partial(_attn_prompt_kernel, seq=seq),
        grid=(nb, n_pairs),
        in_specs=in_specs,
        out_specs=[pl.BlockSpec((None, seq, LANES), lambda b, hp: (b, 0, hp))] + [
            pl.BlockSpec((None, 2, LANES, keep), lambda b, hp: (b, 0, hp, 0)) for keep in keeps],
        out_shape=[jax.ShapeDtypeStruct((nb, seq, ATTN_WIDTH), BF16)] + [
            jax.ShapeDtypeStruct((nb, 2, ATTN_WIDTH, keep), F32) for keep in keeps],
        scratch_shapes=[pltpu.VMEM((seq, LANES), F32)] * 6,
        compiler_params=_cparams(("parallel", "parallel")),
        name="attn_prompt")(*([qkv] * 9))
    kv = [jnp.transpose(kvt.reshape(nb, 2, ATTN_HEADS, ATTN_HEAD_DIM, keep), (0, 4, 1, 2, 3))
          for kvt, keep in zip(outs[1:], keeps)]
    return outs[0], kv


_DEC_ATTN_HEADS = 4


def _attn_decode_kernel(qkv_ref, c0_ref, c1_ref, c2_ref, y_ref, pad_ref):
    caches = (c0_ref, c1_ref, c2_ref)
    hg = pl.program_id(1)
    hd = ATTN_HEAD_DIM
    rnd = lambda a: a.astype(BF16).astype(F32)
    lane = lax.broadcasted_iota(I32, (hd, LANES), 1)

    def head_columns(a):
        pad_ref[...] = jnp.zeros_like(pad_ref)
        pad_ref[0:ATTN_HEADS, 0:hd] = a
        return pad_ref[...].T[0:hd, :]

    cols = [[rnd(head_columns(qkv_ref[g, j])) for j in range(3)] for g in range(len(ATTN_PATTERNS))]
    acc = jnp.zeros((hd, LANES), F32)
    for hh in range(_DEC_ATTN_HEADS):
        h = hg * _DEC_ATTN_HEADS + hh
        pick = lambda a: jnp.sum(jnp.where(lane == h, a, 0.0), axis=-1, keepdims=True)
        outs, lams = [], []
        for g, (window, dil) in enumerate(ATTN_PATTERNS):
            qc, knc, vnc = (pick(a) for a in cols[g])
            kt = rnd(caches[g][0, hh])
            vt = rnd(caches[g][1, hh])
            s = jnp.sum(kt * qc, axis=0, keepdims=True) * ATTN_SCALE
            if dil > 1:
                tok = lax.broadcasted_iota(I32, (1, window), 1)
                s = jnp.where(tok % dil == 0, s, -jnp.inf)
            s_new = jnp.sum(knc * qc, axis=0, keepdims=True) * ATTN_SCALE
            m = jnp.maximum(jnp.max(s, axis=-1, keepdims=True), s_new)
            p = jnp.exp(s - m)
            p_new = jnp.exp(s_new - m)
            l = jnp.sum(p, axis=-1, keepdims=True) + p_new
            o = (jnp.sum(vt * rnd(p), axis=-1, keepdims=True) + rnd(p_new) * vnc) / l
            outs.append(o)
            lams.append(m + jnp.log(l))
        mx = jnp.maximum(jnp.maximum(lams[0], lams[1]), lams[2])
        es = [jnp.exp(lm - mx) for lm in lams]
        y = (es[0] * outs[0] + es[1] * outs[1] + es[2] * outs[2]) / (es[0] + es[1] + es[2])
        acc = jnp.where(lane == hh, y, acc)
    y_ref[...] = acc


def attn_decode(qkv_s, caches):
    nb = qkv_s.shape[0]
    hpb = _DEC_ATTN_HEADS
    in_specs = [pl.BlockSpec((None, 3, 3, ATTN_HEADS, ATTN_HEAD_DIM), lambda b, g: (b, 0, 0, 0, 0))]
    args = [qkv_s]
    for c, (window, dil) in zip(caches, ATTN_PATTERNS):
        assert c.shape[1] == window and window % dil == 0, "cache must hold a full window"
        args.append(jnp.transpose(c, (0, 2, 3, 4, 1)))
        in_specs.append(pl.BlockSpec((None, 2, hpb, ATTN_HEAD_DIM, window),
                                     lambda b, g: (b, 0, g, 0, 0)))
    y = pl.pallas_call(
        _attn_decode_kernel, grid=(nb, ATTN_HEADS // hpb),
        in_specs=in_specs,
        out_specs=pl.BlockSpec((None, None, ATTN_HEAD_DIM, LANES), lambda b, g: (b, g, 0, 0)),
        out_shape=jax.ShapeDtypeStruct((nb, ATTN_HEADS // hpb, ATTN_HEAD_DIM, LANES), F32),
        scratch_shapes=[pltpu.VMEM((LANES, LANES), F32)],
        compiler_params=_cparams(("parallel", "arbitrary")),
        name="attn_decode")(*args)
    return jnp.transpose(y[..., :hpb], (0, 1, 3, 2)).reshape(nb, ATTN_WIDTH)


_R_E1, _R_E2, _R_RANK1, _R_RANK2, _R_W1, _R_W2 = range(6)


def _router_kernel(x_ref, g_ref, wr_ref, info_ref, cnt_ref, carry_ref):
    tm = x_ref.shape[0]

    @pl.when(pl.program_id(0) == 0)
    def _():
        carry_ref[...] = jnp.zeros_like(carry_ref)

    xn = _rms(x_ref[...], g_ref[...])
    logits = _mm(xn, wr_ref[...])
    lane = lax.broadcasted_iota(I32, (tm, LANES), 1)
    lanef = lane.astype(F32)
    big = float(LANES)
    neg = -jnp.inf

    is_g = (lane >= GROUP_LOGIT_LANE) & (lane < GROUP_LOGIT_LANE + N_EXPERT_GROUPS)
    gl = jnp.where(is_g, logits, neg)
    gmax = jnp.max(gl, axis=-1, keepdims=True)
    grp = jnp.min(jnp.where(gl == gmax, lanef, big), axis=-1, keepdims=True) - GROUP_LOGIT_LANE
    g_gate = 1.0 / jnp.sum(jnp.exp(gl - gmax), axis=-1, keepdims=True)

    lo = grp * EXPERTS_PER_GROUP
    in_grp = (lanef >= lo) & (lanef < lo + EXPERTS_PER_GROUP)
    el = jnp.where(in_grp, logits, neg)
    m1 = jnp.max(el, axis=-1, keepdims=True)
    i1 = jnp.min(jnp.where(el == m1, lanef, big), axis=-1, keepdims=True)
    el2 = jnp.where(lanef == i1, neg, el)
    m2 = jnp.max(el2, axis=-1, keepdims=True)
    i2 = jnp.min(jnp.where(el2 == m2, lanef, big), axis=-1, keepdims=True)
    ratio = jnp.exp(m2 - m1)
    w1 = g_gate / (1.0 + ratio)
    w2 = g_gate * ratio / (1.0 + ratio)

    oh1 = lanef == i1
    oh2 = lanef == i2
    a = (oh1 | oh2).astype(BF16)
    ri = lax.broadcasted_iota(I32, (tm, tm), 0)
    ci = lax.broadcasted_iota(I32, (tm, tm), 1)
    before = (ci < ri).astype(BF16)
    pref = jnp.dot(before, a, preferred_element_type=F32) + carry_ref[...]
    rank1 = jnp.sum(jnp.where(oh1, pref, 0.0), axis=-1, keepdims=True)
    rank2 = jnp.sum(jnp.where(oh2, pref, 0.0), axis=-1, keepdims=True)
    carry_ref[...] = carry_ref[...] + jnp.sum(a.astype(F32), axis=0, keepdims=True)

    info = jnp.zeros((tm, LANES), F32)
    for ln, val in ((_R_E1, i1), (_R_E2, i2), (_R_RANK1, rank1), (_R_RANK2, rank2),
                    (_R_W1, w1), (_R_W2, w2)):
        info = jnp.where(lane == ln, val, info)
    info_ref[...] = info
    cnt_ref[...] = carry_ref[...]


def moe_router(x, g, wr, *, tm):
    t, d = x.shape
    return pl.pallas_call(
        _router_kernel, grid=(t // tm,),
        in_specs=[pl.BlockSpec((tm, d), lambda i: (i, 0)),
                  pl.BlockSpec((1, d), lambda i: (0, 0)),
                  pl.BlockSpec((d, LANES), lambda i: (0, 0))],
        out_specs=[pl.BlockSpec((tm, LANES), lambda i: (i, 0)),
                   pl.BlockSpec((1, LANES), lambda i: (0, 0))],
        out_shape=[jax.ShapeDtypeStruct((t, LANES), F32), jax.ShapeDtypeStruct((1, LANES), F32)],
        scratch_shapes=[pltpu.VMEM((1, LANES), F32)],
        compiler_params=_cparams(("arbitrary",)),
        name="moe_router")(x, g, wr)


_ROW_UNROLL = 8
_ROW_TILES = D_MODEL // LANES
assert _ROW_TILES == SUBLANES


def _rows_to_tiles(ref, val):
    n = val.shape[0]
    for j in range(_ROW_TILES):
        ref[pl.ds(j, n, stride=_ROW_TILES), :] = val[:, j * LANES:(j + 1) * LANES]


def _tiles_to_rows(ref, n):
    return jnp.concatenate([ref[pl.ds(j, n, stride=_ROW_TILES), :] for j in range(_ROW_TILES)], axis=1)


def _tile_at(ref, row8):
    return ref.at[pl.ds(pl.multiple_of(row8, _ROW_TILES), _ROW_TILES)]


def _pos_blocks(pos, tm):
    n = pos.shape[0] // tm
    return (pos * _ROW_TILES).reshape(n, tm, 2).transpose(0, 2, 1).reshape(n, 1, 2 * tm)


def _row_copies(n_rows, make_copies):
    def body(i, carry):
        for u in range(_ROW_UNROLL):
            for cp in make_copies(i * _ROW_UNROLL + u):
                cp.start()
        return carry
    lax.fori_loop(0, n_rows // _ROW_UNROLL, body, 0)


def _dispatch_kernel(pos_ref, x_ref, g_ref, xs_in_ref, xs_ref, xn_ref, sem):
    del xs_in_ref
    tm = x_ref.shape[0]
    _rows_to_tiles(xn_ref, _rms(x_ref[...], g_ref[...]))

    def copies(r):
        src = _tile_at(xn_ref, r * _ROW_TILES)
        return [pltpu.make_async_copy(src, _tile_at(xs_ref, pos_ref[0, 0, k * tm + r]), sem)
                for k in range(2)]

    _row_copies(tm, copies)
    for _ in range(2):
        pltpu.make_async_copy(xn_ref, xs_ref.at[pl.ds(0, tm * _ROW_TILES)], sem).wait()


def moe_dispatch(x, g, pos, xs_init, *, tm):
    t, d = x.shape
    n = t // tm
    pos_blk = _pos_blocks(pos, tm)
    return pl.pallas_call(
        _dispatch_kernel, grid=(n,),
        in_specs=[pl.BlockSpec((1, 1, 2 * tm), lambda i: (i, 0, 0), memory_space=pltpu.SMEM),
                  pl.BlockSpec((tm, d), lambda i: (i, 0)),
                  pl.BlockSpec((1, d), lambda i: (0, 0)),
                  pl.BlockSpec(memory_space=pl.ANY)],
        out_specs=pl.BlockSpec(memory_space=pl.ANY),
        out_shape=jax.ShapeDtypeStruct(xs_init.shape, F32),
        scratch_shapes=[pltpu.VMEM((tm * _ROW_TILES, LANES), F32), pltpu.SemaphoreType.DMA(())],
        input_output_aliases={3: 0},
        compiler_params=_cparams(("arbitrary",)),
        name="moe_dispatch")(pos_blk, x, g, xs_init)


def _expert_kernel(te_ref, tv_ref, xs_ref, wgu_ref, wdn_ref, ys_ref, wgu_bf, wdn_bf):
    t = pl.program_id(0)
    tmx = xs_ref.shape[0] // _ROW_TILES
    changed = jnp.logical_or(t == 0, te_ref[t] != te_ref[jnp.maximum(t - 1, 0)])

    @pl.when(changed)
    def _():
        wgu_bf[...] = wgu_ref[...].astype(BF16)
        wdn_bf[...] = wdn_ref[...].astype(BF16)

    @pl.when(tv_ref[t] > 0)
    def _():
        x = _tiles_to_rows(xs_ref, tmx).astype(BF16)
        gu = jnp.dot(x, wgu_bf[...], preferred_element_type=F32)
        act = _silu(gu[:, :D_EXPERT]) * gu[:, D_EXPERT:]
        _rows_to_tiles(ys_ref, jnp.dot(act.astype(BF16), wdn_bf[...], preferred_element_type=F32))

    @pl.when(tv_ref[t] == 0)
    def _():
        ys_ref[...] = jnp.zeros_like(ys_ref)


def moe_experts(xs, w_gate_up, w_down, layer, tile_expert, tile_valid, *, tmx):
    d = w_gate_up.shape[2]
    blk = tmx * _ROW_TILES
    n_tiles = xs.shape[0] // blk
    grid_spec = pltpu.PrefetchScalarGridSpec(
        num_scalar_prefetch=2, grid=(n_tiles,),
        in_specs=[pl.BlockSpec((blk, LANES), lambda t, te, tv: (t, 0)),
                  pl.BlockSpec((None, None, d, 2 * D_EXPERT), lambda t, te, tv: (layer, te[t], 0, 0)),
                  pl.BlockSpec((None, None, D_EXPERT, d), lambda t, te, tv: (layer, te[t], 0, 0))],
        out_specs=pl.BlockSpec((blk, LANES), lambda t, te, tv: (t, 0)),
        scratch_shapes=[pltpu.VMEM((d, 2 * D_EXPERT), BF16), pltpu.VMEM((D_EXPERT, d), BF16)])
    return pl.pallas_call(
        _expert_kernel, grid_spec=grid_spec,
        out_shape=jax.ShapeDtypeStruct(xs.shape, F32),
        compiler_params=_cparams(("arbitrary",)),
        name="moe_experts")(tile_expert, tile_valid, xs, w_gate_up, w_down)


def _combine_kernel(pos_ref, x_ref, info_ref, g_ref, ys_ref, o_ref, buf0_ref, buf1_ref, sem, *,
                    final_norm):
    tm = x_ref.shape[0]
    bufs = (buf0_ref, buf1_ref)

    def copies(r):
        return [pltpu.make_async_copy(_tile_at(ys_ref, pos_ref[0, 0, k * tm + r]),
                                      _tile_at(bufs[k], r * _ROW_TILES), sem)
                for k in range(2)]

    _row_copies(tm, copies)
    for k in range(2):
        pltpu.make_async_copy(ys_ref.at[pl.ds(0, tm * _ROW_TILES)], bufs[k], sem).wait()
    info = info_ref[...]
    out = (x_ref[...] + info[:, _R_W1:_R_W1 + 1] * _tiles_to_rows(buf0_ref, tm)
           + info[:, _R_W2:_R_W2 + 1] * _tiles_to_rows(buf1_ref, tm))
    if final_norm:
        out = _rms(out, g_ref[...])
    o_ref[...] = out


def moe_combine(x, info, pos, ys, g_final, *, tm, final_norm):
    t, d = x.shape
    n = t // tm
    pos_blk = _pos_blocks(pos, tm)
    return pl.pallas_call(
        functools.partial(_combine_kernel, final_norm=final_norm), grid=(n,),
        in_specs=[pl.BlockSpec((1, 1, 2 * tm), lambda i: (i, 0, 0), memory_space=pltpu.SMEM),
                  pl.BlockSpec((tm, d), lambda i: (i, 0)),
                  pl.BlockSpec((tm, LANES), lambda i: (i, 0)),
                  pl.BlockSpec((1, d), lambda i: (0, 0)),
                  pl.BlockSpec(memory_space=pl.ANY)],
        out_specs=pl.BlockSpec((tm, d), lambda i: (i, 0)),
        out_shape=jax.ShapeDtypeStruct((t, d), F32),
        scratch_shapes=[pltpu.VMEM((tm * _ROW_TILES, LANES), F32)] * 2 + [pltpu.SemaphoreType.DMA(())],
        compiler_params=_cparams(("arbitrary",)),
        name="moe_combine")(pos_blk, x, info, g_final, ys)


_SC_CORES = 2
_SC_SUBCORES = 16
_SC_CHUNK = 64


def sc_gather_rows(table, idx):
    b = idx.shape[0]
    n_workers = _SC_CORES * _SC_SUBCORES
    per_worker = b // n_workers
    assert per_worker * n_workers == b and per_worker % _SC_CHUNK == 0
    mesh = plsc.VectorSubcoreMesh(core_axis_name="c", subcore_axis_name="s",
                                  num_cores=_SC_CORES, num_subcores=_SC_SUBCORES)

    @functools.partial(
        pl.kernel, mesh=mesh,
        out_type=jax.ShapeDtypeStruct((b,) + table.shape[1:], table.dtype),
        scratch_types=[pltpu.VMEM((_SC_CHUNK,), I32),
                       pltpu.VMEM((_SC_CHUNK,) + table.shape[1:], table.dtype),
                       pltpu.SemaphoreType.DMA],
        name="sc_gather_rows")
    def gather(table_hbm, idx_hbm, out_hbm, idx_v, rows_v, sem):
        worker = lax.axis_index("s") * _SC_CORES + lax.axis_index("c")
        base = worker * per_worker

        def chunk(i, carry):
            off = pl.multiple_of(base + i * _SC_CHUNK, _SC_CHUNK)
            pltpu.sync_copy(idx_hbm.at[pl.ds(off, _SC_CHUNK)], idx_v)
            pltpu.async_copy(table_hbm.at[idx_v], rows_v, sem).wait()
            pltpu.sync_copy(rows_v, out_hbm.at[pl.ds(off, _SC_CHUNK)])
            return carry

        lax.fori_loop(0, per_worker // _SC_CHUNK, chunk, 0)

    return gather(table, idx)


def _combine_dense_kernel(x_ref, info_ref, g_ref, y0_ref, y1_ref, o_ref, *, final_norm):
    tm = x_ref.shape[0]
    info = info_ref[...]
    out = (x_ref[...] + info[:, _R_W1:_R_W1 + 1] * _tiles_to_rows(y0_ref, tm)
           + info[:, _R_W2:_R_W2 + 1] * _tiles_to_rows(y1_ref, tm))
    if final_norm:
        out = _rms(out, g_ref[...])
    o_ref[...] = out


def moe_combine_dense(x, info, y_slots, g_final, *, tm, final_norm):
    t, d = x.shape
    n = t // tm
    blk = tm * _ROW_TILES
    return pl.pallas_call(
        functools.partial(_combine_dense_kernel, final_norm=final_norm), grid=(n,),
        in_specs=[pl.BlockSpec((tm, d), lambda i: (i, 0)),
                  pl.BlockSpec((tm, LANES), lambda i: (i, 0)),
                  pl.BlockSpec((1, d), lambda i: (0, 0)),
                  pl.BlockSpec((blk, LANES), lambda i: (i, 0)),
                  pl.BlockSpec((blk, LANES), lambda i: (n + i, 0))],
        out_specs=pl.BlockSpec((tm, d), lambda i: (i, 0)),
        out_shape=jax.ShapeDtypeStruct((t, d), F32),
        compiler_params=_cparams(("parallel",)),
        name="moe_combine_dense")(x, info, g_final, y_slots, y_slots)


def _router_weight(w_group, w_expert):
    d = w_group.shape[0]
    we = jnp.transpose(w_expert, (1, 0, 2)).reshape(d, N_EXPERTS)
    w = jnp.concatenate([we, w_group], axis=1)
    return jnp.pad(w, ((0, 0), (0, LANES - w.shape[1]))).astype(BF16)


def hier_moe_layer(x, norm_g, w_group, w_expert, w_gate_up, w_down, layer, g_final, *, tm, tmx,
                   final_norm):
    t, d = x.shape
    g = norm_g.reshape(1, d)
    info, counts = moe_router(x, g, _router_weight(w_group, w_expert), tm=tm)

    counts = counts[0, :N_EXPERTS].astype(I32)
    padded = ((counts + tmx - 1) // tmx) * tmx
    ends = jnp.cumsum(padded)
    offsets = ends - padded
    e12 = info[:, _R_E1:_R_E2 + 1].astype(I32)
    rank12 = info[:, _R_RANK1:_R_RANK2 + 1].astype(I32)
    pos = offsets[e12] + rank12
    n_rows = 2 * t + N_EXPERTS * tmx
    n_tiles = n_rows // tmx
    tile_start = jnp.arange(n_tiles, dtype=I32) * tmx
    tile_valid = (tile_start < ends[-1]).astype(I32)
    probe = jnp.minimum(tile_start, jnp.maximum(ends[-1] - 1, 0))
    tile_expert = jnp.sum((probe[:, None] >= ends[None, :]).astype(I32), axis=1)
    tile_expert = jnp.minimum(tile_expert, N_EXPERTS - 1)

    depth = w_gate_up.shape[0]
    xs = moe_dispatch(x, g, pos, jnp.zeros((n_rows * _ROW_TILES, LANES), F32), tm=tm)
    ys = moe_experts(xs, w_gate_up.reshape(depth, N_EXPERTS, d, 2 * D_EXPERT),
                     w_down.reshape(depth, N_EXPERTS, D_EXPERT, d), layer,
                     tile_expert, tile_valid, tmx=tmx)
    if (2 * t) % (_SC_CORES * _SC_SUBCORES * _SC_CHUNK) == 0:
        rows = sc_gather_rows(ys.reshape(n_rows, _ROW_TILES, LANES), pos.T.reshape(2 * t))
        return moe_combine_dense(x, info, rows.reshape(2 * t * _ROW_TILES, LANES),
                                 g_final.reshape(1, d), tm=tm, final_norm=final_norm)
    return moe_combine(x, info, pos, ys, g_final.reshape(1, d), tm=tm, final_norm=final_norm)


def kernel(x_prompt, x_sample, state_conv, state_ssm, cache_kv_w128, cache_kv_w512, cache_kv_w2048, norm_mix, norm_ffn, norm_final, ssm_in_proj, ssm_conv_w, ssm_conv_b, ssm_dt_bias, ssm_a_log, ssm_d, ssm_norm, ssm_out_proj, attn_qkv, attn_out, moe_route_group, moe_route_expert, moe_w_gate_up, moe_w_down):
    nb, seq, d = x_prompt.shape
    ns = x_sample.shape[0]
    t = nb * seq
    tm = min(t, 1024)
    n_groups = len(ATTN_PATTERNS)
    kv_caches = (cache_kv_w128, cache_kv_w512, cache_kv_w2048)

    ssm_params = (ssm_in_proj[0], ssm_conv_w[0], ssm_conv_b[0], ssm_dt_bias[0], ssm_a_log[0],
                  ssm_d[0], ssm_norm[0], ssm_out_proj[0])
    xp, conv_p, ssm_p = _mamba_prompt(x_prompt, norm_mix[0], *ssm_params)
    xs, conv_s, ssm_s = _mamba_sample(x_sample, norm_mix[0], state_conv[0], state_ssm[0], *ssm_params)

    def moe(x2d, i, tm_moe, tmx, final_norm):
        return hier_moe_layer(x2d, norm_ffn[i], moe_route_group[i], moe_route_expert[i],
                              moe_w_gate_up, moe_w_down, i, norm_final,
                              tm=tm_moe, tmx=tmx, final_norm=final_norm)

    xp = moe(xp.reshape(t, d), 0, 256, 256, False)
    xs = moe(xs.reshape(ns, d), 0, ns, 16, False)

    g1 = norm_mix[1].reshape(1, d)
    w_qkv = attn_qkv[0].astype(BF16)
    w_o = attn_out[0].astype(BF16)
    qkv_p = norm_matmul(xp, g1, w_qkv, tm=tm, tn=1024)
    y_p, kv_p = attn_prompt(qkv_p.reshape(nb, seq, -1), nb, seq)
    xp = matmul_res(y_p.reshape(t, ATTN_WIDTH), w_o, xp, tm=tm)

    qkv_s = norm_matmul(xs, g1, w_qkv, tm=ns, tn=1024)
    qkv_s5 = qkv_s.reshape(ns, n_groups, 3, ATTN_HEADS, ATTN_HEAD_DIM)
    y_s = attn_decode(qkv_s5, [c[0] for c in kv_caches])
    xs = matmul_res(y_s.reshape(ns, ATTN_WIDTH), w_o, xs, tm=ns)
    kv_s = [qkv_s5[:, g, 1:3][:, None] for g in range(n_groups)]

    y_prompt = moe(xp, 1, 256, 256, True).reshape(nb, seq, d)
    y_sample = moe(xs, 1, ns, 16, True).reshape(ns, 1, d)

    return (y_prompt, y_sample, conv_p[None], conv_s[None], ssm_p[None], ssm_s[None],
            kv_p[0][None], kv_s[0][None], kv_p[1][None], kv_s[1][None], kv_p[2][None], kv_s[2][None])
```

```python
import functools
import math

import jax
import jax.numpy as jnp
from jax import lax
from jax.experimental import pallas as pl
from jax.experimental.pallas import tpu as pltpu
from jax.experimental.pallas import tpu_sc as plsc

F32 = jnp.float32
BF16 = jnp.bfloat16
I32 = jnp.int32

EPS = 1e-6
D_MODEL = 1024
LANES = 128
SUBLANES = 8

SSM_D_INNER = 2048
SSM_HEAD_DIM = 64
SSM_HEADS = 32
SSM_GROUPS = 4
SSM_HEADS_PER_GROUP = 8
SSM_D_STATE = 128
CONV_WIDTH = 4
SSD_CHUNK = 128
SSM_BC_DIM = 2 * SSM_GROUPS * SSM_D_STATE
SSM_CONV_DIM = SSM_D_INNER + SSM_BC_DIM

ATTN_PATTERNS = ((128, 1), (512, 4), (2048, 16))
ATTN_HEAD_DIM = 64
ATTN_HEADS = 16
ATTN_WIDTH = 1024
ATTN_BLOCK = 128
ATTN_SCALE = ATTN_HEAD_DIM ** -0.5

N_EXPERT_GROUPS = 4
EXPERTS_PER_GROUP = 8
N_EXPERTS = N_EXPERT_GROUPS * EXPERTS_PER_GROUP
D_EXPERT = 512
GROUP_LOGIT_LANE = N_EXPERTS

VMEM_LIMIT = 56 * 1024 * 1024


def _cparams(sem):
    return pltpu.CompilerParams(dimension_semantics=sem, vmem_limit_bytes=VMEM_LIMIT)


def _rms(x, g):
    ms = jnp.mean(x * x, axis=-1, keepdims=True)
    return x * lax.rsqrt(ms + EPS) * g


def _silu(x):
    return x * jax.nn.sigmoid(x)


def _softplus(x):
    return jnp.maximum(x, 0.0) + jnp.log1p(jnp.exp(-jnp.abs(x)))


def _mm(a, b):
    prec = lax.Precision.HIGHEST if b.dtype == F32 else None
    return jnp.dot(a.astype(b.dtype), b, preferred_element_type=F32, precision=prec)


def _norm_matmul_kernel(x_ref, g_ref, w_ref, o_ref, xn_ref):
    @pl.when(pl.program_id(1) == 0)
    def _():
        xn_ref[...] = _rms(x_ref[...], g_ref[...]).astype(xn_ref.dtype)

    o_ref[...] = _mm(xn_ref[...], w_ref[...])


def _norm_matmul2_kernel(x_ref, g_ref, w_ref, w2_ref, o_ref, o2_ref, xn_ref):
    @pl.when(pl.program_id(1) == 0)
    def _():
        xn = _rms(x_ref[...], g_ref[...]).astype(xn_ref.dtype)
        xn_ref[...] = xn
        o2_ref[...] = _mm(xn, w2_ref[...])

    o_ref[...] = _mm(xn_ref[...], w_ref[...])


def norm_matmul(x, g, w, w2=None, *, tm, tn):
    t, d = x.shape
    n = w.shape[1]
    grid = (t // tm, n // tn)
    x_spec = pl.BlockSpec((tm, d), lambda i, j: (i, 0))
    g_spec = pl.BlockSpec((1, d), lambda i, j: (0, 0))
    w_spec = pl.BlockSpec((d, tn), lambda i, j: (0, j))
    o_spec = pl.BlockSpec((tm, tn), lambda i, j: (i, j))
    scratch = [pltpu.VMEM((tm, d), w.dtype)]
    if w2 is None:
        return pl.pallas_call(
            _norm_matmul_kernel, grid=grid,
            in_specs=[x_spec, g_spec, w_spec], out_specs=o_spec,
            out_shape=jax.ShapeDtypeStruct((t, n), F32),
            scratch_shapes=scratch,
            compiler_params=_cparams(("parallel", "arbitrary")),
            name="norm_matmul")(x, g, w)
    n2 = w2.shape[1]
    return pl.pallas_call(
        _norm_matmul2_kernel, grid=grid,
        in_specs=[x_spec, g_spec, w_spec, pl.BlockSpec((d, n2), lambda i, j: (0, 0))],
        out_specs=[o_spec, pl.BlockSpec((tm, n2), lambda i, j: (i, 0))],
        out_shape=[jax.ShapeDtypeStruct((t, n), F32), jax.ShapeDtypeStruct((t, n2), F32)],
        scratch_shapes=scratch,
        compiler_params=_cparams(("parallel", "arbitrary")),
        name="norm_matmul2")(x, g, w, w2)


def _matmul_res_kernel(a_ref, w_ref, r_ref, o_ref):
    o_ref[...] = r_ref[...] + _mm(a_ref[...], w_ref[...])


def matmul_res(a, w, res, *, tm):
    t, k = a.shape
    n = w.shape[1]
    return pl.pallas_call(
        _matmul_res_kernel, grid=(t // tm,),
        in_specs=[pl.BlockSpec((tm, k), lambda i: (i, 0)),
                  pl.BlockSpec((k, n), lambda i: (0, 0)),
                  pl.BlockSpec((tm, n), lambda i: (i, 0))],
        out_specs=pl.BlockSpec((tm, n), lambda i: (i, 0)),
        out_shape=jax.ShapeDtypeStruct((t, n), F32),
        compiler_params=_cparams(("parallel",)),
        name="matmul_res")(a, w, res)


_CONV_COLS = 512


def _ssd_kernel(z_ref, xs_ref, bc_ref, dt_ref, cw_ref, cb_ref, dtb_ref, alog_ref,
                dsk_ref, ng_ref,
                y_ref, tail_out_ref, st_ref,
                tail_ref, h_ref, xc_ref, xst_ref, yt_ref, dtt_ref, cst_ref, xde_ref, yn_ref):
    q = SSD_CHUNK
    c = pl.program_id(1)

    @pl.when(c == 0)
    def _():
        tail_ref[...] = jnp.zeros_like(tail_ref)
        h_ref[...] = jnp.zeros_like(h_ref)

    row = lax.broadcasted_iota(I32, (q, _CONV_COLS), 0)
    for k in range(SSM_CONV_DIM // _CONV_COLS):
        lo = k * _CONV_COLS
        if lo < SSM_D_INNER:
            src = xs_ref[0, :, lo:lo + _CONV_COLS]
        else:
            src = bc_ref[0, :, lo - SSM_D_INNER:lo - SSM_D_INNER + _CONV_COLS]
        tl = tail_ref[:, lo:lo + _CONV_COLS]
        w = cw_ref[:, lo:lo + _CONV_COLS]
        acc = src * w[CONV_WIDTH - 1:CONV_WIDTH] + cb_ref[:, lo:lo + _CONV_COLS]
        for s in range(1, CONV_WIDTH):
            cur = pltpu.roll(src, s, axis=0)
            prev = jnp.tile(pltpu.roll(tl, s, axis=0), (q // SUBLANES, 1))
            shifted = jnp.where(row < s, prev, cur)
            acc = acc + shifted * w[CONV_WIDTH - 1 - s:CONV_WIDTH - s]
        xc_ref[:, lo:lo + _CONV_COLS] = _silu(acc)
        tail_ref[:, lo:lo + _CONV_COLS] = src[q - SUBLANES:, :]

    for k in range(SSM_D_INNER // LANES):
        xst_ref[k * LANES:(k + 1) * LANES, :] = xc_ref[:, k * LANES:(k + 1) * LANES].T

    dt = _softplus(dt_ref[0] + dtb_ref[...])
    da = dt * (-jnp.exp(alog_ref[...]))
    ri = lax.broadcasted_iota(I32, (q, q), 0)
    ci = lax.broadcasted_iota(I32, (q, q), 1)
    tril = (ci <= ri).astype(F32)
    cs = jnp.dot(tril, da, preferred_element_type=F32, precision=lax.Precision.HIGHEST)
    dtt_ref[...] = dt.T
    cst_ref[...] = cs.T

    causal_t = ci >= ri
    hpg = SSM_HEADS_PER_GROUP
    gp = hpg * SSM_HEAD_DIM
    for g in range(SSM_GROUPS):
        b0 = SSM_D_INNER + g * SSM_D_STATE
        c0 = SSM_D_INNER + SSM_GROUPS * SSM_D_STATE + g * SSM_D_STATE
        bm = xc_ref[:, b0:b0 + SSM_D_STATE].astype(BF16)
        cm = xc_ref[:, c0:c0 + SSM_D_STATE].astype(BF16)
        cbt = lax.dot_general(bm, cm, (((1,), (1,)), ((), ())), preferred_element_type=F32)
        hg = h_ref[g * gp:(g + 1) * gp, :].astype(BF16)
        yt_ref[g * gp:(g + 1) * gp, :] = lax.dot_general(
            hg, cm, (((1,), (1,)), ((), ())), preferred_element_type=F32)

        for e in range(hpg):
            hd = g * hpg + e
            rows = slice(hd * SSM_HEAD_DIM, (hd + 1) * SSM_HEAD_DIM)
            csr = cst_ref[hd:hd + 1, :]
            rowb = jnp.broadcast_to(csr, (q, q))
            colb = rowb.T
            dec = jnp.exp(jnp.where(causal_t, rowb - colb, -jnp.inf))
            mt = (cbt * dec).astype(BF16)
            dtr = dtt_ref[hd:hd + 1, :]
            xsh = xst_ref[rows, :]
            xdt = xsh * dtr
            ydt = jnp.dot(xdt.astype(BF16), mt, preferred_element_type=F32)
            cs_end = csr[:, q - 1:q]
            yt_ref[rows, :] = yt_ref[rows, :] * jnp.exp(csr) + ydt + xsh * dsk_ref[rows, :]
            xde_ref[e * SSM_HEAD_DIM:(e + 1) * SSM_HEAD_DIM, :] = (
                xdt * jnp.exp(cs_end - csr)).astype(BF16)
            h_ref[rows, :] = h_ref[rows, :] * jnp.exp(cs_end)
        h_ref[g * gp:(g + 1) * gp, :] = h_ref[g * gp:(g + 1) * gp, :] + jnp.dot(
            xde_ref[...], bm, preferred_element_type=F32)

    ssq = jnp.zeros((q, 1), F32)
    for k in range(SSM_D_INNER // LANES):
        yk = yt_ref[k * LANES:(k + 1) * LANES, :].T * _silu(z_ref[0, :, k * LANES:(k + 1) * LANES])
        yn_ref[:, k * LANES:(k + 1) * LANES] = yk
        ssq = ssq + jnp.sum(yk * yk, axis=-1, keepdims=True)
    scale = lax.rsqrt(ssq * (1.0 / SSM_D_INNER) + EPS)
    y_ref[0] = (yn_ref[...] * scale * ng_ref[...]).astype(BF16)

    @pl.when(c == pl.num_programs(1) - 1)
    def _():
        tail_out_ref[0] = tail_ref[...]
        st_ref[0] = h_ref[...]


def ssd_prompt(proj, dt_raw, conv_w, conv_b, dt_bias, a_log, d_rows, norm_g, nb, seq):
    nc = seq // SSD_CHUNK
    q = SSD_CHUNK
    const2 = lambda b, c: (0, 0)
    return pl.pallas_call(
        _ssd_kernel, grid=(nb, nc),
        in_specs=[
            pl.BlockSpec((1, q, SSM_D_INNER), lambda b, c: (b, c, 0)),
            pl.BlockSpec((1, q, SSM_D_INNER), lambda b, c: (b, c, 1)),
            pl.BlockSpec((1, q, SSM_BC_DIM), lambda b, c: (b, c, 2 * SSM_D_INNER // SSM_BC_DIM)),
            pl.BlockSpec((1, q, LANES), lambda b, c: (b, c, 0)),
            pl.BlockSpec((CONV_WIDTH, SSM_CONV_DIM), const2),
            pl.BlockSpec((1, SSM_CONV_DIM), const2),
            pl.BlockSpec((1, LANES), const2),
            pl.BlockSpec((1, LANES), const2),
            pl.BlockSpec((SSM_D_INNER, LANES), const2),
            pl.BlockSpec((1, SSM_D_INNER), const2),
        ],
        out_specs=[
            pl.BlockSpec((1, q, SSM_D_INNER), lambda b, c: (b, c, 0)),
            pl.BlockSpec((1, SUBLANES, SSM_CONV_DIM), lambda b, c: (b, 0, 0)),
            pl.BlockSpec((1, SSM_D_INNER, SSM_D_STATE), lambda b, c: (b, 0, 0)),
        ],
        out_shape=[
            jax.ShapeDtypeStruct((nb, seq, SSM_D_INNER), BF16),
            jax.ShapeDtypeStruct((nb, SUBLANES, SSM_CONV_DIM), F32),
            jax.ShapeDtypeStruct((nb, SSM_D_INNER, SSM_D_STATE), F32),
        ],
        scratch_shapes=[
            pltpu.VMEM((SUBLANES, SSM_CONV_DIM), F32),
            pltpu.VMEM((SSM_D_INNER, SSM_D_STATE), F32),
            pltpu.VMEM((q, SSM_CONV_DIM), F32),
            pltpu.VMEM((SSM_D_INNER, q), F32),
            pltpu.VMEM((SSM_D_INNER, q), F32),
            pltpu.VMEM((LANES, q), F32),
            pltpu.VMEM((LANES, q), F32),
            pltpu.VMEM((SSM_HEADS_PER_GROUP * SSM_HEAD_DIM, q), BF16),
            pltpu.VMEM((q, SSM_D_INNER), F32),
        ],
        compiler_params=_cparams(("parallel", "arbitrary")),
        name="ssd_prompt")(proj, proj, proj, dt_raw, conv_w, conv_b, dt_bias, a_log, d_rows, norm_g)


def _pad_lanes(v, n=LANES):
    return jnp.pad(v.astype(F32), (0, n - v.shape[0])).reshape(1, n)


def _ssm_weights(w_in, dtype):
    n_main = SSM_D_INNER + SSM_CONV_DIM
    w_main = w_in[:, :n_main].astype(dtype)
    w_dt = jnp.pad(w_in[:, n_main:], ((0, 0), (0, LANES - SSM_HEADS))).astype(dtype)
    return w_main, w_dt


def _mamba_prompt(x, norm_g, w_in, conv_w, conv_b, dt_bias, a_log, d_skip, ssm_norm, w_out):
    nb, seq, d = x.shape
    t = nb * seq
    w_main, w_dt = _ssm_weights(w_in, BF16)
    tm = min(t, 1024)
    proj, dt_raw = norm_matmul(x.reshape(t, d), norm_g.reshape(1, d), w_main, w_dt, tm=tm, tn=1024)
    d_rows = jnp.broadcast_to(jnp.repeat(d_skip.astype(F32), SSM_HEAD_DIM)[:, None], (SSM_D_INNER, LANES))
    y, tail, st = ssd_prompt(
        proj.reshape(nb, seq, -1), dt_raw.reshape(nb, seq, LANES),
        conv_w, conv_b.reshape(1, -1), _pad_lanes(dt_bias), _pad_lanes(a_log), d_rows,
        ssm_norm.reshape(1, -1), nb, seq)
    x_new = matmul_res(y.reshape(t, SSM_D_INNER), w_out.astype(BF16), x.reshape(t, d), tm=tm)
    conv_state = tail[:, SUBLANES - (CONV_WIDTH - 1):]
    ssm_state = st.reshape(nb, SSM_HEADS, SSM_HEAD_DIM, SSM_D_STATE)
    return x_new.reshape(nb, seq, d), conv_state, ssm_state


_DEC_HEADS = 4


def _ssd_decode_kernel(zt_ref, xbct_ref, convt_ref, dtt_raw_ref, cwt_ref, cbt_ref, dtbt_ref,
                       alogt_ref, dsk_ref, ngt_ref, wout_ref, res_ref, st_ref,
                       so_ref, out_ref,
                       xct_ref, dtt_ref, dect_ref, bcn_ref, yt_ref, *, nb):
    step = pl.program_id(0)
    rb = 512

    @pl.when(step == 0)
    def _():
        for k in range(SSM_CONV_DIM // rb):
            rows = slice(k * rb, (k + 1) * rb)
            acc = xbct_ref[rows, :] * cwt_ref[rows, CONV_WIDTH - 1:CONV_WIDTH] + cbt_ref[rows, 0:1]
            for j in range(CONV_WIDTH - 1):
                acc = acc + convt_ref[j, rows, :] * cwt_ref[rows, j:j + 1]
            xct_ref[rows, :] = _silu(acc)
        dt = _softplus(dtt_raw_ref[...] + dtbt_ref[:, 0:1])
        dtt_ref[...] = dt
        dect_ref[...] = jnp.exp(dt * (-jnp.exp(alogt_ref[:, 0:1])))
        for k in range(SSM_BC_DIM // LANES):
            bcn_ref[:, k * LANES:(k + 1) * LANES] = xct_ref[
                SSM_D_INNER + k * LANES:SSM_D_INNER + (k + 1) * LANES, :].T
        yt_ref[...] = jnp.zeros_like(yt_ref)

    for hh in range(_DEC_HEADS):
        hd = step * _DEC_HEADS + hh
        r0 = pl.multiple_of(hd * SSM_HEAD_DIM, SSM_HEAD_DIM)
        grp = hd // SSM_HEADS_PER_GROUP
        b_lane = pl.multiple_of(grp * SSM_D_STATE, SSM_D_STATE)
        c_lane = pl.multiple_of(SSM_GROUPS * SSM_D_STATE + grp * SSM_D_STATE, SSM_D_STATE)
        xh = xct_ref[pl.ds(r0, SSM_HEAD_DIM), :]
        dth = dtt_ref[pl.ds(hd, 1), :]
        dech = dect_ref[pl.ds(hd, 1), :]
        xdt = xh * dth
        for b in range(nb):
            brow = bcn_ref[b:b + 1, pl.ds(b_lane, SSM_D_STATE)]
            crow = bcn_ref[b:b + 1, pl.ds(c_lane, SSM_D_STATE)]
            hn = st_ref[b, hh] * dech[:, b:b + 1] + xdt[:, b:b + 1] * brow
            so_ref[b, hh] = hn
            yt_ref[pl.ds(r0, SSM_HEAD_DIM), b:b + 1] = jnp.sum(hn * crow, axis=-1, keepdims=True)

    @pl.when(step == pl.num_programs(0) - 1)
    def _():
        ssq = jnp.zeros((1, LANES), F32)
        for k in range(SSM_D_INNER // rb):
            rows = slice(k * rb, (k + 1) * rb)
            y = (yt_ref[rows, :] + xct_ref[rows, :] * dsk_ref[rows, :]) * _silu(zt_ref[rows, :])
            yt_ref[rows, :] = y
            ssq = ssq + jnp.sum(y * y, axis=0, keepdims=True)
        scale = lax.rsqrt(ssq * (1.0 / SSM_D_INNER) + EPS)
        acc = res_ref[...]
        for k in range(SSM_D_INNER // LANES):
            rows = slice(k * LANES, (k + 1) * LANES)
            yn = (yt_ref[rows, :] * scale * ngt_ref[rows, :]).T
            acc = acc + _mm(yn, wout_ref[rows, :])
        out_ref[...] = acc


def ssd_decode(zt, xbct, convt, dtt_raw, cwt, cbt, dtbt, alogt, d_rows, ngt, w_out, res, state):
    nb = state.shape[0]
    n_steps = SSM_HEADS // _DEC_HEADS
    full = lambda a: pl.BlockSpec(a.shape, lambda s, n=a.ndim: (0,) * n)
    st_spec = pl.BlockSpec((nb, _DEC_HEADS, SSM_HEAD_DIM, SSM_D_STATE), lambda s: (0, s, 0, 0))
    small = (zt, xbct, convt, dtt_raw, cwt, cbt, dtbt, alogt, d_rows, ngt, w_out, res)
    return pl.pallas_call(
        functools.partial(_ssd_decode_kernel, nb=nb), grid=(n_steps,),
        in_specs=[full(a) for a in small] + [st_spec],
        out_specs=[st_spec, full(res)],
        out_shape=[jax.ShapeDtypeStruct(state.shape, F32), jax.ShapeDtypeStruct(res.shape, F32)],
        scratch_shapes=[
            pltpu.VMEM((SSM_CONV_DIM, LANES), F32),
            pltpu.VMEM((LANES, LANES), F32),
            pltpu.VMEM((LANES, LANES), F32),
            pltpu.VMEM((LANES, SSM_BC_DIM), F32),
            pltpu.VMEM((SSM_D_INNER, LANES), F32),
        ],
        compiler_params=_cparams(("arbitrary",)),
        name="ssd_decode")(*small, state)


def _to_lanes(a, n=LANES):
    return jnp.pad(a.T, ((0, 0), (0, n - a.shape[0])))


def _mamba_sample(x, norm_g, conv_state, ssm_state, w_in, conv_w, conv_b, dt_bias, a_log, d_skip,
                  ssm_norm, w_out):
    nb, _, d = x.shape
    assert nb <= LANES and nb % SUBLANES == 0
    w_main, w_dt = _ssm_weights(w_in, F32)
    proj, dt_raw = norm_matmul(x.reshape(nb, d), norm_g.reshape(1, d), w_main, w_dt, tm=nb, tn=512)
    xbc = proj[:, SSM_D_INNER:]
    col = lambda v: jnp.broadcast_to(v.astype(F32)[:, None], (v.shape[0], LANES))
    convt = jnp.pad(jnp.transpose(conv_state, (1, 2, 0)), ((0, 0), (0, 0), (0, LANES - nb)))
    res = jnp.pad(x.reshape(nb, d), ((0, LANES - nb), (0, 0)))
    st, out = ssd_decode(
        _to_lanes(proj[:, :SSM_D_INNER]), _to_lanes(xbc), convt, _to_lanes(dt_raw),
        jnp.pad(conv_w.T, ((0, 0), (0, LANES - CONV_WIDTH))), col(conv_b),
        col(_pad_lanes(dt_bias)[0]), col(_pad_lanes(a_log)[0]),
        col(jnp.repeat(d_skip, SSM_HEAD_DIM)), col(ssm_norm), w_out, res, ssm_state)
    new_conv = jnp.concatenate([conv_state[:, 1:], xbc[:, None, :]], axis=1)
    return out[:nb].reshape(nb, 1, d), new_conv, st


def _attn_block(q_ref, k_ref, v_ref, o_scr, lam_scr, start, dil, has_prev):
    nq = ATTN_BLOCK
    rows = pl.ds(start, nq, stride=dil) if dil > 1 else pl.ds(start, nq)
    hd = ATTN_HEAD_DIM
    q2 = q_ref[rows, :]
    lane = lax.broadcasted_iota(I32, (nq, LANES), 1)
    qblk = jnp.concatenate([jnp.where(lane < hd, q2, 0.0), jnp.where(lane >= hd, q2, 0.0)],
                           axis=0).astype(BF16)
    kj = lax.broadcasted_iota(I32, (nq, 2 * nq), 0)
    qi = lax.broadcasted_iota(I32, (nq, 2 * nq), 1) % nq
    nt = (((1,), (1,)), ((), ()))
    s = lax.dot_general(k_ref[rows, :].astype(BF16), qblk, nt,
                        preferred_element_type=F32) * ATTN_SCALE
    s = jnp.where(kj <= qi, s, -jnp.inf)
    vt = v_ref[rows, :].T.astype(BF16)
    if has_prev:
        prows = (pl.ds(start - dil * nq, nq, stride=dil) if dil > 1 else pl.ds(start - nq, nq))
        sp = lax.dot_general(k_ref[prows, :].astype(BF16), qblk, nt,
                             preferred_element_type=F32) * ATTN_SCALE
        s = jnp.concatenate([jnp.where(kj >= qi, sp, -jnp.inf), s], axis=0)
        vt = jnp.concatenate([v_ref[prows, :].T.astype(BF16), vt], axis=1)
    m = jnp.max(s, axis=0, keepdims=True)
    p = jnp.exp(s - m)
    l = jnp.sum(p, axis=0, keepdims=True)
    ot = jnp.dot(vt, p.astype(BF16), preferred_element_type=F32) * (1.0 / l)
    lam = jnp.broadcast_to(m + jnp.log(l), (hd, 2 * nq))
    o_scr[rows, :] = jnp.concatenate([ot[:hd, :nq], ot[hd:, nq:]], axis=0).T
    lam_scr[rows, :] = jnp.concatenate([lam[:, :nq], lam[:, nq:]], axis=0).T


def _attn_prompt_kernel(*refs, seq):
    qkv = refs[:9]
    y_ref = refs[9]
    kvt_refs = refs[10:13]
    o_scrs, lam_scrs = refs[13:16], refs[16:19]
    nq = ATTN_BLOCK
    for g, (window, dil) in enumerate(ATTN_PATTERNS):
        q_ref, k_ref, v_ref = qkv[3 * g:3 * g + 3]
        o_scr, lam_scr = o_scrs[g], lam_scrs[g]
        keep = min(window, seq)
        for c in range(keep // LANES):
            src = slice(seq - keep + c * LANES, seq - keep + (c + 1) * LANES)
            kvt_refs[g][0, :, c * LANES:(c + 1) * LANES] = k_ref[src, :].T
            kvt_refs[g][1, :, c * LANES:(c + 1) * LANES] = v_ref[src, :].T
        n_blk = seq // (dil * nq)
        for r in range(dil):
            for n in range(n_blk):
                _attn_block(q_ref, k_ref, v_ref, o_scr, lam_scr, r + dil * nq * n, dil, n > 0)

    rows_per = 256

    def merge(i, carry):
        rows = pl.ds(pl.multiple_of(i * rows_per, rows_per), rows_per)
        l0, l1, l2 = lam_scrs[0][rows, :], lam_scrs[1][rows, :], lam_scrs[2][rows, :]
        mx = jnp.maximum(jnp.maximum(l0, l1), l2)
        e0, e1, e2 = jnp.exp(l0 - mx), jnp.exp(l1 - mx), jnp.exp(l2 - mx)
        num = e0 * o_scrs[0][rows, :] + e1 * o_scrs[1][rows, :] + e2 * o_scrs[2][rows, :]
        y_ref[rows, :] = (num / (e0 + e1 + e2)).astype(BF16)
        return carry

    lax.fori_loop(0, seq // rows_per, merge, 0)


def attn_prompt(qkv, nb, seq):
    for window, dil in ATTN_PATTERNS:
        assert window // dil == ATTN_BLOCK and seq % (dil * ATTN_BLOCK) == 0
    n_pairs = ATTN_WIDTH // LANES
    keeps = [min(window, seq) for window, _ in ATTN_PATTERNS]
    in_specs = []
    for g in range(len(ATTN_PATTERNS)):
        for j in range(3):
            in_specs.append(pl.BlockSpec(
                (None, seq, LANES), lambda b, hp, g=g, j=j: (b, 0, (g * 3 + j) * n_pairs + hp)))
    outs = pl.pallas_call(
        functools.partial(_attn_prompt_kernel, seq=seq),
        grid=(nb, n_pairs),
        in_specs=in_specs,
        out_specs=[pl.BlockSpec((None, seq, LANES), lambda b, hp: (b, 0, hp))] + [
            pl.BlockSpec((None, 2, LANES, keep), lambda b, hp: (b, 0, hp, 0)) for keep in keeps],
        out_shape=[jax.ShapeDtypeStruct((nb, seq, ATTN_WIDTH), BF16)] + [
            jax.ShapeDtypeStruct((nb, 2, ATTN_WIDTH, keep), F32) for keep in keeps],
        scratch_shapes=[pltpu.VMEM((seq, LANES), F32)] * 6,
        compiler_params=_cparams(("parallel", "parallel")),
        name="attn_prompt")(*([qkv] * 9))
    kv = [jnp.transpose(kvt.reshape(nb, 2, ATTN_HEADS, ATTN_HEAD_DIM, keep), (0, 4, 1, 2, 3))
          for kvt, keep in zip(outs[1:], keeps)]
    return outs[0], kv


_DEC_ATTN_HEADS = 4


def _attn_decode_kernel(qkv_ref, c0_ref, c1_ref, c2_ref, y_ref, pad_ref):
    caches = (c0_ref, c1_ref, c2_ref)
    hg = pl.program_id(1)
    hd = ATTN_HEAD_DIM
    lane = lax.broadcasted_iota(I32, (hd, LANES), 1)

    def head_columns(a):
        pad_ref[...] = jnp.zeros_like(pad_ref)
        pad_ref[0:ATTN_HEADS, 0:hd] = a
        return pad_ref[...].T[0:hd, :]

    cols = [[head_columns(qkv_ref[g, j]) for j in range(3)] for g in range(len(ATTN_PATTERNS))]
    acc = jnp.zeros((hd, LANES), F32)
    for hh in range(_DEC_ATTN_HEADS):
        h = hg * _DEC_ATTN_HEADS + hh
        pick = lambda a: jnp.sum(jnp.where(lane == h, a, 0.0), axis=-1, keepdims=True)
        outs, lams = [], []
        for g, (window, dil) in enumerate(ATTN_PATTERNS):
            qc, knc, vnc = (pick(a) for a in cols[g])
            kt = caches[g][0, hh]
            vt = caches[g][1, hh]
            s = jnp.sum(kt * qc, axis=0, keepdims=True) * ATTN_SCALE
            if dil > 1:
                tok = lax.broadcasted_iota(I32, (1, window), 1)
                s = jnp.where(tok % dil == 0, s, -jnp.inf)
            s_new = jnp.sum(knc * qc, axis=0, keepdims=True) * ATTN_SCALE
            m = jnp.maximum(jnp.max(s, axis=-1, keepdims=True), s_new)
            p = jnp.exp(s - m)
            p_new = jnp.exp(s_new - m)
            l = jnp.sum(p, axis=-1, keepdims=True) + p_new
            o = (jnp.sum(vt * p, axis=-1, keepdims=True) + p_new * vnc) / l
            outs.append(o)
            lams.append(m + jnp.log(l))
        mx = jnp.maximum(jnp.maximum(lams[0], lams[1]), lams[2])
        es = [jnp.exp(lm - mx) for lm in lams]
        y = (es[0] * outs[0] + es[1] * outs[1] + es[2] * outs[2]) / (es[0] + es[1] + es[2])
        acc = jnp.where(lane == hh, y, acc)
    y_ref[...] = acc


def attn_decode(qkv_s, caches):
    nb = qkv_s.shape[0]
    hpb = _DEC_ATTN_HEADS
    in_specs = [pl.BlockSpec((None, 3, 3, ATTN_HEADS, ATTN_HEAD_DIM), lambda b, g: (b, 0, 0, 0, 0))]
    args = [qkv_s]
    for c, (window, dil) in zip(caches, ATTN_PATTERNS):
        assert c.shape[1] == window and window % dil == 0, "cache must hold a full window"
        args.append(jnp.transpose(c, (0, 2, 3, 4, 1)))
        in_specs.append(pl.BlockSpec((None, 2, hpb, ATTN_HEAD_DIM, window),
                                     lambda b, g: (b, 0, g, 0, 0)))
    y = pl.pallas_call(
        _attn_decode_kernel, grid=(nb, ATTN_HEADS // hpb),
        in_specs=in_specs,
        out_specs=pl.BlockSpec((None, None, ATTN_HEAD_DIM, LANES), lambda b, g: (b, g, 0, 0)),
        out_shape=jax.ShapeDtypeStruct((nb, ATTN_HEADS // hpb, ATTN_HEAD_DIM, LANES), F32),
        scratch_shapes=[pltpu.VMEM((LANES, LANES), F32)],
        compiler_params=_cparams(("parallel", "arbitrary")),
        name="attn_decode")(*args)
    return jnp.transpose(y[..., :hpb], (0, 1, 3, 2)).reshape(nb, ATTN_WIDTH)


_R_E1, _R_E2, _R_RANK1, _R_RANK2, _R_W1, _R_W2 = range(6)


def _router_kernel(x_ref, g_ref, wr_ref, info_ref, cnt_ref, *rest):
    xt_ref = rest[0] if len(rest) == 2 else None
    carry_ref = rest[-1]
    tm = x_ref.shape[0]

    @pl.when(pl.program_id(0) == 0)
    def _():
        carry_ref[...] = jnp.zeros_like(carry_ref)

    xn = _rms(x_ref[...], g_ref[...])
    if xt_ref is not None:
        _rows_to_tiles(xt_ref, xn)
    logits = _mm(xn, wr_ref[...])
    lane = lax.broadcasted_iota(I32, (tm, LANES), 1)
    lanef = lane.astype(F32)
    big = float(LANES)
    neg = -jnp.inf

    is_g = (lane >= GROUP_LOGIT_LANE) & (lane < GROUP_LOGIT_LANE + N_EXPERT_GROUPS)
    gl = jnp.where(is_g, logits, neg)
    gmax = jnp.max(gl, axis=-1, keepdims=True)
    grp = jnp.min(jnp.where(gl == gmax, lanef, big), axis=-1, keepdims=True) - GROUP_LOGIT_LANE
    g_gate = 1.0 / jnp.sum(jnp.exp(gl - gmax), axis=-1, keepdims=True)

    lo = grp * EXPERTS_PER_GROUP
    in_grp = (lanef >= lo) & (lanef < lo + EXPERTS_PER_GROUP)
    el = jnp.where(in_grp, logits, neg)
    m1 = jnp.max(el, axis=-1, keepdims=True)
    i1 = jnp.min(jnp.where(el == m1, lanef, big), axis=-1, keepdims=True)
    el2 = jnp.where(lanef == i1, neg, el)
    m2 = jnp.max(el2, axis=-1, keepdims=True)
    i2 = jnp.min(jnp.where(el2 == m2, lanef, big), axis=-1, keepdims=True)
    ratio = jnp.exp(m2 - m1)
    w1 = g_gate / (1.0 + ratio)
    w2 = g_gate * ratio / (1.0 + ratio)

    oh1 = lanef == i1
    oh2 = lanef == i2
    a = (oh1 | oh2).astype(BF16)
    ri = lax.broadcasted_iota(I32, (tm, tm), 0)
    ci = lax.broadcasted_iota(I32, (tm, tm), 1)
    before = (ci < ri).astype(BF16)
    pref = jnp.dot(before, a, preferred_element_type=F32) + carry_ref[...]
    rank1 = jnp.sum(jnp.where(oh1, pref, 0.0), axis=-1, keepdims=True)
    rank2 = jnp.sum(jnp.where(oh2, pref, 0.0), axis=-1, keepdims=True)
    carry_ref[...] = carry_ref[...] + jnp.sum(a.astype(F32), axis=0, keepdims=True)

    info = jnp.zeros((tm, LANES), F32)
    for ln, val in ((_R_E1, i1), (_R_E2, i2), (_R_RANK1, rank1), (_R_RANK2, rank2),
                    (_R_W1, w1), (_R_W2, w2)):
        info = jnp.where(lane == ln, val, info)
    info_ref[...] = info
    cnt_ref[...] = carry_ref[...]


def moe_router(x, g, wr, *, tm, emit_tiles=False):
    t, d = x.shape
    out_specs = [pl.BlockSpec((tm, LANES), lambda i: (i, 0)), pl.BlockSpec((1, LANES), lambda i: (0, 0))]
    out_shape = [jax.ShapeDtypeStruct((t, LANES), F32), jax.ShapeDtypeStruct((1, LANES), F32)]
    if emit_tiles:
        out_specs.append(pl.BlockSpec((tm * SUBLANES, LANES), lambda i: (i, 0)))
        out_shape.append(jax.ShapeDtypeStruct((t * SUBLANES, LANES), F32))
    return pl.pallas_call(
        _router_kernel, grid=(t // tm,),
        in_specs=[pl.BlockSpec((tm, d), lambda i: (i, 0)),
                  pl.BlockSpec((1, d), lambda i: (0, 0)),
                  pl.BlockSpec((d, LANES), lambda i: (0, 0))],
        out_specs=out_specs,
        out_shape=out_shape,
        scratch_shapes=[pltpu.VMEM((1, LANES), F32)],
        compiler_params=_cparams(("arbitrary",)),
        name="moe_router")(x, g, wr)


_ROW_UNROLL = 8
_ROW_TILES = D_MODEL // LANES
assert _ROW_TILES == SUBLANES


def _rows_to_tiles(ref, val):
    n = val.shape[0]
    for j in range(_ROW_TILES):
        ref[pl.ds(j, n, stride=_ROW_TILES), :] = val[:, j * LANES:(j + 1) * LANES]


def _tiles_to_rows(ref, n):
    return jnp.concatenate([ref[pl.ds(j, n, stride=_ROW_TILES), :] for j in range(_ROW_TILES)], axis=1)


def _tile_at(ref, row8):
    return ref.at[pl.ds(pl.multiple_of(row8, _ROW_TILES), _ROW_TILES)]


def _pos_blocks(pos, tm):
    n = pos.shape[0] // tm
    return (pos * _ROW_TILES).reshape(n, tm, 2).transpose(0, 2, 1).reshape(n, 1, 2 * tm)


def _row_copies(n_rows, make_copies):
    def body(i, carry):
        for u in range(_ROW_UNROLL):
            for cp in make_copies(i * _ROW_UNROLL + u):
                cp.start()
        return carry
    lax.fori_loop(0, n_rows // _ROW_UNROLL, body, 0)


def _dispatch_kernel(pos_ref, x_ref, g_ref, xs_in_ref, xs_ref, xn_ref, sem):
    del xs_in_ref
    tm = x_ref.shape[0]
    _rows_to_tiles(xn_ref, _rms(x_ref[...], g_ref[...]))

    def copies(r):
        src = _tile_at(xn_ref, r * _ROW_TILES)
        return [pltpu.make_async_copy(src, _tile_at(xs_ref, pos_ref[0, 0, k * tm + r]), sem)
                for k in range(2)]

    _row_copies(tm, copies)
    for _ in range(2):
        pltpu.make_async_copy(xn_ref, xs_ref.at[pl.ds(0, tm * _ROW_TILES)], sem).wait()


def moe_dispatch(x, g, pos, xs_init, *, tm):
    t, d = x.shape
    n = t // tm
    pos_blk = _pos_blocks(pos, tm)
    return pl.pallas_call(
        _dispatch_kernel, grid=(n,),
        in_specs=[pl.BlockSpec((1, 1, 2 * tm), lambda i: (i, 0, 0), memory_space=pltpu.SMEM),
                  pl.BlockSpec((tm, d), lambda i: (i, 0)),
                  pl.BlockSpec((1, d), lambda i: (0, 0)),
                  pl.BlockSpec(memory_space=pl.ANY)],
        out_specs=pl.BlockSpec(memory_space=pl.ANY),
        out_shape=jax.ShapeDtypeStruct(xs_init.shape, F32),
        scratch_shapes=[pltpu.VMEM((tm * _ROW_TILES, LANES), F32), pltpu.SemaphoreType.DMA(())],
        input_output_aliases={3: 0},
        compiler_params=_cparams(("arbitrary",)),
        name="moe_dispatch")(pos_blk, x, g, xs_init)


def _expert_kernel(te_ref, tv_ref, xs_ref, wgu_ref, wdn_ref, ys_ref, wgu_bf, wdn_bf):
    t = pl.program_id(0)
    tmx = xs_ref.shape[0] // _ROW_TILES
    changed = jnp.logical_or(t == 0, te_ref[t] != te_ref[jnp.maximum(t - 1, 0)])

    @pl.when(changed)
    def _():
        wgu_bf[...] = wgu_ref[...].astype(BF16)
        wdn_bf[...] = wdn_ref[...].astype(BF16)

    @pl.when(tv_ref[t] > 0)
    def _():
        x = _tiles_to_rows(xs_ref, tmx).astype(BF16)
        gu = jnp.dot(x, wgu_bf[...], preferred_element_type=F32)
        act = _silu(gu[:, :D_EXPERT]) * gu[:, D_EXPERT:]
        _rows_to_tiles(ys_ref, jnp.dot(act.astype(BF16), wdn_bf[...], preferred_element_type=F32))

    @pl.when(tv_ref[t] == 0)
    def _():
        ys_ref[...] = jnp.zeros_like(ys_ref)


def moe_experts(xs, w_gate_up, w_down, layer, tile_expert, tile_valid, *, tmx):
    d = w_gate_up.shape[2]
    blk = tmx * _ROW_TILES
    n_tiles = xs.shape[0] // blk
    grid_spec = pltpu.PrefetchScalarGridSpec(
        num_scalar_prefetch=2, grid=(n_tiles,),
        in_specs=[pl.BlockSpec((blk, LANES), lambda t, te, tv: (t, 0)),
                  pl.BlockSpec((None, None, d, 2 * D_EXPERT), lambda t, te, tv: (layer, te[t], 0, 0)),
                  pl.BlockSpec((None, None, D_EXPERT, d), lambda t, te, tv: (layer, te[t], 0, 0))],
        out_specs=pl.BlockSpec((blk, LANES), lambda t, te, tv: (t, 0)),
        scratch_shapes=[pltpu.VMEM((d, 2 * D_EXPERT), BF16), pltpu.VMEM((D_EXPERT, d), BF16)])
    return pl.pallas_call(
        _expert_kernel, grid_spec=grid_spec,
        out_shape=jax.ShapeDtypeStruct(xs.shape, F32),
        compiler_params=_cparams(("arbitrary",)),
        name="moe_experts")(tile_expert, tile_valid, xs, w_gate_up, w_down)


def _combine_kernel(pos_ref, x_ref, info_ref, g_ref, ys_ref, o_ref, buf0_ref, buf1_ref, sem, *,
                    final_norm):
    tm = x_ref.shape[0]
    bufs = (buf0_ref, buf1_ref)

    def copies(r):
        return [pltpu.make_async_copy(_tile_at(ys_ref, pos_ref[0, 0, k * tm + r]),
                                      _tile_at(bufs[k], r * _ROW_TILES), sem)
                for k in range(2)]

    _row_copies(tm, copies)
    for k in range(2):
        pltpu.make_async_copy(ys_ref.at[pl.ds(0, tm * _ROW_TILES)], bufs[k], sem).wait()
    info = info_ref[...]
    out = (x_ref[...] + info[:, _R_W1:_R_W1 + 1] * _tiles_to_rows(buf0_ref, tm)
           + info[:, _R_W2:_R_W2 + 1] * _tiles_to_rows(buf1_ref, tm))
    if final_norm:
        out = _rms(out, g_ref[...])
    o_ref[...] = out


def moe_combine(x, info, pos, ys, g_final, *, tm, final_norm):
    t, d = x.shape
    n = t // tm
    pos_blk = _pos_blocks(pos, tm)
    return pl.pallas_call(
        functools.partial(_combine_kernel, final_norm=final_norm), grid=(n,),
        in_specs=[pl.BlockSpec((1, 1, 2 * tm), lambda i: (i, 0, 0), memory_space=pltpu.SMEM),
                  pl.BlockSpec((tm, d), lambda i: (i, 0)),
                  pl.BlockSpec((tm, LANES), lambda i: (i, 0)),
                  pl.BlockSpec((1, d), lambda i: (0, 0)),
                  pl.BlockSpec(memory_space=pl.ANY)],
        out_specs=pl.BlockSpec((tm, d), lambda i: (i, 0)),
        out_shape=jax.ShapeDtypeStruct((t, d), F32),
        scratch_shapes=[pltpu.VMEM((tm * _ROW_TILES, LANES), F32)] * 2 + [pltpu.SemaphoreType.DMA(())],
        compiler_params=_cparams(("arbitrary",)),
        name="moe_combine")(pos_blk, x, info, g_final, ys)


_SC_CORES = 2
_SC_SUBCORES = 16
_SC_CHUNK = 64


def sc_gather_rows(table, idx):
    b = idx.shape[0]
    n_workers = _SC_CORES * _SC_SUBCORES
    per_worker = b // n_workers
    assert per_worker * n_workers == b and per_worker % _SC_CHUNK == 0
    mesh = plsc.VectorSubcoreMesh(core_axis_name="c", subcore_axis_name="s",
                                  num_cores=_SC_CORES, num_subcores=_SC_SUBCORES)

    @functools.partial(
        pl.kernel, mesh=mesh,
        out_type=jax.ShapeDtypeStruct((b,) + table.shape[1:], table.dtype),
        scratch_types=[pltpu.VMEM((_SC_CHUNK,), I32),
                       pltpu.VMEM((_SC_CHUNK,) + table.shape[1:], table.dtype),
                       pltpu.SemaphoreType.DMA],
        name="sc_gather_rows")
    def gather(table_hbm, idx_hbm, out_hbm, idx_v, rows_v, sem):
        worker = lax.axis_index("s") * _SC_CORES + lax.axis_index("c")
        base = worker * per_worker

        def chunk(i, carry):
            off = pl.multiple_of(base + i * _SC_CHUNK, _SC_CHUNK)
            pltpu.sync_copy(idx_hbm.at[pl.ds(off, _SC_CHUNK)], idx_v)
            pltpu.async_copy(table_hbm.at[idx_v], rows_v, sem).wait()
            pltpu.sync_copy(rows_v, out_hbm.at[pl.ds(off, _SC_CHUNK)])
            return carry

        lax.fori_loop(0, per_worker // _SC_CHUNK, chunk, 0)

    return gather(table, idx)


def sc_scatter_rows(rows, dst, n_tokens):
    b = dst.shape[0]
    n_workers = _SC_CORES * _SC_SUBCORES
    per_worker = b // n_workers
    assert per_worker * n_workers == b and per_worker % _SC_CHUNK == 0
    assert n_tokens % _SC_CHUNK == 0 and b - 2 * n_tokens <= n_tokens
    mesh = plsc.VectorSubcoreMesh(core_axis_name="c", subcore_axis_name="s",
                                  num_cores=_SC_CORES, num_subcores=_SC_SUBCORES)

    @functools.partial(
        pl.kernel, mesh=mesh,
        out_type=jax.ShapeDtypeStruct((b,) + rows.shape[1:], rows.dtype),
        scratch_types=[pltpu.VMEM((_SC_CHUNK,), I32),
                       pltpu.VMEM((_SC_CHUNK,) + rows.shape[1:], rows.dtype),
                       pltpu.SemaphoreType.DMA],
        name="sc_scatter_rows")
    def scatter(rows_hbm, dst_hbm, out_hbm, idx_v, rows_v, sem):
        worker = lax.axis_index("s") * _SC_CORES + lax.axis_index("c")
        base = worker * per_worker

        def chunk(i, carry):
            off = pl.multiple_of(base + i * _SC_CHUNK, _SC_CHUNK)
            src = jnp.where(off < 2 * n_tokens, lax.rem(off, n_tokens), off - 2 * n_tokens)
            src = pl.multiple_of(src, _SC_CHUNK)
            pltpu.sync_copy(dst_hbm.at[pl.ds(off, _SC_CHUNK)], idx_v)
            pltpu.sync_copy(rows_hbm.at[pl.ds(src, _SC_CHUNK)], rows_v)
            pltpu.async_copy(rows_v, out_hbm.at[idx_v], sem).wait()
            return carry

        lax.fori_loop(0, per_worker // _SC_CHUNK, chunk, 0)

    return scatter(rows, dst)


def _combine_dense_kernel(x_ref, info_ref, g_ref, y0_ref, y1_ref, o_ref, *, final_norm):
    tm = x_ref.shape[0]
    info = info_ref[...]
    out = (x_ref[...] + info[:, _R_W1:_R_W1 + 1] * _tiles_to_rows(y0_ref, tm)
           + info[:, _R_W2:_R_W2 + 1] * _tiles_to_rows(y1_ref, tm))
    if final_norm:
        out = _rms(out, g_ref[...])
    o_ref[...] = out


def moe_combine_dense(x, info, y_slots, g_final, *, tm, final_norm):
    t, d = x.shape
    n = t // tm
    blk = tm * _ROW_TILES
    return pl.pallas_call(
        functools.partial(_combine_dense_kernel, final_norm=final_norm), grid=(n,),
        in_specs=[pl.BlockSpec((tm, d), lambda i: (i, 0)),
                  pl.BlockSpec((tm, LANES), lambda i: (i, 0)),
                  pl.BlockSpec((1, d), lambda i: (0, 0)),
                  pl.BlockSpec((blk, LANES), lambda i: (i, 0)),
                  pl.BlockSpec((blk, LANES), lambda i: (n + i, 0))],
        out_specs=pl.BlockSpec((tm, d), lambda i: (i, 0)),
        out_shape=jax.ShapeDtypeStruct((t, d), F32),
        compiler_params=_cparams(("parallel",)),
        name="moe_combine_dense")(x, info, g_final, y_slots, y_slots)


def _router_weight(w_group, w_expert, dtype):
    d = w_group.shape[0]
    we = jnp.transpose(w_expert, (1, 0, 2)).reshape(d, N_EXPERTS)
    w = jnp.concatenate([we, w_group], axis=1)
    return jnp.pad(w, ((0, 0), (0, LANES - w.shape[1]))).astype(dtype)


def hier_moe_layer(x, norm_g, w_group, w_expert, w_gate_up, w_down, layer, g_final, *, tm, tmx,
                   final_norm, route_dtype):
    t, d = x.shape
    g = norm_g.reshape(1, d)
    n_rows = 2 * t + N_EXPERTS * tmx
    sc_rows = _SC_CORES * _SC_SUBCORES * _SC_CHUNK
    use_sc = (2 * t) % sc_rows == 0 and n_rows % sc_rows == 0
    routed = moe_router(x, g, _router_weight(w_group, w_expert, route_dtype), tm=tm, emit_tiles=use_sc)
    info, counts = routed[0], routed[1]

    counts = counts[0, :N_EXPERTS].astype(I32)
    padded = ((counts + tmx - 1) // tmx) * tmx
    ends = jnp.cumsum(padded)
    offsets = ends - padded
    e12 = info[:, _R_E1:_R_E2 + 1].astype(I32)
    rank12 = info[:, _R_RANK1:_R_RANK2 + 1].astype(I32)
    pos = offsets[e12] + rank12
    n_tiles = n_rows // tmx
    tile_start = jnp.arange(n_tiles, dtype=I32) * tmx
    tile_valid = (tile_start < ends[-1]).astype(I32)
    probe = jnp.minimum(tile_start, jnp.maximum(ends[-1] - 1, 0))
    tile_expert = jnp.sum((probe[:, None] >= ends[None, :]).astype(I32), axis=1)
    tile_expert = jnp.minimum(tile_expert, N_EXPERTS - 1)

    depth = w_gate_up.shape[0]
    slot_major = pos.T.reshape(2 * t)
    if use_sc:
        j = jnp.arange(tmx, dtype=I32)[None, :]
        is_pad = (j < (padded - counts)[:, None]).reshape(-1)
        pad_pos = ((offsets + counts)[:, None] + j).reshape(-1)
        spare = jnp.cumsum((~is_pad).astype(I32)) - 1
        dst = jnp.concatenate([slot_major, jnp.where(is_pad, pad_pos, ends[-1] + spare)])
        xs = sc_scatter_rows(routed[2].reshape(t, _ROW_TILES, LANES), dst, t)
        xs = xs.reshape(n_rows * _ROW_TILES, LANES)
    else:
        xs = moe_dispatch(x, g, pos, jnp.zeros((n_rows * _ROW_TILES, LANES), F32), tm=tm)
    ys = moe_experts(xs, w_gate_up.reshape(depth, N_EXPERTS, d, 2 * D_EXPERT),
                     w_down.reshape(depth, N_EXPERTS, D_EXPERT, d), layer,
                     tile_expert, tile_valid, tmx=tmx)
    if use_sc:
        rows = sc_gather_rows(ys.reshape(n_rows, _ROW_TILES, LANES), slot_major)
        return moe_combine_dense(x, info, rows.reshape(2 * t * _ROW_TILES, LANES),
                                 g_final.reshape(1, d), tm=tm, final_norm=final_norm)
    return moe_combine(x, info, pos, ys, g_final.reshape(1, d), tm=tm, final_norm=final_norm)


def kernel(x_prompt, x_sample, state_conv, state_ssm, cache_kv_w128, cache_kv_w512, cache_kv_w2048, norm_mix, norm_ffn, norm_final, ssm_in_proj, ssm_conv_w, ssm_conv_b, ssm_dt_bias, ssm_a_log, ssm_d, ssm_norm, ssm_out_proj, attn_qkv, attn_out, moe_route_group, moe_route_expert, moe_w_gate_up, moe_w_down):
    nb, seq, d = x_prompt.shape
    ns = x_sample.shape[0]
    t = nb * seq
    tm = min(t, 1024)
    n_groups = len(ATTN_PATTERNS)
    kv_caches = (cache_kv_w128, cache_kv_w512, cache_kv_w2048)

    ssm_params = (ssm_in_proj[0], ssm_conv_w[0], ssm_conv_b[0], ssm_dt_bias[0], ssm_a_log[0],
                  ssm_d[0], ssm_norm[0], ssm_out_proj[0])
    xp, conv_p, ssm_p = _mamba_prompt(x_prompt, norm_mix[0], *ssm_params)
    xs, conv_s, ssm_s = _mamba_sample(x_sample, norm_mix[0], state_conv[0], state_ssm[0], *ssm_params)

    def moe(x2d, i, tm_moe, tmx, final_norm, route_dtype):
        return hier_moe_layer(x2d, norm_ffn[i], moe_route_group[i], moe_route_expert[i],
                              moe_w_gate_up, moe_w_down, i, norm_final,
                              tm=tm_moe, tmx=tmx, final_norm=final_norm, route_dtype=route_dtype)

    xp = moe(xp.reshape(t, d), 0, 256, 256, False, BF16)
    xs = moe(xs.reshape(ns, d), 0, ns, 16, False, F32)

    g1 = norm_mix[1].reshape(1, d)
    w_qkv = attn_qkv[0].astype(BF16)
    w_o = attn_out[0].astype(BF16)
    qkv_p = norm_matmul(xp, g1, w_qkv, tm=tm, tn=1024)
    y_p, kv_p = attn_prompt(qkv_p.reshape(nb, seq, -1), nb, seq)
    xp = matmul_res(y_p.reshape(t, ATTN_WIDTH), w_o, xp, tm=tm)

    qkv_s = norm_matmul(xs, g1, attn_qkv[0], tm=ns, tn=512)
    qkv_s5 = qkv_s.reshape(ns, n_groups, 3, ATTN_HEADS, ATTN_HEAD_DIM)
    y_s = attn_decode(qkv_s5, [c[0] for c in kv_caches])
    xs = matmul_res(y_s.reshape(ns, ATTN_WIDTH), attn_out[0], xs, tm=ns)
    kv_s = [qkv_s5[:, g, 1:3][:, None] for g in range(n_groups)]

    y_prompt = moe(xp, 1, 256, 256, True, BF16).reshape(nb, seq, d)
    y_sample = moe(xs, 1, ns, 16, True, F32).reshape(ns, 1, d)

    return (y_prompt, y_sample, conv_p[None], conv_s[None], ssm_p[None], ssm_s[None],
            kv_p[0][None], kv_s[0][None], kv_p[1][None], kv_s[1][None], kv_p[2][None], kv_s[2][None])
```

```python
import functools
import math

import jax
import jax.numpy as jnp
from jax import lax
from jax.experimental import pallas as pl
from jax.experimental.pallas import tpu as pltpu
from jax.experimental.pallas import tpu_sc as plsc

F32 = jnp.float32
BF16 = jnp.bfloat16
I32 = jnp.int32

EPS = 1e-6
D_MODEL = 1024
LANES = 128
SUBLANES = 8

SSM_D_INNER = 2048
SSM_HEAD_DIM = 64
SSM_HEADS = 32
SSM_GROUPS = 4
SSM_HEADS_PER_GROUP = 8
SSM_D_STATE = 128
CONV_WIDTH = 4
SSD_CHUNK = 128
SSM_BC_DIM = 2 * SSM_GROUPS * SSM_D_STATE
SSM_CONV_DIM = SSM_D_INNER + SSM_BC_DIM

ATTN_PATTERNS = ((128, 1), (512, 4), (2048, 16))
ATTN_HEAD_DIM = 64
ATTN_HEADS = 16
ATTN_WIDTH = 1024
ATTN_BLOCK = 128
ATTN_SCALE = ATTN_HEAD_DIM ** -0.5

N_EXPERT_GROUPS = 4
EXPERTS_PER_GROUP = 8
N_EXPERTS = N_EXPERT_GROUPS * EXPERTS_PER_GROUP
D_EXPERT = 512
GROUP_LOGIT_LANE = N_EXPERTS

VMEM_LIMIT = 56 * 1024 * 1024


def _cparams(sem):
    return pltpu.CompilerParams(dimension_semantics=sem, vmem_limit_bytes=VMEM_LIMIT)


def _rms(x, g):
    ms = jnp.mean(x * x, axis=-1, keepdims=True)
    return x * lax.rsqrt(ms + EPS) * g


def _silu(x):
    return x * jax.nn.sigmoid(x)


def _softplus(x):
    return jnp.maximum(x, 0.0) + jnp.log1p(jnp.exp(-jnp.abs(x)))


def _mm(a, b):
    prec = lax.Precision.HIGHEST if b.dtype == F32 else None
    return jnp.dot(a.astype(b.dtype), b, preferred_element_type=F32, precision=prec)


def _norm_matmul_kernel(x_ref, g_ref, w_ref, o_ref, xn_ref):
    @pl.when(pl.program_id(1) == 0)
    def _():
        xn_ref[...] = _rms(x_ref[...], g_ref[...]).astype(xn_ref.dtype)

    o_ref[...] = _mm(xn_ref[...], w_ref[...])


def _norm_matmul2_kernel(x_ref, g_ref, w_ref, w2_ref, o_ref, o2_ref, xn_ref):
    @pl.when(pl.program_id(1) == 0)
    def _():
        xn = _rms(x_ref[...], g_ref[...]).astype(xn_ref.dtype)
        xn_ref[...] = xn
        o2_ref[...] = _mm(xn, w2_ref[...])

    o_ref[...] = _mm(xn_ref[...], w_ref[...])


def norm_matmul(x, g, w, w2=None, *, tm, tn):
    t, d = x.shape
    n = w.shape[1]
    grid = (t // tm, n // tn)
    x_spec = pl.BlockSpec((tm, d), lambda i, j: (i, 0))
    g_spec = pl.BlockSpec((1, d), lambda i, j: (0, 0))
    w_spec = pl.BlockSpec((d, tn), lambda i, j: (0, j))
    o_spec = pl.BlockSpec((tm, tn), lambda i, j: (i, j))
    scratch = [pltpu.VMEM((tm, d), w.dtype)]
    if w2 is None:
        return pl.pallas_call(
            _norm_matmul_kernel, grid=grid,
            in_specs=[x_spec, g_spec, w_spec], out_specs=o_spec,
            out_shape=jax.ShapeDtypeStruct((t, n), F32),
            scratch_shapes=scratch,
            compiler_params=_cparams(("parallel", "arbitrary")),
            name="norm_matmul")(x, g, w)
    n2 = w2.shape[1]
    return pl.pallas_call(
        _norm_matmul2_kernel, grid=grid,
        in_specs=[x_spec, g_spec, w_spec, pl.BlockSpec((d, n2), lambda i, j: (0, 0))],
        out_specs=[o_spec, pl.BlockSpec((tm, n2), lambda i, j: (i, 0))],
        out_shape=[jax.ShapeDtypeStruct((t, n), F32), jax.ShapeDtypeStruct((t, n2), F32)],
        scratch_shapes=scratch,
        compiler_params=_cparams(("parallel", "arbitrary")),
        name="norm_matmul2")(x, g, w, w2)


def _matmul_res_kernel(a_ref, w_ref, r_ref, o_ref):
    o_ref[...] = r_ref[...] + _mm(a_ref[...], w_ref[...])


def matmul_res(a, w, res, *, tm):
    t, k = a.shape
    n = w.shape[1]
    return pl.pallas_call(
        _matmul_res_kernel, grid=(t // tm,),
        in_specs=[pl.BlockSpec((tm, k), lambda i: (i, 0)),
                  pl.BlockSpec((k, n), lambda i: (0, 0)),
                  pl.BlockSpec((tm, n), lambda i: (i, 0))],
        out_specs=pl.BlockSpec((tm, n), lambda i: (i, 0)),
        out_shape=jax.ShapeDtypeStruct((t, n), F32),
        compiler_params=_cparams(("parallel",)),
        name="matmul_res")(a, w, res)


_CONV_COLS = 512


def _ssd_kernel(z_ref, xs_ref, bc_ref, dt_ref, cw_ref, cb_ref, dtb_ref, alog_ref,
                dsk_ref, ng_ref,
                y_ref, tail_out_ref, st_ref,
                tail_ref, h_ref, xc_ref, xst_ref, yt_ref, dtt_ref, cst_ref, xde_ref, yn_ref):
    q = SSD_CHUNK
    c = pl.program_id(1)

    @pl.when(c == 0)
    def _():
        tail_ref[...] = jnp.zeros_like(tail_ref)
        h_ref[...] = jnp.zeros_like(h_ref)

    row = lax.broadcasted_iota(I32, (q, _CONV_COLS), 0)
    for k in range(SSM_CONV_DIM // _CONV_COLS):
        lo = k * _CONV_COLS
        if lo < SSM_D_INNER:
            src = xs_ref[0, :, lo:lo + _CONV_COLS]
        else:
            src = bc_ref[0, :, lo - SSM_D_INNER:lo - SSM_D_INNER + _CONV_COLS]
        tl = tail_ref[:, lo:lo + _CONV_COLS]
        w = cw_ref[:, lo:lo + _CONV_COLS]
        acc = src * w[CONV_WIDTH - 1:CONV_WIDTH] + cb_ref[:, lo:lo + _CONV_COLS]
        for s in range(1, CONV_WIDTH):
            cur = pltpu.roll(src, s, axis=0)
            prev = jnp.tile(pltpu.roll(tl, s, axis=0), (q // SUBLANES, 1))
            shifted = jnp.where(row < s, prev, cur)
            acc = acc + shifted * w[CONV_WIDTH - 1 - s:CONV_WIDTH - s]
        xc_ref[:, lo:lo + _CONV_COLS] = _silu(acc)
        tail_ref[:, lo:lo + _CONV_COLS] = src[q - SUBLANES:, :]

    for k in range(SSM_D_INNER // LANES):
        xst_ref[k * LANES:(k + 1) * LANES, :] = xc_ref[:, k * LANES:(k + 1) * LANES].T

    dt = _softplus(dt_ref[0] + dtb_ref[...])
    da = dt * (-jnp.exp(alog_ref[...]))
    ri = lax.broadcasted_iota(I32, (q, q), 0)
    ci = lax.broadcasted_iota(I32, (q, q), 1)
    tril = (ci <= ri).astype(F32)
    cs = jnp.dot(tril, da, preferred_element_type=F32, precision=lax.Precision.HIGHEST)
    dtt_ref[...] = dt.T
    cst_ref[...] = cs.T

    causal_t = ci >= ri
    hpg = SSM_HEADS_PER_GROUP
    gp = hpg * SSM_HEAD_DIM
    for g in range(SSM_GROUPS):
        b0 = SSM_D_INNER + g * SSM_D_STATE
        c0 = SSM_D_INNER + SSM_GROUPS * SSM_D_STATE + g * SSM_D_STATE
        bm = xc_ref[:, b0:b0 + SSM_D_STATE].astype(BF16)
        cm = xc_ref[:, c0:c0 + SSM_D_STATE].astype(BF16)
        cbt = lax.dot_general(bm, cm, (((1,), (1,)), ((), ())), preferred_element_type=F32)
        hg = h_ref[g * gp:(g + 1) * gp, :].astype(BF16)
        yt_ref[g * gp:(g + 1) * gp, :] = lax.dot_general(
            hg, cm, (((1,), (1,)), ((), ())), preferred_element_type=F32)

        for e in range(hpg):
            hd = g * hpg + e
            rows = slice(hd * SSM_HEAD_DIM, (hd + 1) * SSM_HEAD_DIM)
            csr = cst_ref[hd:hd + 1, :]
            rowb = jnp.broadcast_to(csr, (q, q))
            colb = rowb.T
            dec = jnp.exp(jnp.where(causal_t, rowb - colb, -jnp.inf))
            mt = (cbt * dec).astype(BF16)
            dtr = dtt_ref[hd:hd + 1, :]
            xsh = xst_ref[rows, :]
            xdt = xsh * dtr
            ydt = jnp.dot(xdt.astype(BF16), mt, preferred_element_type=F32)
            cs_end = csr[:, q - 1:q]
            yt_ref[rows, :] = yt_ref[rows, :] * jnp.exp(csr) + ydt + xsh * dsk_ref[rows, :]
            xde_ref[e * SSM_HEAD_DIM:(e + 1) * SSM_HEAD_DIM, :] = (
                xdt * jnp.exp(cs_end - csr)).astype(BF16)
            h_ref[rows, :] = h_ref[rows, :] * jnp.exp(cs_end)
        h_ref[g * gp:(g + 1) * gp, :] = h_ref[g * gp:(g + 1) * gp, :] + jnp.dot(
            xde_ref[...], bm, preferred_element_type=F32)

    ssq = jnp.zeros((q, 1), F32)
    for k in range(SSM_D_INNER // LANES):
        yk = yt_ref[k * LANES:(k + 1) * LANES, :].T * _silu(z_ref[0, :, k * LANES:(k + 1) * LANES])
        yn_ref[:, k * LANES:(k + 1) * LANES] = yk
        ssq = ssq + jnp.sum(yk * yk, axis=-1, keepdims=True)
    scale = lax.rsqrt(ssq * (1.0 / SSM_D_INNER) + EPS)
    y_ref[0] = (yn_ref[...] * scale * ng_ref[...]).astype(BF16)

    @pl.when(c == pl.num_programs(1) - 1)
    def _():
        tail_out_ref[0] = tail_ref[...]
        st_ref[0] = h_ref[...]


def ssd_prompt(proj, dt_raw, conv_w, conv_b, dt_bias, a_log, d_rows, norm_g, nb, seq):
    nc = seq // SSD_CHUNK
    q = SSD_CHUNK
    const2 = lambda b, c: (0, 0)
    return pl.pallas_call(
        _ssd_kernel, grid=(nb, nc),
        in_specs=[
            pl.BlockSpec((1, q, SSM_D_INNER), lambda b, c: (b, c, 0)),
            pl.BlockSpec((1, q, SSM_D_INNER), lambda b, c: (b, c, 1)),
            pl.BlockSpec((1, q, SSM_BC_DIM), lambda b, c: (b, c, 2 * SSM_D_INNER // SSM_BC_DIM)),
            pl.BlockSpec((1, q, LANES), lambda b, c: (b, c, 0)),
            pl.BlockSpec((CONV_WIDTH, SSM_CONV_DIM), const2),
            pl.BlockSpec((1, SSM_CONV_DIM), const2),
            pl.BlockSpec((1, LANES), const2),
            pl.BlockSpec((1, LANES), const2),
            pl.BlockSpec((SSM_D_INNER, LANES), const2),
            pl.BlockSpec((1, SSM_D_INNER), const2),
        ],
        out_specs=[
            pl.BlockSpec((1, q, SSM_D_INNER), lambda b, c: (b, c, 0)),
            pl.BlockSpec((1, SUBLANES, SSM_CONV_DIM), lambda b, c: (b, 0, 0)),
            pl.BlockSpec((1, SSM_D_INNER, SSM_D_STATE), lambda b, c: (b, 0, 0)),
        ],
        out_shape=[
            jax.ShapeDtypeStruct((nb, seq, SSM_D_INNER), BF16),
            jax.ShapeDtypeStruct((nb, SUBLANES, SSM_CONV_DIM), F32),
            jax.ShapeDtypeStruct((nb, SSM_D_INNER, SSM_D_STATE), F32),
        ],
        scratch_shapes=[
            pltpu.VMEM((SUBLANES, SSM_CONV_DIM), F32),
            pltpu.VMEM((SSM_D_INNER, SSM_D_STATE), F32),
            pltpu.VMEM((q, SSM_CONV_DIM), F32),
            pltpu.VMEM((SSM_D_INNER, q), F32),
            pltpu.VMEM((SSM_D_INNER, q), F32),
            pltpu.VMEM((LANES, q), F32),
            pltpu.VMEM((LANES, q), F32),
            pltpu.VMEM((SSM_HEADS_PER_GROUP * SSM_HEAD_DIM, q), BF16),
            pltpu.VMEM((q, SSM_D_INNER), F32),
        ],
        compiler_params=_cparams(("parallel", "arbitrary")),
        name="ssd_prompt")(proj, proj, proj, dt_raw, conv_w, conv_b, dt_bias, a_log, d_rows, norm_g)


def _pad_lanes(v, n=LANES):
    return jnp.pad(v.astype(F32), (0, n - v.shape[0])).reshape(1, n)


def _ssm_weights(w_in, dtype):
    n_main = SSM_D_INNER + SSM_CONV_DIM
    w_main = w_in[:, :n_main].astype(dtype)
    w_dt = jnp.pad(w_in[:, n_main:], ((0, 0), (0, LANES - SSM_HEADS))).astype(dtype)
    return w_main, w_dt


def _mamba_prompt(x, norm_g, w_in, conv_w, conv_b, dt_bias, a_log, d_skip, ssm_norm, w_out):
    nb, seq, d = x.shape
    t = nb * seq
    w_main, w_dt = _ssm_weights(w_in, BF16)
    tm = min(t, 1024)
    proj, dt_raw = norm_matmul(x.reshape(t, d), norm_g.reshape(1, d), w_main, w_dt, tm=tm, tn=1024)
    d_rows = jnp.broadcast_to(jnp.repeat(d_skip.astype(F32), SSM_HEAD_DIM)[:, None], (SSM_D_INNER, LANES))
    y, tail, st = ssd_prompt(
        proj.reshape(nb, seq, -1), dt_raw.reshape(nb, seq, LANES),
        conv_w, conv_b.reshape(1, -1), _pad_lanes(dt_bias), _pad_lanes(a_log), d_rows,
        ssm_norm.reshape(1, -1), nb, seq)
    x_new = matmul_res(y.reshape(t, SSM_D_INNER), w_out.astype(BF16), x.reshape(t, d), tm=tm)
    conv_state = tail[:, SUBLANES - (CONV_WIDTH - 1):]
    ssm_state = st.reshape(nb, SSM_HEADS, SSM_HEAD_DIM, SSM_D_STATE)
    return x_new.reshape(nb, seq, d), conv_state, ssm_state


_DEC_HEADS = 4


def _ssd_decode_kernel(zt_ref, xbct_ref, convt_ref, dtt_raw_ref, cwt_ref, cbt_ref, dtbt_ref,
                       alogt_ref, dsk_ref, ngt_ref, wout_ref, res_ref, st_ref,
                       so_ref, out_ref,
                       xct_ref, dtt_ref, dect_ref, bcn_ref, yt_ref, *, nb):
    step = pl.program_id(0)
    rb = 512

    @pl.when(step == 0)
    def _():
        for k in range(SSM_CONV_DIM // rb):
            rows = slice(k * rb, (k + 1) * rb)
            acc = xbct_ref[rows, :] * cwt_ref[rows, CONV_WIDTH - 1:CONV_WIDTH] + cbt_ref[rows, 0:1]
            for j in range(CONV_WIDTH - 1):
                acc = acc + convt_ref[j, rows, :] * cwt_ref[rows, j:j + 1]
            xct_ref[rows, :] = _silu(acc)
        dt = _softplus(dtt_raw_ref[...] + dtbt_ref[:, 0:1])
        dtt_ref[...] = dt
        dect_ref[...] = jnp.exp(dt * (-jnp.exp(alogt_ref[:, 0:1])))
        for k in range(SSM_BC_DIM // LANES):
            bcn_ref[:, k * LANES:(k + 1) * LANES] = xct_ref[
                SSM_D_INNER + k * LANES:SSM_D_INNER + (k + 1) * LANES, :].T
        yt_ref[...] = jnp.zeros_like(yt_ref)

    for hh in range(_DEC_HEADS):
        hd = step * _DEC_HEADS + hh
        r0 = pl.multiple_of(hd * SSM_HEAD_DIM, SSM_HEAD_DIM)
        grp = hd // SSM_HEADS_PER_GROUP
        b_lane = pl.multiple_of(grp * SSM_D_STATE, SSM_D_STATE)
        c_lane = pl.multiple_of(SSM_GROUPS * SSM_D_STATE + grp * SSM_D_STATE, SSM_D_STATE)
        xh = xct_ref[pl.ds(r0, SSM_HEAD_DIM), :]
        dth = dtt_ref[pl.ds(hd, 1), :]
        dech = dect_ref[pl.ds(hd, 1), :]
        xdt = xh * dth
        for b in range(nb):
            brow = bcn_ref[b:b + 1, pl.ds(b_lane, SSM_D_STATE)]
            crow = bcn_ref[b:b + 1, pl.ds(c_lane, SSM_D_STATE)]
            hn = st_ref[b, hh] * dech[:, b:b + 1] + xdt[:, b:b + 1] * brow
            so_ref[b, hh] = hn
            yt_ref[pl.ds(r0, SSM_HEAD_DIM), b:b + 1] = jnp.sum(hn * crow, axis=-1, keepdims=True)

    @pl.when(step == pl.num_programs(0) - 1)
    def _():
        ssq = jnp.zeros((1, LANES), F32)
        for k in range(SSM_D_INNER // rb):
            rows = slice(k * rb, (k + 1) * rb)
            y = (yt_ref[rows, :] + xct_ref[rows, :] * dsk_ref[rows, :]) * _silu(zt_ref[rows, :])
            yt_ref[rows, :] = y
            ssq = ssq + jnp.sum(y * y, axis=0, keepdims=True)
        scale = lax.rsqrt(ssq * (1.0 / SSM_D_INNER) + EPS)
        acc = res_ref[...]
        for k in range(SSM_D_INNER // LANES):
            rows = slice(k * LANES, (k + 1) * LANES)
            yn = (yt_ref[rows, :] * scale * ngt_ref[rows, :]).T
            acc = acc + _mm(yn, wout_ref[rows, :])
        out_ref[...] = acc


def ssd_decode(zt, xbct, convt, dtt_raw, cwt, cbt, dtbt, alogt, d_rows, ngt, w_out, res, state):
    nb = state.shape[0]
    n_steps = SSM_HEADS // _DEC_HEADS
    full = lambda a: pl.BlockSpec(a.shape, lambda s, n=a.ndim: (0,) * n)
    st_spec = pl.BlockSpec((nb, _DEC_HEADS, SSM_HEAD_DIM, SSM_D_STATE), lambda s: (0, s, 0, 0))
    small = (zt, xbct, convt, dtt_raw, cwt, cbt, dtbt, alogt, d_rows, ngt, w_out, res)
    return pl.pallas_call(
        functools.partial(_ssd_decode_kernel, nb=nb), grid=(n_steps,),
        in_specs=[full(a) for a in small] + [st_spec],
        out_specs=[st_spec, full(res)],
        out_shape=[jax.ShapeDtypeStruct(state.shape, F32), jax.ShapeDtypeStruct(res.shape, F32)],
        scratch_shapes=[
            pltpu.VMEM((SSM_CONV_DIM, LANES), F32),
            pltpu.VMEM((LANES, LANES), F32),
            pltpu.VMEM((LANES, LANES), F32),
            pltpu.VMEM((LANES, SSM_BC_DIM), F32),
            pltpu.VMEM((SSM_D_INNER, LANES), F32),
        ],
        compiler_params=_cparams(("arbitrary",)),
        name="ssd_decode")(*small, state)


def _to_lanes(a, n=LANES):
    return jnp.pad(a.T, ((0, 0), (0, n - a.shape[0])))


def _mamba_sample(x, norm_g, conv_state, ssm_state, w_in, conv_w, conv_b, dt_bias, a_log, d_skip,
                  ssm_norm, w_out):
    nb, _, d = x.shape
    assert nb <= LANES and nb % SUBLANES == 0
    w_main, w_dt = _ssm_weights(w_in, F32)
    proj, dt_raw = norm_matmul(x.reshape(nb, d), norm_g.reshape(1, d), w_main, w_dt, tm=nb, tn=512)
    xbc = proj[:, SSM_D_INNER:]
    col = lambda v: jnp.broadcast_to(v.astype(F32)[:, None], (v.shape[0], LANES))
    convt = jnp.pad(jnp.transpose(conv_state, (1, 2, 0)), ((0, 0), (0, 0), (0, LANES - nb)))
    res = jnp.pad(x.reshape(nb, d), ((0, LANES - nb), (0, 0)))
    st, out = ssd_decode(
        _to_lanes(proj[:, :SSM_D_INNER]), _to_lanes(xbc), convt, _to_lanes(dt_raw),
        jnp.pad(conv_w.T, ((0, 0), (0, LANES - CONV_WIDTH))), col(conv_b),
        col(_pad_lanes(dt_bias)[0]), col(_pad_lanes(a_log)[0]),
        col(jnp.repeat(d_skip, SSM_HEAD_DIM)), col(ssm_norm), w_out, res, ssm_state)
    new_conv = jnp.concatenate([conv_state[:, 1:], xbc[:, None, :]], axis=1)
    return out[:nb].reshape(nb, 1, d), new_conv, st


def _attn_block(q_ref, k_ref, v_ref, o_scr, lam_scr, start, dil, has_prev):
    nq = ATTN_BLOCK
    rows = pl.ds(start, nq, stride=dil) if dil > 1 else pl.ds(start, nq)
    hd = ATTN_HEAD_DIM
    q2 = q_ref[rows, :]
    lane = lax.broadcasted_iota(I32, (nq, LANES), 1)
    qblk = jnp.concatenate([jnp.where(lane < hd, q2, 0.0), jnp.where(lane >= hd, q2, 0.0)],
                           axis=0).astype(BF16)
    kj = lax.broadcasted_iota(I32, (nq, 2 * nq), 0)
    qi = lax.broadcasted_iota(I32, (nq, 2 * nq), 1) % nq
    nt = (((1,), (1,)), ((), ()))
    s = lax.dot_general(k_ref[rows, :].astype(BF16), qblk, nt,
                        preferred_element_type=F32) * ATTN_SCALE
    s = jnp.where(kj <= qi, s, -jnp.inf)
    vt = v_ref[rows, :].T.astype(BF16)
    if has_prev:
        prows = (pl.ds(start - dil * nq, nq, stride=dil) if dil > 1 else pl.ds(start - nq, nq))
        sp = lax.dot_general(k_ref[prows, :].astype(BF16), qblk, nt,
                             preferred_element_type=F32) * ATTN_SCALE
        s = jnp.concatenate([jnp.where(kj >= qi, sp, -jnp.inf), s], axis=0)
        vt = jnp.concatenate([v_ref[prows, :].T.astype(BF16), vt], axis=1)
    m = jnp.max(s, axis=0, keepdims=True)
    p = jnp.exp(s - m)
    l = jnp.sum(p, axis=0, keepdims=True)
    ot = jnp.dot(vt, p.astype(BF16), preferred_element_type=F32) * (1.0 / l)
    lam = jnp.broadcast_to(m + jnp.log(l), (hd, 2 * nq))
    o_scr[rows, :] = jnp.concatenate([ot[:hd, :nq], ot[hd:, nq:]], axis=0).T
    lam_scr[rows, :] = jnp.concatenate([lam[:, :nq], lam[:, nq:]], axis=0).T


def _attn_prompt_kernel(*refs, seq):
    qkv = refs[:9]
    y_ref = refs[9]
    kvt_refs = refs[10:13]
    o_scrs, lam_scrs = refs[13:16], refs[16:19]
    nq = ATTN_BLOCK
    for g, (window, dil) in enumerate(ATTN_PATTERNS):
        q_ref, k_ref, v_ref = qkv[3 * g:3 * g + 3]
        o_scr, lam_scr = o_scrs[g], lam_scrs[g]
        keep = min(window, seq)
        for c in range(keep // LANES):
            src = slice(seq - keep + c * LANES, seq - keep + (c + 1) * LANES)
            kvt_refs[g][0, :, c * LANES:(c + 1) * LANES] = k_ref[src, :].T
            kvt_refs[g][1, :, c * LANES:(c + 1) * LANES] = v_ref[src, :].T
        n_blk = seq // (dil * nq)
        for r in range(dil):
            for n in range(n_blk):
                _attn_block(q_ref, k_ref, v_ref, o_scr, lam_scr, r + dil * nq * n, dil, n > 0)

    rows_per = 256

    def merge(i, carry):
        rows = pl.ds(pl.multiple_of(i * rows_per, rows_per), rows_per)
        l0, l1, l2 = lam_scrs[0][rows, :], lam_scrs[1][rows, :], lam_scrs[2][rows, :]
        mx = jnp.maximum(jnp.maximum(l0, l1), l2)
        e0, e1, e2 = jnp.exp(l0 - mx), jnp.exp(l1 - mx), jnp.exp(l2 - mx)
        num = e0 * o_scrs[0][rows, :] + e1 * o_scrs[1][rows, :] + e2 * o_scrs[2][rows, :]
        y_ref[rows, :] = (num / (e0 + e1 + e2)).astype(BF16)
        return carry

    lax.fori_loop(0, seq // rows_per, merge, 0)


def attn_prompt(qkv, nb, seq):
    for window, dil in ATTN_PATTERNS:
        assert window // dil == ATTN_BLOCK and seq % (dil * ATTN_BLOCK) == 0
    n_pairs = ATTN_WIDTH // LANES
    keeps = [min(window, seq) for window, _ in ATTN_PATTERNS]
    in_specs = []
    for g in range(len(ATTN_PATTERNS)):
        for j in range(3):
            in_specs.append(pl.BlockSpec(
                (None, seq, LANES), lambda b, hp, g=g, j=j: (b, 0, (g * 3 + j) * n_pairs + hp)))
    outs = pl.pallas_call(
        functools.partial(_attn_prompt_kernel, seq=seq),
        grid=(nb, n_pairs),
        in_specs=in_specs,
        out_specs=[pl.BlockSpec((None, seq, LANES), lambda b, hp: (b, 0, hp))] + [
            pl.BlockSpec((None, 2, LANES, keep), lambda b, hp: (b, 0, hp, 0)) for keep in keeps],
        out_shape=[jax.ShapeDtypeStruct((nb, seq, ATTN_WIDTH), BF16)] + [
            jax.ShapeDtypeStruct((nb, 2, ATTN_WIDTH, keep), F32) for keep in keeps],
        scratch_shapes=[pltpu.VMEM((seq, LANES), F32)] * 6,
        compiler_params=_cparams(("parallel", "parallel")),
        name="attn_prompt")(*([qkv] * 9))
    kv = [jnp.transpose(kvt.reshape(nb, 2, ATTN_HEADS, ATTN_HEAD_DIM, keep), (0, 4, 1, 2, 3))
          for kvt, keep in zip(outs[1:], keeps)]
    return outs[0], kv


_DEC_ATTN_HEADS = 4


def _attn_decode_kernel(qkv_ref, c0_ref, c1_ref, c2_ref, y_ref, pad_ref):
    caches = (c0_ref, c1_ref, c2_ref)
    hg = pl.program_id(1)
    hd = ATTN_HEAD_DIM
    lane = lax.broadcasted_iota(I32, (hd, LANES), 1)

    def head_columns(a):
        pad_ref[...] = jnp.zeros_like(pad_ref)
        pad_ref[0:ATTN_HEADS, 0:hd] = a
        return pad_ref[...].T[0:hd, :]

    cols = [[head_columns(qkv_ref[g, j]) for j in range(3)] for g in range(len(ATTN_PATTERNS))]
    acc = jnp.zeros((hd, LANES), F32)
    for hh in range(_DEC_ATTN_HEADS):
        h = hg * _DEC_ATTN_HEADS + hh
        pick = lambda a: jnp.sum(jnp.where(lane == h, a, 0.0), axis=-1, keepdims=True)
        outs, lams = [], []
        for g, (window, dil) in enumerate(ATTN_PATTERNS):
            qc, knc, vnc = (pick(a) for a in cols[g])
            kt = caches[g][0, hh]
            vt = caches[g][1, hh]
            s = jnp.sum(kt * qc, axis=0, keepdims=True) * ATTN_SCALE
            if dil > 1:
                tok = lax.broadcasted_iota(I32, (1, window), 1)
                s = jnp.where(tok % dil == 0, s, -jnp.inf)
            s_new = jnp.sum(knc * qc, axis=0, keepdims=True) * ATTN_SCALE
            m = jnp.maximum(jnp.max(s, axis=-1, keepdims=True), s_new)
            p = jnp.exp(s - m)
            p_new = jnp.exp(s_new - m)
            l = jnp.sum(p, axis=-1, keepdims=True) + p_new
            o = (jnp.sum(vt * p, axis=-1, keepdims=True) + p_new * vnc) / l
            outs.append(o)
            lams.append(m + jnp.log(l))
        mx = jnp.maximum(jnp.maximum(lams[0], lams[1]), lams[2])
        es = [jnp.exp(lm - mx) for lm in lams]
        y = (es[0] * outs[0] + es[1] * outs[1] + es[2] * outs[2]) / (es[0] + es[1] + es[2])
        acc = jnp.where(lane == hh, y, acc)
    y_ref[...] = acc


def attn_decode(qkv_s, caches):
    nb = qkv_s.shape[0]
    hpb = _DEC_ATTN_HEADS
    in_specs = [pl.BlockSpec((None, 3, 3, ATTN_HEADS, ATTN_HEAD_DIM), lambda b, g: (b, 0, 0, 0, 0))]
    args = [qkv_s]
    for c, (window, dil) in zip(caches, ATTN_PATTERNS):
        assert c.shape[1] == window and window % dil == 0, "cache must hold a full window"
        args.append(jnp.transpose(c, (0, 2, 3, 4, 1)))
        in_specs.append(pl.BlockSpec((None, 2, hpb, ATTN_HEAD_DIM, window),
                                     lambda b, g: (b, 0, g, 0, 0)))
    y = pl.pallas_call(
        _attn_decode_kernel, grid=(nb, ATTN_HEADS // hpb),
        in_specs=in_specs,
        out_specs=pl.BlockSpec((None, None, ATTN_HEAD_DIM, LANES), lambda b, g: (b, g, 0, 0)),
        out_shape=jax.ShapeDtypeStruct((nb, ATTN_HEADS // hpb, ATTN_HEAD_DIM, LANES), F32),
        scratch_shapes=[pltpu.VMEM((LANES, LANES), F32)],
        compiler_params=_cparams(("parallel", "arbitrary")),
        name="attn_decode")(*args)
    return jnp.transpose(y[..., :hpb], (0, 1, 3, 2)).reshape(nb, ATTN_WIDTH)


_R_E1, _R_E2, _R_RANK1, _R_RANK2, _R_W1, _R_W2 = range(6)


def _router_kernel(x_ref, g_ref, wr_ref, info_ref, cnt_ref, idx_ref, *rest):
    xt_ref = rest[0] if len(rest) == 2 else None
    carry_ref = rest[-1]
    tm = x_ref.shape[0]

    @pl.when(pl.program_id(0) == 0)
    def _():
        carry_ref[...] = jnp.zeros_like(carry_ref)

    xn = _rms(x_ref[...], g_ref[...])
    if xt_ref is not None:
        _rows_to_tiles(xt_ref, xn)
    logits = _mm(xn, wr_ref[...])
    lane = lax.broadcasted_iota(I32, (tm, LANES), 1)
    lanef = lane.astype(F32)
    big = float(LANES)
    neg = -jnp.inf

    is_g = (lane >= GROUP_LOGIT_LANE) & (lane < GROUP_LOGIT_LANE + N_EXPERT_GROUPS)
    gl = jnp.where(is_g, logits, neg)
    gmax = jnp.max(gl, axis=-1, keepdims=True)
    grp = jnp.min(jnp.where(gl == gmax, lanef, big), axis=-1, keepdims=True) - GROUP_LOGIT_LANE
    g_gate = 1.0 / jnp.sum(jnp.exp(gl - gmax), axis=-1, keepdims=True)

    lo = grp * EXPERTS_PER_GROUP
    in_grp = (lanef >= lo) & (lanef < lo + EXPERTS_PER_GROUP)
    el = jnp.where(in_grp, logits, neg)
    m1 = jnp.max(el, axis=-1, keepdims=True)
    i1 = jnp.min(jnp.where(el == m1, lanef, big), axis=-1, keepdims=True)
    el2 = jnp.where(lanef == i1, neg, el)
    m2 = jnp.max(el2, axis=-1, keepdims=True)
    i2 = jnp.min(jnp.where(el2 == m2, lanef, big), axis=-1, keepdims=True)
    ratio = jnp.exp(m2 - m1)
    w1 = g_gate / (1.0 + ratio)
    w2 = g_gate * ratio / (1.0 + ratio)

    oh1 = lanef == i1
    oh2 = lanef == i2
    a = (oh1 | oh2).astype(BF16)
    ri = lax.broadcasted_iota(I32, (tm, tm), 0)
    ci = lax.broadcasted_iota(I32, (tm, tm), 1)
    before = (ci < ri).astype(BF16)
    pref = jnp.dot(before, a, preferred_element_type=F32) + carry_ref[...]
    rank1 = jnp.sum(jnp.where(oh1, pref, 0.0), axis=-1, keepdims=True)
    rank2 = jnp.sum(jnp.where(oh2, pref, 0.0), axis=-1, keepdims=True)
    carry_ref[...] = carry_ref[...] + jnp.sum(a.astype(F32), axis=0, keepdims=True)

    info = jnp.zeros((tm, LANES), F32)
    for ln, val in ((_R_E1, i1), (_R_E2, i2), (_R_RANK1, rank1), (_R_RANK2, rank2),
                    (_R_W1, w1), (_R_W2, w2)):
        info = jnp.where(lane == ln, val, info)
    info_ref[...] = info
    idx_ref[...] = info.T[0:SUBLANES, :]
    cnt_ref[...] = carry_ref[...]


def moe_router(x, g, wr, *, tm, emit_tiles=False):
    t, d = x.shape
    out_specs = [pl.BlockSpec((tm, LANES), lambda i: (i, 0)), pl.BlockSpec((1, LANES), lambda i: (0, 0)),
                 pl.BlockSpec((SUBLANES, tm), lambda i: (0, i))]
    out_shape = [jax.ShapeDtypeStruct((t, LANES), F32), jax.ShapeDtypeStruct((1, LANES), F32),
                 jax.ShapeDtypeStruct((SUBLANES, t), F32)]
    if emit_tiles:
        out_specs.append(pl.BlockSpec((tm * SUBLANES, LANES), lambda i: (i, 0)))
        out_shape.append(jax.ShapeDtypeStruct((t * SUBLANES, LANES), F32))
    return pl.pallas_call(
        _router_kernel, grid=(t // tm,),
        in_specs=[pl.BlockSpec((tm, d), lambda i: (i, 0)),
                  pl.BlockSpec((1, d), lambda i: (0, 0)),
                  pl.BlockSpec((d, LANES), lambda i: (0, 0))],
        out_specs=out_specs,
        out_shape=out_shape,
        scratch_shapes=[pltpu.VMEM((1, LANES), F32)],
        compiler_params=_cparams(("arbitrary",)),
        name="moe_router")(x, g, wr)


_ROW_UNROLL = 8
_ROW_TILES = D_MODEL // LANES
assert _ROW_TILES == SUBLANES


def _rows_to_tiles(ref, val):
    n = val.shape[0]
    for j in range(_ROW_TILES):
        ref[pl.ds(j, n, stride=_ROW_TILES), :] = val[:, j * LANES:(j + 1) * LANES]


def _tiles_to_rows(ref, n):
    return jnp.concatenate([ref[pl.ds(j, n, stride=_ROW_TILES), :] for j in range(_ROW_TILES)], axis=1)


def _tile_at(ref, row8):
    return ref.at[pl.ds(pl.multiple_of(row8, _ROW_TILES), _ROW_TILES)]


def _pos_blocks(pos, tm):
    n = pos.shape[0] // tm
    return (pos * _ROW_TILES).reshape(n, tm, 2).transpose(0, 2, 1).reshape(n, 1, 2 * tm)


def _row_copies(n_rows, make_copies):
    def body(i, carry):
        for u in range(_ROW_UNROLL):
            for cp in make_copies(i * _ROW_UNROLL + u):
                cp.start()
        return carry
    lax.fori_loop(0, n_rows // _ROW_UNROLL, body, 0)


def _dispatch_kernel(pos_ref, x_ref, g_ref, xs_in_ref, xs_ref, xn_ref, sem):
    del xs_in_ref
    tm = x_ref.shape[0]
    _rows_to_tiles(xn_ref, _rms(x_ref[...], g_ref[...]))

    def copies(r):
        src = _tile_at(xn_ref, r * _ROW_TILES)
        return [pltpu.make_async_copy(src, _tile_at(xs_ref, pos_ref[0, 0, k * tm + r]), sem)
                for k in range(2)]

    _row_copies(tm, copies)
    for _ in range(2):
        pltpu.make_async_copy(xn_ref, xs_ref.at[pl.ds(0, tm * _ROW_TILES)], sem).wait()


def moe_dispatch(x, g, pos, xs_init, *, tm):
    t, d = x.shape
    n = t // tm
    pos_blk = _pos_blocks(pos, tm)
    return pl.pallas_call(
        _dispatch_kernel, grid=(n,),
        in_specs=[pl.BlockSpec((1, 1, 2 * tm), lambda i: (i, 0, 0), memory_space=pltpu.SMEM),
                  pl.BlockSpec((tm, d), lambda i: (i, 0)),
                  pl.BlockSpec((1, d), lambda i: (0, 0)),
                  pl.BlockSpec(memory_space=pl.ANY)],
        out_specs=pl.BlockSpec(memory_space=pl.ANY),
        out_shape=jax.ShapeDtypeStruct(xs_init.shape, F32),
        scratch_shapes=[pltpu.VMEM((tm * _ROW_TILES, LANES), F32), pltpu.SemaphoreType.DMA(())],
        input_output_aliases={3: 0},
        compiler_params=_cparams(("arbitrary",)),
        name="moe_dispatch")(pos_blk, x, g, xs_init)


def _expert_kernel(te_ref, tv_ref, xs_ref, wgu_ref, wdn_ref, ys_ref, wgu_bf, wdn_bf):
    t = pl.program_id(0)
    tmx = xs_ref.shape[0] // _ROW_TILES
    changed = jnp.logical_or(t == 0, te_ref[t] != te_ref[jnp.maximum(t - 1, 0)])

    @pl.when(changed)
    def _():
        wgu_bf[...] = wgu_ref[...].astype(BF16)
        wdn_bf[...] = wdn_ref[...].astype(BF16)

    @pl.when(tv_ref[t] > 0)
    def _():
        x = _tiles_to_rows(xs_ref, tmx).astype(BF16)
        gu = jnp.dot(x, wgu_bf[...], preferred_element_type=F32)
        act = _silu(gu[:, :D_EXPERT]) * gu[:, D_EXPERT:]
        _rows_to_tiles(ys_ref, jnp.dot(act.astype(BF16), wdn_bf[...], preferred_element_type=F32))

    @pl.when(tv_ref[t] == 0)
    def _():
        ys_ref[...] = jnp.zeros_like(ys_ref)


def moe_experts(xs, w_gate_up, w_down, layer, tile_expert, tile_valid, *, tmx):
    d = w_gate_up.shape[2]
    blk = tmx * _ROW_TILES
    n_tiles = xs.shape[0] // blk
    grid_spec = pltpu.PrefetchScalarGridSpec(
        num_scalar_prefetch=2, grid=(n_tiles,),
        in_specs=[pl.BlockSpec((blk, LANES), lambda t, te, tv: (t, 0)),
                  pl.BlockSpec((None, None, d, 2 * D_EXPERT), lambda t, te, tv: (layer, te[t], 0, 0)),
                  pl.BlockSpec((None, None, D_EXPERT, d), lambda t, te, tv: (layer, te[t], 0, 0))],
        out_specs=pl.BlockSpec((blk, LANES), lambda t, te, tv: (t, 0)),
        scratch_shapes=[pltpu.VMEM((d, 2 * D_EXPERT), BF16), pltpu.VMEM((D_EXPERT, d), BF16)])
    return pl.pallas_call(
        _expert_kernel, grid_spec=grid_spec,
        out_shape=jax.ShapeDtypeStruct(xs.shape, F32),
        compiler_params=_cparams(("arbitrary",)),
        name="moe_experts")(tile_expert, tile_valid, xs, w_gate_up, w_down)


def _combine_kernel(pos_ref, x_ref, info_ref, g_ref, ys_ref, o_ref, buf0_ref, buf1_ref, sem, *,
                    final_norm):
    tm = x_ref.shape[0]
    bufs = (buf0_ref, buf1_ref)

    def copies(r):
        return [pltpu.make_async_copy(_tile_at(ys_ref, pos_ref[0, 0, k * tm + r]),
                                      _tile_at(bufs[k], r * _ROW_TILES), sem)
                for k in range(2)]

    _row_copies(tm, copies)
    for k in range(2):
        pltpu.make_async_copy(ys_ref.at[pl.ds(0, tm * _ROW_TILES)], bufs[k], sem).wait()
    info = info_ref[...]
    out = (x_ref[...] + info[:, _R_W1:_R_W1 + 1] * _tiles_to_rows(buf0_ref, tm)
           + info[:, _R_W2:_R_W2 + 1] * _tiles_to_rows(buf1_ref, tm))
    if final_norm:
        out = _rms(out, g_ref[...])
    o_ref[...] = out


def moe_combine(x, info, pos, ys, g_final, *, tm, final_norm):
    t, d = x.shape
    n = t // tm
    pos_blk = _pos_blocks(pos, tm)
    return pl.pallas_call(
        functools.partial(_combine_kernel, final_norm=final_norm), grid=(n,),
        in_specs=[pl.BlockSpec((1, 1, 2 * tm), lambda i: (i, 0, 0), memory_space=pltpu.SMEM),
                  pl.BlockSpec((tm, d), lambda i: (i, 0)),
                  pl.BlockSpec((tm, LANES), lambda i: (i, 0)),
                  pl.BlockSpec((1, d), lambda i: (0, 0)),
                  pl.BlockSpec(memory_space=pl.ANY)],
        out_specs=pl.BlockSpec((tm, d), lambda i: (i, 0)),
        out_shape=jax.ShapeDtypeStruct((t, d), F32),
        scratch_shapes=[pltpu.VMEM((tm * _ROW_TILES, LANES), F32)] * 2 + [pltpu.SemaphoreType.DMA(())],
        compiler_params=_cparams(("arbitrary",)),
        name="moe_combine")(pos_blk, x, info, g_final, ys)


_SC_CORES = 2
_SC_SUBCORES = 16
_SC_CHUNK = 64


def sc_gather_rows(table, idx):
    b = idx.shape[0]
    n_workers = _SC_CORES * _SC_SUBCORES
    per_worker = b // n_workers
    assert per_worker * n_workers == b and per_worker % _SC_CHUNK == 0
    mesh = plsc.VectorSubcoreMesh(core_axis_name="c", subcore_axis_name="s",
                                  num_cores=_SC_CORES, num_subcores=_SC_SUBCORES)

    @functools.partial(
        pl.kernel, mesh=mesh,
        out_type=jax.ShapeDtypeStruct((b,) + table.shape[1:], table.dtype),
        scratch_types=[pltpu.VMEM((_SC_CHUNK,), I32),
                       pltpu.VMEM((_SC_CHUNK,) + table.shape[1:], table.dtype),
                       pltpu.SemaphoreType.DMA],
        name="sc_gather_rows")
    def gather(table_hbm, idx_hbm, out_hbm, idx_v, rows_v, sem):
        worker = lax.axis_index("s") * _SC_CORES + lax.axis_index("c")
        base = worker * per_worker

        def chunk(i, carry):
            off = pl.multiple_of(base + i * _SC_CHUNK, _SC_CHUNK)
            pltpu.sync_copy(idx_hbm.at[pl.ds(off, _SC_CHUNK)], idx_v)
            pltpu.async_copy(table_hbm.at[idx_v], rows_v, sem).wait()
            pltpu.sync_copy(rows_v, out_hbm.at[pl.ds(off, _SC_CHUNK)])
            return carry

        lax.fori_loop(0, per_worker // _SC_CHUNK, chunk, 0)

    return gather(table, idx)


def sc_scatter_rows(rows, dst, n_tokens):
    b = dst.shape[0]
    n_workers = _SC_CORES * _SC_SUBCORES
    per_worker = b // n_workers
    assert per_worker * n_workers == b and per_worker % _SC_CHUNK == 0
    assert n_tokens % _SC_CHUNK == 0 and b - 2 * n_tokens <= n_tokens
    mesh = plsc.VectorSubcoreMesh(core_axis_name="c", subcore_axis_name="s",
                                  num_cores=_SC_CORES, num_subcores=_SC_SUBCORES)

    @functools.partial(
        pl.kernel, mesh=mesh,
        out_type=jax.ShapeDtypeStruct((b,) + rows.shape[1:], rows.dtype),
        scratch_types=[pltpu.VMEM((_SC_CHUNK,), I32),
                       pltpu.VMEM((_SC_CHUNK,) + rows.shape[1:], rows.dtype),
                       pltpu.SemaphoreType.DMA],
        name="sc_scatter_rows")
    def scatter(rows_hbm, dst_hbm, out_hbm, idx_v, rows_v, sem):
        worker = lax.axis_index("s") * _SC_CORES + lax.axis_index("c")
        base = worker * per_worker

        def chunk(i, carry):
            off = pl.multiple_of(base + i * _SC_CHUNK, _SC_CHUNK)
            src = jnp.where(off < 2 * n_tokens, lax.rem(off, n_tokens), off - 2 * n_tokens)
            src = pl.multiple_of(src, _SC_CHUNK)
            pltpu.sync_copy(dst_hbm.at[pl.ds(off, _SC_CHUNK)], idx_v)
            pltpu.sync_copy(rows_hbm.at[pl.ds(src, _SC_CHUNK)], rows_v)
            pltpu.async_copy(rows_v, out_hbm.at[idx_v], sem).wait()
            return carry

        lax.fori_loop(0, per_worker // _SC_CHUNK, chunk, 0)

    return scatter(rows, dst)


def _combine_dense_kernel(x_ref, info_ref, g_ref, y0_ref, y1_ref, o_ref, *, final_norm):
    tm = x_ref.shape[0]
    info = info_ref[...]
    out = (x_ref[...] + info[:, _R_W1:_R_W1 + 1] * _tiles_to_rows(y0_ref, tm)
           + info[:, _R_W2:_R_W2 + 1] * _tiles_to_rows(y1_ref, tm))
    if final_norm:
        out = _rms(out, g_ref[...])
    o_ref[...] = out


def moe_combine_dense(x, info, y_slots, g_final, *, tm, final_norm):
    t, d = x.shape
    n = t // tm
    blk = tm * _ROW_TILES
    return pl.pallas_call(
        functools.partial(_combine_dense_kernel, final_norm=final_norm), grid=(n,),
        in_specs=[pl.BlockSpec((tm, d), lambda i: (i, 0)),
                  pl.BlockSpec((tm, LANES), lambda i: (i, 0)),
                  pl.BlockSpec((1, d), lambda i: (0, 0)),
                  pl.BlockSpec((blk, LANES), lambda i: (i, 0)),
                  pl.BlockSpec((blk, LANES), lambda i: (n + i, 0))],
        out_specs=pl.BlockSpec((tm, d), lambda i: (i, 0)),
        out_shape=jax.ShapeDtypeStruct((t, d), F32),
        compiler_params=_cparams(("parallel",)),
        name="moe_combine_dense")(x, info, g_final, y_slots, y_slots)


def _router_weight(w_group, w_expert, dtype):
    d = w_group.shape[0]
    we = jnp.transpose(w_expert, (1, 0, 2)).reshape(d, N_EXPERTS)
    w = jnp.concatenate([we, w_group], axis=1)
    return jnp.pad(w, ((0, 0), (0, LANES - w.shape[1]))).astype(dtype)


def hier_moe_layer(x, norm_g, w_group, w_expert, w_gate_up, w_down, layer, g_final, *, tm, tmx,
                   final_norm, route_dtype):
    t, d = x.shape
    g = norm_g.reshape(1, d)
    n_rows = 2 * t + N_EXPERTS * tmx
    sc_rows = _SC_CORES * _SC_SUBCORES * _SC_CHUNK
    use_sc = (2 * t) % sc_rows == 0 and n_rows % sc_rows == 0
    routed = moe_router(x, g, _router_weight(w_group, w_expert, route_dtype), tm=tm, emit_tiles=use_sc)
    info, counts = routed[0], routed[1]

    counts = counts[0, :N_EXPERTS].astype(I32)
    padded = ((counts + tmx - 1) // tmx) * tmx
    ends = jnp.cumsum(padded)
    offsets = ends - padded
    e12 = routed[2][_R_E1:_R_E2 + 1].astype(I32)
    rank12 = routed[2][_R_RANK1:_R_RANK2 + 1].astype(I32)
    slot_major = (offsets[e12] + rank12).reshape(2 * t)
    n_tiles = n_rows // tmx
    tile_start = jnp.arange(n_tiles, dtype=I32) * tmx
    tile_valid = (tile_start < ends[-1]).astype(I32)
    probe = jnp.minimum(tile_start, jnp.maximum(ends[-1] - 1, 0))
    tile_expert = jnp.sum((probe[:, None] >= ends[None, :]).astype(I32), axis=1)
    tile_expert = jnp.minimum(tile_expert, N_EXPERTS - 1)

    depth = w_gate_up.shape[0]
    pos = slot_major.reshape(2, t).T
    if use_sc:
        j = jnp.arange(tmx, dtype=I32)[None, :]
        is_pad = (j < (padded - counts)[:, None]).reshape(-1)
        pad_pos = ((offsets + counts)[:, None] + j).reshape(-1)
        spare = jnp.cumsum((~is_pad).astype(I32)) - 1
        dst = jnp.concatenate([slot_major, jnp.where(is_pad, pad_pos, ends[-1] + spare)])
        xs = sc_scatter_rows(routed[3].reshape(t, _ROW_TILES, LANES), dst, t)
        xs = xs.reshape(n_rows * _ROW_TILES, LANES)
    else:
        xs = moe_dispatch(x, g, pos, jnp.zeros((n_rows * _ROW_TILES, LANES), F32), tm=tm)
    ys = moe_experts(xs, w_gate_up.reshape(depth, N_EXPERTS, d, 2 * D_EXPERT),
                     w_down.reshape(depth, N_EXPERTS, D_EXPERT, d), layer,
                     tile_expert, tile_valid, tmx=tmx)
    if use_sc:
        rows = sc_gather_rows(ys.reshape(n_rows, _ROW_TILES, LANES), slot_major)
        return moe_combine_dense(x, info, rows.reshape(2 * t * _ROW_TILES, LANES),
                                 g_final.reshape(1, d), tm=tm, final_norm=final_norm)
    return moe_combine(x, info, pos, ys, g_final.reshape(1, d), tm=tm, final_norm=final_norm)


def kernel(x_prompt, x_sample, state_conv, state_ssm, cache_kv_w128, cache_kv_w512, cache_kv_w2048, norm_mix, norm_ffn, norm_final, ssm_in_proj, ssm_conv_w, ssm_conv_b, ssm_dt_bias, ssm_a_log, ssm_d, ssm_norm, ssm_out_proj, attn_qkv, attn_out, moe_route_group, moe_route_expert, moe_w_gate_up, moe_w_down):
    nb, seq, d = x_prompt.shape
    ns = x_sample.shape[0]
    t = nb * seq
    tm = min(t, 1024)
    n_groups = len(ATTN_PATTERNS)
    kv_caches = (cache_kv_w128, cache_kv_w512, cache_kv_w2048)

    ssm_params = (ssm_in_proj[0], ssm_conv_w[0], ssm_conv_b[0], ssm_dt_bias[0], ssm_a_log[0],
                  ssm_d[0], ssm_norm[0], ssm_out_proj[0])
    xp, conv_p, ssm_p = _mamba_prompt(x_prompt, norm_mix[0], *ssm_params)
    xs, conv_s, ssm_s = _mamba_sample(x_sample, norm_mix[0], state_conv[0], state_ssm[0], *ssm_params)

    def moe(x2d, i, tm_moe, tmx, final_norm, route_dtype):
        return hier_moe_layer(x2d, norm_ffn[i], moe_route_group[i], moe_route_expert[i],
                              moe_w_gate_up, moe_w_down, i, norm_final,
                              tm=tm_moe, tmx=tmx, final_norm=final_norm, route_dtype=route_dtype)

    xp = moe(xp.reshape(t, d), 0, 256, 256, False, BF16)
    xs = moe(xs.reshape(ns, d), 0, ns, 16, False, F32)

    g1 = norm_mix[1].reshape(1, d)
    w_qkv = attn_qkv[0].astype(BF16)
    w_o = attn_out[0].astype(BF16)
    qkv_p = norm_matmul(xp, g1, w_qkv, tm=tm, tn=1024)
    y_p, kv_p = attn_prompt(qkv_p.reshape(nb, seq, -1), nb, seq)
    xp = matmul_res(y_p.reshape(t, ATTN_WIDTH), w_o, xp, tm=tm)

    qkv_s = norm_matmul(xs, g1, attn_qkv[0], tm=ns, tn=512)
    qkv_s5 = qkv_s.reshape(ns, n_groups, 3, ATTN_HEADS, ATTN_HEAD_DIM)
    y_s = attn_decode(qkv_s5, [c[0] for c in kv_caches])
    xs = matmul_res(y_s.reshape(ns, ATTN_WIDTH), attn_out[0], xs, tm=ns)
    kv_s = [qkv_s5[:, g, 1:3][:, None] for g in range(n_groups)]

    y_prompt = moe(xp, 1, 256, 256, True, BF16).reshape(nb, seq, d)
    y_sample = moe(xs, 1, ns, 16, True, F32).reshape(ns, 1, d)

    return (y_prompt, y_sample, conv_p[None], conv_s[None], ssm_p[None], ssm_s[None],
            kv_p[0][None], kv_s[0][None], kv_p[1][None], kv_s[1][None], kv_p[2][None], kv_s[2][None])
```

```python
import functools
import math

import jax
import jax.numpy as jnp
from jax import lax
from jax.experimental import pallas as pl
from jax.experimental.pallas import tpu as pltpu
from jax.experimental.pallas import tpu_sc as plsc

F32 = jnp.float32
BF16 = jnp.bfloat16
I32 = jnp.int32

EPS = 1e-6
D_MODEL = 1024
LANES = 128
SUBLANES = 8

SSM_D_INNER = 2048
SSM_HEAD_DIM = 64
SSM_HEADS = 32
SSM_GROUPS = 4
SSM_HEADS_PER_GROUP = 8
SSM_D_STATE = 128
CONV_WIDTH = 4
SSD_CHUNK = 128
SSM_BC_DIM = 2 * SSM_GROUPS * SSM_D_STATE
SSM_CONV_DIM = SSM_D_INNER + SSM_BC_DIM

ATTN_PATTERNS = ((128, 1), (512, 4), (2048, 16))
ATTN_HEAD_DIM = 64
ATTN_HEADS = 16
ATTN_WIDTH = 1024
ATTN_BLOCK = 128
ATTN_SCALE = ATTN_HEAD_DIM ** -0.5

N_EXPERT_GROUPS = 4
EXPERTS_PER_GROUP = 8
N_EXPERTS = N_EXPERT_GROUPS * EXPERTS_PER_GROUP
D_EXPERT = 512
GROUP_LOGIT_LANE = N_EXPERTS

VMEM_LIMIT = 56 * 1024 * 1024


def _cparams(sem):
    return pltpu.CompilerParams(dimension_semantics=sem, vmem_limit_bytes=VMEM_LIMIT)


def _rms(x, g):
    ms = jnp.mean(x * x, axis=-1, keepdims=True)
    return x * lax.rsqrt(ms + EPS) * g


def _silu(x):
    return x * jax.nn.sigmoid(x)


def _softplus(x):
    return jnp.maximum(x, 0.0) + jnp.log1p(jnp.exp(-jnp.abs(x)))


def _mm(a, b):
    prec = lax.Precision.HIGHEST if b.dtype == F32 else None
    return jnp.dot(a.astype(b.dtype), b, preferred_element_type=F32, precision=prec)


def _norm_matmul_kernel(x_ref, g_ref, w_ref, o_ref, xn_ref):
    @pl.when(pl.program_id(1) == 0)
    def _():
        xn_ref[...] = _rms(x_ref[...], g_ref[...]).astype(xn_ref.dtype)

    o_ref[...] = _mm(xn_ref[...], w_ref[...])


def _norm_matmul2_kernel(x_ref, g_ref, w_ref, w2_ref, o_ref, o2_ref, xn_ref):
    @pl.when(pl.program_id(1) == 0)
    def _():
        xn = _rms(x_ref[...], g_ref[...]).astype(xn_ref.dtype)
        xn_ref[...] = xn
        o2_ref[...] = _mm(xn, w2_ref[...])

    o_ref[...] = _mm(xn_ref[...], w_ref[...])


def norm_matmul(x, g, w, w2=None, *, tm, tn):
    t, d = x.shape
    n = w.shape[1]
    grid = (t // tm, n // tn)
    x_spec = pl.BlockSpec((tm, d), lambda i, j: (i, 0))
    g_spec = pl.BlockSpec((1, d), lambda i, j: (0, 0))
    w_spec = pl.BlockSpec((d, tn), lambda i, j: (0, j))
    o_spec = pl.BlockSpec((tm, tn), lambda i, j: (i, j))
    scratch = [pltpu.VMEM((tm, d), w.dtype)]
    if w2 is None:
        return pl.pallas_call(
            _norm_matmul_kernel, grid=grid,
            in_specs=[x_spec, g_spec, w_spec], out_specs=o_spec,
            out_shape=jax.ShapeDtypeStruct((t, n), F32),
            scratch_shapes=scratch,
            compiler_params=_cparams(("parallel", "arbitrary")),
            name="norm_matmul")(x, g, w)
    n2 = w2.shape[1]
    return pl.pallas_call(
        _norm_matmul2_kernel, grid=grid,
        in_specs=[x_spec, g_spec, w_spec, pl.BlockSpec((d, n2), lambda i, j: (0, 0))],
        out_specs=[o_spec, pl.BlockSpec((tm, n2), lambda i, j: (i, 0))],
        out_shape=[jax.ShapeDtypeStruct((t, n), F32), jax.ShapeDtypeStruct((t, n2), F32)],
        scratch_shapes=scratch,
        compiler_params=_cparams(("parallel", "arbitrary")),
        name="norm_matmul2")(x, g, w, w2)


def _matmul_res_kernel(a_ref, w_ref, r_ref, o_ref):
    o_ref[...] = r_ref[...] + _mm(a_ref[...], w_ref[...])


def matmul_res(a, w, res, *, tm):
    t, k = a.shape
    n = w.shape[1]
    return pl.pallas_call(
        _matmul_res_kernel, grid=(t // tm,),
        in_specs=[pl.BlockSpec((tm, k), lambda i: (i, 0)),
                  pl.BlockSpec((k, n), lambda i: (0, 0)),
                  pl.BlockSpec((tm, n), lambda i: (i, 0))],
        out_specs=pl.BlockSpec((tm, n), lambda i: (i, 0)),
        out_shape=jax.ShapeDtypeStruct((t, n), F32),
        compiler_params=_cparams(("parallel",)),
        name="matmul_res")(a, w, res)


_CONV_COLS = 512


def _ssd_kernel(z_ref, xs_ref, bc_ref, dt_ref, cw_ref, cb_ref, dtb_ref, alog_ref,
                dsk_ref, ng_ref,
                y_ref, tail_out_ref, st_ref,
                tail_ref, h_ref, xc_ref, xst_ref, yt_ref, dtt_ref, cst_ref, xde_ref, yn_ref):
    q = SSD_CHUNK
    c = pl.program_id(1)

    @pl.when(c == 0)
    def _():
        tail_ref[...] = jnp.zeros_like(tail_ref)
        h_ref[...] = jnp.zeros_like(h_ref)

    row = lax.broadcasted_iota(I32, (q, _CONV_COLS), 0)
    for k in range(SSM_CONV_DIM // _CONV_COLS):
        lo = k * _CONV_COLS
        if lo < SSM_D_INNER:
            src = xs_ref[0, :, lo:lo + _CONV_COLS]
        else:
            src = bc_ref[0, :, lo - SSM_D_INNER:lo - SSM_D_INNER + _CONV_COLS]
        tl = tail_ref[:, lo:lo + _CONV_COLS]
        w = cw_ref[:, lo:lo + _CONV_COLS]
        acc = src * w[CONV_WIDTH - 1:CONV_WIDTH] + cb_ref[:, lo:lo + _CONV_COLS]
        for s in range(1, CONV_WIDTH):
            cur = pltpu.roll(src, s, axis=0)
            prev = jnp.tile(pltpu.roll(tl, s, axis=0), (q // SUBLANES, 1))
            shifted = jnp.where(row < s, prev, cur)
            acc = acc + shifted * w[CONV_WIDTH - 1 - s:CONV_WIDTH - s]
        xc_ref[:, lo:lo + _CONV_COLS] = _silu(acc)
        tail_ref[:, lo:lo + _CONV_COLS] = src[q - SUBLANES:, :]

    for k in range(SSM_D_INNER // LANES):
        xst_ref[k * LANES:(k + 1) * LANES, :] = xc_ref[:, k * LANES:(k + 1) * LANES].T

    dt = _softplus(dt_ref[0] + dtb_ref[...])
    da = dt * (-jnp.exp(alog_ref[...]))
    ri = lax.broadcasted_iota(I32, (q, q), 0)
    ci = lax.broadcasted_iota(I32, (q, q), 1)
    tril = (ci <= ri).astype(F32)
    cs = jnp.dot(tril, da, preferred_element_type=F32, precision=lax.Precision.HIGHEST)
    dtt_ref[...] = dt.T
    cst_ref[...] = cs.T

    causal_t = ci >= ri
    hpg = SSM_HEADS_PER_GROUP
    gp = hpg * SSM_HEAD_DIM
    for g in range(SSM_GROUPS):
        b0 = SSM_D_INNER + g * SSM_D_STATE
        c0 = SSM_D_INNER + SSM_GROUPS * SSM_D_STATE + g * SSM_D_STATE
        bm = xc_ref[:, b0:b0 + SSM_D_STATE].astype(BF16)
        cm = xc_ref[:, c0:c0 + SSM_D_STATE].astype(BF16)
        cbt = lax.dot_general(bm, cm, (((1,), (1,)), ((), ())), preferred_element_type=F32)
        hg = h_ref[g * gp:(g + 1) * gp, :].astype(BF16)
        yt_ref[g * gp:(g + 1) * gp, :] = lax.dot_general(
            hg, cm, (((1,), (1,)), ((), ())), preferred_element_type=F32)

        for e in range(hpg):
            hd = g * hpg + e
            rows = slice(hd * SSM_HEAD_DIM, (hd + 1) * SSM_HEAD_DIM)
            csr = cst_ref[hd:hd + 1, :]
            rowb = jnp.broadcast_to(csr, (q, q))
            colb = rowb.T
            dec = jnp.exp(jnp.where(causal_t, rowb - colb, -jnp.inf))
            mt = (cbt * dec).astype(BF16)
            dtr = dtt_ref[hd:hd + 1, :]
            xsh = xst_ref[rows, :]
            xdt = xsh * dtr
            ydt = jnp.dot(xdt.astype(BF16), mt, preferred_element_type=F32)
            cs_end = csr[:, q - 1:q]
            yt_ref[rows, :] = yt_ref[rows, :] * jnp.exp(csr) + ydt + xsh * dsk_ref[rows, :]
            xde_ref[e * SSM_HEAD_DIM:(e + 1) * SSM_HEAD_DIM, :] = (
                xdt * jnp.exp(cs_end - csr)).astype(BF16)
            h_ref[rows, :] = h_ref[rows, :] * jnp.exp(cs_end)
        h_ref[g * gp:(g + 1) * gp, :] = h_ref[g * gp:(g + 1) * gp, :] + jnp.dot(
            xde_ref[...], bm, preferred_element_type=F32)

    ssq = jnp.zeros((q, 1), F32)
    for k in range(SSM_D_INNER // LANES):
        yk = yt_ref[k * LANES:(k + 1) * LANES, :].T * _silu(z_ref[0, :, k * LANES:(k + 1) * LANES])
        yn_ref[:, k * LANES:(k + 1) * LANES] = yk
        ssq = ssq + jnp.sum(yk * yk, axis=-1, keepdims=True)
    scale = lax.rsqrt(ssq * (1.0 / SSM_D_INNER) + EPS)
    y_ref[0] = (yn_ref[...] * scale * ng_ref[...]).astype(BF16)

    @pl.when(c == pl.num_programs(1) - 1)
    def _():
        tail_out_ref[0] = tail_ref[...]
        st_ref[0] = h_ref[...]


def ssd_prompt(proj, dt_raw, conv_w, conv_b, dt_bias, a_log, d_rows, norm_g, nb, seq):
    nc = seq // SSD_CHUNK
    q = SSD_CHUNK
    const2 = lambda b, c: (0, 0)
    return pl.pallas_call(
        _ssd_kernel, grid=(nb, nc),
        in_specs=[
            pl.BlockSpec((1, q, SSM_D_INNER), lambda b, c: (b, c, 0)),
            pl.BlockSpec((1, q, SSM_D_INNER), lambda b, c: (b, c, 1)),
            pl.BlockSpec((1, q, SSM_BC_DIM), lambda b, c: (b, c, 2 * SSM_D_INNER // SSM_BC_DIM)),
            pl.BlockSpec((1, q, LANES), lambda b, c: (b, c, 0)),
            pl.BlockSpec((CONV_WIDTH, SSM_CONV_DIM), const2),
            pl.BlockSpec((1, SSM_CONV_DIM), const2),
            pl.BlockSpec((1, LANES), const2),
            pl.BlockSpec((1, LANES), const2),
            pl.BlockSpec((SSM_D_INNER, LANES), const2),
            pl.BlockSpec((1, SSM_D_INNER), const2),
        ],
        out_specs=[
            pl.BlockSpec((1, q, SSM_D_INNER), lambda b, c: (b, c, 0)),
            pl.BlockSpec((1, SUBLANES, SSM_CONV_DIM), lambda b, c: (b, 0, 0)),
            pl.BlockSpec((1, SSM_D_INNER, SSM_D_STATE), lambda b, c: (b, 0, 0)),
        ],
        out_shape=[
            jax.ShapeDtypeStruct((nb, seq, SSM_D_INNER), BF16),
            jax.ShapeDtypeStruct((nb, SUBLANES, SSM_CONV_DIM), F32),
            jax.ShapeDtypeStruct((nb, SSM_D_INNER, SSM_D_STATE), F32),
        ],
        scratch_shapes=[
            pltpu.VMEM((SUBLANES, SSM_CONV_DIM), F32),
            pltpu.VMEM((SSM_D_INNER, SSM_D_STATE), F32),
            pltpu.VMEM((q, SSM_CONV_DIM), F32),
            pltpu.VMEM((SSM_D_INNER, q), F32),
            pltpu.VMEM((SSM_D_INNER, q), F32),
            pltpu.VMEM((LANES, q), F32),
            pltpu.VMEM((LANES, q), F32),
            pltpu.VMEM((SSM_HEADS_PER_GROUP * SSM_HEAD_DIM, q), BF16),
            pltpu.VMEM((q, SSM_D_INNER), F32),
        ],
        compiler_params=_cparams(("parallel", "arbitrary")),
        name="ssd_prompt")(proj, proj, proj, dt_raw, conv_w, conv_b, dt_bias, a_log, d_rows, norm_g)


def _pad_lanes(v, n=LANES):
    return jnp.pad(v.astype(F32), (0, n - v.shape[0])).reshape(1, n)


def _ssm_weights(w_in, dtype):
    n_main = SSM_D_INNER + SSM_CONV_DIM
    w_main = w_in[:, :n_main].astype(dtype)
    w_dt = jnp.pad(w_in[:, n_main:], ((0, 0), (0, LANES - SSM_HEADS))).astype(dtype)
    return w_main, w_dt


def _mamba_prompt(x, norm_g, w_in, conv_w, conv_b, dt_bias, a_log, d_skip, ssm_norm, w_out):
    nb, seq, d = x.shape
    t = nb * seq
    w_main, w_dt = _ssm_weights(w_in, BF16)
    tm = min(t, 1024)
    proj, dt_raw = norm_matmul(x.reshape(t, d), norm_g.reshape(1, d), w_main, w_dt, tm=tm, tn=1024)
    d_rows = jnp.broadcast_to(jnp.repeat(d_skip.astype(F32), SSM_HEAD_DIM)[:, None], (SSM_D_INNER, LANES))
    y, tail, st = ssd_prompt(
        proj.reshape(nb, seq, -1), dt_raw.reshape(nb, seq, LANES),
        conv_w, conv_b.reshape(1, -1), _pad_lanes(dt_bias), _pad_lanes(a_log), d_rows,
        ssm_norm.reshape(1, -1), nb, seq)
    x_new = matmul_res(y.reshape(t, SSM_D_INNER), w_out.astype(BF16), x.reshape(t, d), tm=tm)
    conv_state = tail[:, SUBLANES - (CONV_WIDTH - 1):]
    ssm_state = st.reshape(nb, SSM_HEADS, SSM_HEAD_DIM, SSM_D_STATE)
    return x_new.reshape(nb, seq, d), conv_state, ssm_state


_DEC_HEADS = 4


def _ssd_decode_kernel(zt_ref, xbct_ref, convt_ref, dtt_raw_ref, cwt_ref, cbt_ref, dtbt_ref,
                       alogt_ref, dsk_ref, ngt_ref, wout_ref, res_ref, st_ref,
                       so_ref, out_ref,
                       xct_ref, dtt_ref, dect_ref, bcn_ref, yt_ref, *, nb):
    step = pl.program_id(0)
    rb = 512

    @pl.when(step == 0)
    def _():
        for k in range(SSM_CONV_DIM // rb):
            rows = slice(k * rb, (k + 1) * rb)
            acc = xbct_ref[rows, :] * cwt_ref[rows, CONV_WIDTH - 1:CONV_WIDTH] + cbt_ref[rows, 0:1]
            for j in range(CONV_WIDTH - 1):
                acc = acc + convt_ref[j, rows, :] * cwt_ref[rows, j:j + 1]
            xct_ref[rows, :] = _silu(acc)
        dt = _softplus(dtt_raw_ref[...] + dtbt_ref[:, 0:1])
        dtt_ref[...] = dt
        dect_ref[...] = jnp.exp(dt * (-jnp.exp(alogt_ref[:, 0:1])))
        for k in range(SSM_BC_DIM // LANES):
            bcn_ref[:, k * LANES:(k + 1) * LANES] = xct_ref[
                SSM_D_INNER + k * LANES:SSM_D_INNER + (k + 1) * LANES, :].T
        yt_ref[...] = jnp.zeros_like(yt_ref)

    for hh in range(_DEC_HEADS):
        hd = step * _DEC_HEADS + hh
        r0 = pl.multiple_of(hd * SSM_HEAD_DIM, SSM_HEAD_DIM)
        grp = hd // SSM_HEADS_PER_GROUP
        b_lane = pl.multiple_of(grp * SSM_D_STATE, SSM_D_STATE)
        c_lane = pl.multiple_of(SSM_GROUPS * SSM_D_STATE + grp * SSM_D_STATE, SSM_D_STATE)
        xh = xct_ref[pl.ds(r0, SSM_HEAD_DIM), :]
        dth = dtt_ref[pl.ds(hd, 1), :]
        dech = dect_ref[pl.ds(hd, 1), :]
        xdt = xh * dth
        for b in range(nb):
            brow = bcn_ref[b:b + 1, pl.ds(b_lane, SSM_D_STATE)]
            crow = bcn_ref[b:b + 1, pl.ds(c_lane, SSM_D_STATE)]
            hn = st_ref[b, hh] * dech[:, b:b + 1] + xdt[:, b:b + 1] * brow
            so_ref[b, hh] = hn
            yt_ref[pl.ds(r0, SSM_HEAD_DIM), b:b + 1] = jnp.sum(hn * crow, axis=-1, keepdims=True)

    @pl.when(step == pl.num_programs(0) - 1)
    def _():
        ssq = jnp.zeros((1, LANES), F32)
        for k in range(SSM_D_INNER // rb):
            rows = slice(k * rb, (k + 1) * rb)
            y = (yt_ref[rows, :] + xct_ref[rows, :] * dsk_ref[rows, :]) * _silu(zt_ref[rows, :])
            yt_ref[rows, :] = y
            ssq = ssq + jnp.sum(y * y, axis=0, keepdims=True)
        scale = lax.rsqrt(ssq * (1.0 / SSM_D_INNER) + EPS)
        acc = res_ref[...]
        for k in range(SSM_D_INNER // LANES):
            rows = slice(k * LANES, (k + 1) * LANES)
            yn = (yt_ref[rows, :] * scale * ngt_ref[rows, :]).T
            acc = acc + _mm(yn, wout_ref[rows, :])
        out_ref[...] = acc


def ssd_decode(zt, xbct, convt, dtt_raw, cwt, cbt, dtbt, alogt, d_rows, ngt, w_out, res, state):
    nb = state.shape[0]
    n_steps = SSM_HEADS // _DEC_HEADS
    full = lambda a: pl.BlockSpec(a.shape, lambda s, n=a.ndim: (0,) * n)
    st_spec = pl.BlockSpec((nb, _DEC_HEADS, SSM_HEAD_DIM, SSM_D_STATE), lambda s: (0, s, 0, 0))
    small = (zt, xbct, convt, dtt_raw, cwt, cbt, dtbt, alogt, d_rows, ngt, w_out, res)
    return pl.pallas_call(
        functools.partial(_ssd_decode_kernel, nb=nb), grid=(n_steps,),
        in_specs=[full(a) for a in small] + [st_spec],
        out_specs=[st_spec, full(res)],
        out_shape=[jax.ShapeDtypeStruct(state.shape, F32), jax.ShapeDtypeStruct(res.shape, F32)],
        scratch_shapes=[
            pltpu.VMEM((SSM_CONV_DIM, LANES), F32),
            pltpu.VMEM((LANES, LANES), F32),
            pltpu.VMEM((LANES, LANES), F32),
            pltpu.VMEM((LANES, SSM_BC_DIM), F32),
            pltpu.VMEM((SSM_D_INNER, LANES), F32),
        ],
        compiler_params=_cparams(("arbitrary",)),
        name="ssd_decode")(*small, state)


def _to_lanes(a, n=LANES):
    return jnp.pad(a.T, ((0, 0), (0, n - a.shape[0])))


def _mamba_sample(x, norm_g, conv_state, ssm_state, w_in, conv_w, conv_b, dt_bias, a_log, d_skip,
                  ssm_norm, w_out):
    nb, _, d = x.shape
    assert nb <= LANES and nb % SUBLANES == 0
    w_main, w_dt = _ssm_weights(w_in, F32)
    proj, dt_raw = norm_matmul(x.reshape(nb, d), norm_g.reshape(1, d), w_main, w_dt, tm=nb, tn=512)
    xbc = proj[:, SSM_D_INNER:]
    col = lambda v: jnp.broadcast_to(v.astype(F32)[:, None], (v.shape[0], LANES))
    convt = jnp.pad(jnp.transpose(conv_state, (1, 2, 0)), ((0, 0), (0, 0), (0, LANES - nb)))
    res = jnp.pad(x.reshape(nb, d), ((0, LANES - nb), (0, 0)))
    st, out = ssd_decode(
        _to_lanes(proj[:, :SSM_D_INNER]), _to_lanes(xbc), convt, _to_lanes(dt_raw),
        jnp.pad(conv_w.T, ((0, 0), (0, LANES - CONV_WIDTH))), col(conv_b),
        col(_pad_lanes(dt_bias)[0]), col(_pad_lanes(a_log)[0]),
        col(jnp.repeat(d_skip, SSM_HEAD_DIM)), col(ssm_norm), w_out, res, ssm_state)
    new_conv = jnp.concatenate([conv_state[:, 1:], xbc[:, None, :]], axis=1)
    return out[:nb].reshape(nb, 1, d), new_conv, st


def _attn_block(q_ref, k_ref, v_ref, o_scr, lam_scr, start, dil, has_prev):
    nq = ATTN_BLOCK
    rows = pl.ds(start, nq, stride=dil) if dil > 1 else pl.ds(start, nq)
    hd = ATTN_HEAD_DIM
    q2 = q_ref[rows, :]
    lane = lax.broadcasted_iota(I32, (nq, LANES), 1)
    qblk = jnp.concatenate([jnp.where(lane < hd, q2, 0.0), jnp.where(lane >= hd, q2, 0.0)],
                           axis=0).astype(BF16)
    kj = lax.broadcasted_iota(I32, (nq, 2 * nq), 0)
    qi = lax.broadcasted_iota(I32, (nq, 2 * nq), 1) % nq
    nt = (((1,), (1,)), ((), ()))
    s = lax.dot_general(k_ref[rows, :].astype(BF16), qblk, nt,
                        preferred_element_type=F32) * ATTN_SCALE
    s = jnp.where(kj <= qi, s, -jnp.inf)
    vt = v_ref[rows, :].T.astype(BF16)
    if has_prev:
        prows = (pl.ds(start - dil * nq, nq, stride=dil) if dil > 1 else pl.ds(start - nq, nq))
        sp = lax.dot_general(k_ref[prows, :].astype(BF16), qblk, nt,
                             preferred_element_type=F32) * ATTN_SCALE
        s = jnp.concatenate([jnp.where(kj >= qi, sp, -jnp.inf), s], axis=0)
        vt = jnp.concatenate([v_ref[prows, :].T.astype(BF16), vt], axis=1)
    m = jnp.max(s, axis=0, keepdims=True)
    p = jnp.exp(s - m)
    l = jnp.sum(p, axis=0, keepdims=True)
    ot = jnp.dot(vt, p.astype(BF16), preferred_element_type=F32) * (1.0 / l)
    lam = jnp.broadcast_to(m + jnp.log(l), (hd, 2 * nq))
    o_scr[rows, :] = jnp.concatenate([ot[:hd, :nq], ot[hd:, nq:]], axis=0).T
    lam_scr[rows, :] = jnp.concatenate([lam[:, :nq], lam[:, nq:]], axis=0).T


def _attn_prompt_kernel(*refs, seq):
    qkv = refs[:9]
    y_ref = refs[9]
    kvt_refs = refs[10:13]
    o_scrs, lam_scrs = refs[13:16], refs[16:19]
    nq = ATTN_BLOCK
    for g, (window, dil) in enumerate(ATTN_PATTERNS):
        q_ref, k_ref, v_ref = qkv[3 * g:3 * g + 3]
        o_scr, lam_scr = o_scrs[g], lam_scrs[g]
        keep = min(window, seq)
        for c in range(keep // LANES):
            src = slice(seq - keep + c * LANES, seq - keep + (c + 1) * LANES)
            kvt_refs[g][0, :, c * LANES:(c + 1) * LANES] = k_ref[src, :].T
            kvt_refs[g][1, :, c * LANES:(c + 1) * LANES] = v_ref[src, :].T
        n_blk = seq // (dil * nq)
        for r in range(dil):
            for n in range(n_blk):
                _attn_block(q_ref, k_ref, v_ref, o_scr, lam_scr, r + dil * nq * n, dil, n > 0)

    rows_per = 256

    def merge(i, carry):
        rows = pl.ds(pl.multiple_of(i * rows_per, rows_per), rows_per)
        l0, l1, l2 = lam_scrs[0][rows, :], lam_scrs[1][rows, :], lam_scrs[2][rows, :]
        mx = jnp.maximum(jnp.maximum(l0, l1), l2)
        e0, e1, e2 = jnp.exp(l0 - mx), jnp.exp(l1 - mx), jnp.exp(l2 - mx)
        num = e0 * o_scrs[0][rows, :] + e1 * o_scrs[1][rows, :] + e2 * o_scrs[2][rows, :]
        y_ref[rows, :] = (num / (e0 + e1 + e2)).astype(BF16)
        return carry

    lax.fori_loop(0, seq // rows_per, merge, 0)


def attn_prompt(qkv, nb, seq):
    for window, dil in ATTN_PATTERNS:
        assert window // dil == ATTN_BLOCK and seq % (dil * ATTN_BLOCK) == 0
    n_pairs = ATTN_WIDTH // LANES
    keeps = [min(window, seq) for window, _ in ATTN_PATTERNS]
    in_specs = []
    for g in range(len(ATTN_PATTERNS)):
        for j in range(3):
            in_specs.append(pl.BlockSpec(
                (None, seq, LANES), lambda b, hp, g=g, j=j: (b, 0, (g * 3 + j) * n_pairs + hp)))
    outs = pl.pallas_call(
        functools.partial(_attn_prompt_kernel, seq=seq),
        grid=(nb, n_pairs),
        in_specs=in_specs,
        out_specs=[pl.BlockSpec((None, seq, LANES), lambda b, hp: (b, 0, hp))] + [
            pl.BlockSpec((None, 2, LANES, keep), lambda b, hp: (b, 0, hp, 0)) for keep in keeps],
        out_shape=[jax.ShapeDtypeStruct((nb, seq, ATTN_WIDTH), BF16)] + [
            jax.ShapeDtypeStruct((nb, 2, ATTN_WIDTH, keep), F32) for keep in keeps],
        scratch_shapes=[pltpu.VMEM((seq, LANES), F32)] * 6,
        compiler_params=_cparams(("parallel", "parallel")),
        name="attn_prompt")(*([qkv] * 9))
    kv = [jnp.transpose(kvt.reshape(nb, 2, ATTN_HEADS, ATTN_HEAD_DIM, keep), (0, 4, 1, 2, 3))
          for kvt, keep in zip(outs[1:], keeps)]
    return outs[0], kv


_DEC_ATTN_HEADS = 4


def _attn_decode_kernel(qkv_ref, c0_ref, c1_ref, c2_ref, y_ref, pad_ref):
    caches = (c0_ref, c1_ref, c2_ref)
    hg = pl.program_id(1)
    hd = ATTN_HEAD_DIM
    lane = lax.broadcasted_iota(I32, (hd, LANES), 1)

    def head_columns(a):
        pad_ref[...] = jnp.zeros_like(pad_ref)
        pad_ref[0:ATTN_HEADS, 0:hd] = a
        return pad_ref[...].T[0:hd, :]

    cols = [[head_columns(qkv_ref[g, j]) for j in range(3)] for g in range(len(ATTN_PATTERNS))]
    acc = jnp.zeros((hd, LANES), F32)
    for hh in range(_DEC_ATTN_HEADS):
        h = hg * _DEC_ATTN_HEADS + hh
        pick = lambda a: jnp.sum(jnp.where(lane == h, a, 0.0), axis=-1, keepdims=True)
        outs, lams = [], []
        for g, (window, dil) in enumerate(ATTN_PATTERNS):
            qc, knc, vnc = (pick(a) for a in cols[g])
            kt = caches[g][0, hh]
            vt = caches[g][1, hh]
            s = jnp.sum(kt * qc, axis=0, keepdims=True) * ATTN_SCALE
            if dil > 1:
                tok = lax.broadcasted_iota(I32, (1, window), 1)
                s = jnp.where(tok % dil == 0, s, -jnp.inf)
            s_new = jnp.sum(knc * qc, axis=0, keepdims=True) * ATTN_SCALE
            m = jnp.maximum(jnp.max(s, axis=-1, keepdims=True), s_new)
            p = jnp.exp(s - m)
            p_new = jnp.exp(s_new - m)
            l = jnp.sum(p, axis=-1, keepdims=True) + p_new
            o = (jnp.sum(vt * p, axis=-1, keepdims=True) + p_new * vnc) / l
            outs.append(o)
            lams.append(m + jnp.log(l))
        mx = jnp.maximum(jnp.maximum(lams[0], lams[1]), lams[2])
        es = [jnp.exp(lm - mx) for lm in lams]
        y = (es[0] * outs[0] + es[1] * outs[1] + es[2] * outs[2]) / (es[0] + es[1] + es[2])
        acc = jnp.where(lane == hh, y, acc)
    y_ref[...] = acc


def attn_decode(qkv_s, caches):
    nb = qkv_s.shape[0]
    hpb = _DEC_ATTN_HEADS
    in_specs = [pl.BlockSpec((None, 3, 3, ATTN_HEADS, ATTN_HEAD_DIM), lambda b, g: (b, 0, 0, 0, 0))]
    args = [qkv_s]
    for c, (window, dil) in zip(caches, ATTN_PATTERNS):
        assert c.shape[1] == window and window % dil == 0, "cache must hold a full window"
        args.append(jnp.transpose(c, (0, 2, 3, 4, 1)))
        in_specs.append(pl.BlockSpec((None, 2, hpb, ATTN_HEAD_DIM, window),
                                     lambda b, g: (b, 0, g, 0, 0)))
    y = pl.pallas_call(
        _attn_decode_kernel, grid=(nb, ATTN_HEADS // hpb),
        in_specs=in_specs,
        out_specs=pl.BlockSpec((None, None, ATTN_HEAD_DIM, LANES), lambda b, g: (b, g, 0, 0)),
        out_shape=jax.ShapeDtypeStruct((nb, ATTN_HEADS // hpb, ATTN_HEAD_DIM, LANES), F32),
        scratch_shapes=[pltpu.VMEM((LANES, LANES), F32)],
        compiler_params=_cparams(("parallel", "arbitrary")),
        name="attn_decode")(*args)
    return jnp.transpose(y[..., :hpb], (0, 1, 3, 2)).reshape(nb, ATTN_WIDTH)


_R_E1, _R_E2, _R_RANK1, _R_RANK2, _R_W1, _R_W2 = range(6)


def _router_kernel(x_ref, g_ref, wr_ref, info_ref, cnt_ref, idx_ref, *rest):
    xt_ref = rest[0] if len(rest) == 2 else None
    carry_ref = rest[-1]
    tm = x_ref.shape[0]

    @pl.when(pl.program_id(0) == 0)
    def _():
        carry_ref[...] = jnp.zeros_like(carry_ref)

    xn = _rms(x_ref[...], g_ref[...])
    if xt_ref is not None:
        _rows_to_tiles(xt_ref, xn)
    logits = _mm(xn, wr_ref[...])
    lane = lax.broadcasted_iota(I32, (tm, LANES), 1)
    lanef = lane.astype(F32)
    big = float(LANES)
    neg = -jnp.inf

    is_g = (lane >= GROUP_LOGIT_LANE) & (lane < GROUP_LOGIT_LANE + N_EXPERT_GROUPS)
    gl = jnp.where(is_g, logits, neg)
    gmax = jnp.max(gl, axis=-1, keepdims=True)
    grp = jnp.min(jnp.where(gl == gmax, lanef, big), axis=-1, keepdims=True) - GROUP_LOGIT_LANE
    g_gate = 1.0 / jnp.sum(jnp.exp(gl - gmax), axis=-1, keepdims=True)

    lo = grp * EXPERTS_PER_GROUP
    in_grp = (lanef >= lo) & (lanef < lo + EXPERTS_PER_GROUP)
    el = jnp.where(in_grp, logits, neg)
    m1 = jnp.max(el, axis=-1, keepdims=True)
    i1 = jnp.min(jnp.where(el == m1, lanef, big), axis=-1, keepdims=True)
    el2 = jnp.where(lanef == i1, neg, el)
    m2 = jnp.max(el2, axis=-1, keepdims=True)
    i2 = jnp.min(jnp.where(el2 == m2, lanef, big), axis=-1, keepdims=True)
    ratio = jnp.exp(m2 - m1)
    w1 = g_gate / (1.0 + ratio)
    w2 = g_gate * ratio / (1.0 + ratio)

    oh1 = lanef == i1
    oh2 = lanef == i2
    a = (oh1 | oh2).astype(BF16)
    ri = lax.broadcasted_iota(I32, (tm, tm), 0)
    ci = lax.broadcasted_iota(I32, (tm, tm), 1)
    before = (ci < ri).astype(BF16)
    pref = jnp.dot(before, a, preferred_element_type=F32) + carry_ref[...]
    rank1 = jnp.sum(jnp.where(oh1, pref, 0.0), axis=-1, keepdims=True)
    rank2 = jnp.sum(jnp.where(oh2, pref, 0.0), axis=-1, keepdims=True)
    carry_ref[...] = carry_ref[...] + jnp.sum(a.astype(F32), axis=0, keepdims=True)

    info = jnp.zeros((tm, LANES), F32)
    for ln, val in ((_R_E1, i1), (_R_E2, i2), (_R_RANK1, rank1), (_R_RANK2, rank2),
                    (_R_W1, w1), (_R_W2, w2)):
        info = jnp.where(lane == ln, val, info)
    info_ref[...] = info
    idx_ref[...] = info.T[0:SUBLANES, :]
    cnt_ref[...] = carry_ref[...]


def moe_router(x, g, wr, *, tm, emit_tiles=False):
    t, d = x.shape
    out_specs = [pl.BlockSpec((tm, LANES), lambda i: (i, 0)), pl.BlockSpec((1, LANES), lambda i: (0, 0)),
                 pl.BlockSpec((SUBLANES, tm), lambda i: (0, i))]
    out_shape = [jax.ShapeDtypeStruct((t, LANES), F32), jax.ShapeDtypeStruct((1, LANES), F32),
                 jax.ShapeDtypeStruct((SUBLANES, t), F32)]
    if emit_tiles:
        out_specs.append(pl.BlockSpec((tm * SUBLANES, LANES), lambda i: (i, 0)))
        out_shape.append(jax.ShapeDtypeStruct((t * SUBLANES, LANES), F32))
    return pl.pallas_call(
        _router_kernel, grid=(t // tm,),
        in_specs=[pl.BlockSpec((tm, d), lambda i: (i, 0)),
                  pl.BlockSpec((1, d), lambda i: (0, 0)),
                  pl.BlockSpec((d, LANES), lambda i: (0, 0))],
        out_specs=out_specs,
        out_shape=out_shape,
        scratch_shapes=[pltpu.VMEM((1, LANES), F32)],
        compiler_params=_cparams(("arbitrary",)),
        name="moe_router")(x, g, wr)


_ROW_UNROLL = 8
_ROW_TILES = D_MODEL // LANES
assert _ROW_TILES == SUBLANES


def _rows_to_tiles(ref, val):
    n = val.shape[0]
    for j in range(_ROW_TILES):
        ref[pl.ds(j, n, stride=_ROW_TILES), :] = val[:, j * LANES:(j + 1) * LANES]


def _tiles_to_rows(ref, n):
    return jnp.concatenate([ref[pl.ds(j, n, stride=_ROW_TILES), :] for j in range(_ROW_TILES)], axis=1)


def _tile_at(ref, row8):
    return ref.at[pl.ds(pl.multiple_of(row8, _ROW_TILES), _ROW_TILES)]


def _pos_blocks(pos, tm):
    n = pos.shape[0] // tm
    return (pos * _ROW_TILES).reshape(n, tm, 2).transpose(0, 2, 1).reshape(n, 1, 2 * tm)


def _row_copies(n_rows, make_copies):
    def body(i, carry):
        for u in range(_ROW_UNROLL):
            for cp in make_copies(i * _ROW_UNROLL + u):
                cp.start()
        return carry
    lax.fori_loop(0, n_rows // _ROW_UNROLL, body, 0)


def _dispatch_kernel(pos_ref, x_ref, g_ref, xs_in_ref, xs_ref, xn_ref, sem):
    del xs_in_ref
    tm = x_ref.shape[0]
    _rows_to_tiles(xn_ref, _rms(x_ref[...], g_ref[...]))

    def copies(r):
        src = _tile_at(xn_ref, r * _ROW_TILES)
        return [pltpu.make_async_copy(src, _tile_at(xs_ref, pos_ref[0, 0, k * tm + r]), sem)
                for k in range(2)]

    _row_copies(tm, copies)
    for _ in range(2):
        pltpu.make_async_copy(xn_ref, xs_ref.at[pl.ds(0, tm * _ROW_TILES)], sem).wait()


def moe_dispatch(x, g, pos, xs_init, *, tm):
    t, d = x.shape
    n = t // tm
    pos_blk = _pos_blocks(pos, tm)
    return pl.pallas_call(
        _dispatch_kernel, grid=(n,),
        in_specs=[pl.BlockSpec((1, 1, 2 * tm), lambda i: (i, 0, 0), memory_space=pltpu.SMEM),
                  pl.BlockSpec((tm, d), lambda i: (i, 0)),
                  pl.BlockSpec((1, d), lambda i: (0, 0)),
                  pl.BlockSpec(memory_space=pl.ANY)],
        out_specs=pl.BlockSpec(memory_space=pl.ANY),
        out_shape=jax.ShapeDtypeStruct(xs_init.shape, F32),
        scratch_shapes=[pltpu.VMEM((tm * _ROW_TILES, LANES), F32), pltpu.SemaphoreType.DMA(())],
        input_output_aliases={3: 0},
        compiler_params=_cparams(("arbitrary",)),
        name="moe_dispatch")(pos_blk, x, g, xs_init)


def _expert_kernel(te_ref, tv_ref, xs_ref, wgu_ref, wdn_ref, ys_ref, wgu_bf, wdn_bf):
    t = pl.program_id(0)
    tmx = xs_ref.shape[0] // _ROW_TILES
    changed = jnp.logical_or(t == 0, te_ref[t] != te_ref[jnp.maximum(t - 1, 0)])

    @pl.when(changed)
    def _():
        wgu_bf[...] = wgu_ref[...].astype(BF16)
        wdn_bf[...] = wdn_ref[...].astype(BF16)

    @pl.when(tv_ref[t] > 0)
    def _():
        x = _tiles_to_rows(xs_ref, tmx).astype(BF16)
        gu = jnp.dot(x, wgu_bf[...], preferred_element_type=F32)
        act = _silu(gu[:, :D_EXPERT]) * gu[:, D_EXPERT:]
        _rows_to_tiles(ys_ref, jnp.dot(act.astype(BF16), wdn_bf[...], preferred_element_type=F32))

    @pl.when(tv_ref[t] == 0)
    def _():
        ys_ref[...] = jnp.zeros_like(ys_ref)


def moe_experts(xs, w_gate_up, w_down, layer, tile_expert, tile_valid, *, tmx):
    d = w_gate_up.shape[2]
    blk = tmx * _ROW_TILES
    n_tiles = xs.shape[0] // blk
    grid_spec = pltpu.PrefetchScalarGridSpec(
        num_scalar_prefetch=2, grid=(n_tiles,),
        in_specs=[pl.BlockSpec((blk, LANES), lambda t, te, tv: (t, 0)),
                  pl.BlockSpec((None, None, d, 2 * D_EXPERT), lambda t, te, tv: (layer, te[t], 0, 0)),
                  pl.BlockSpec((None, None, D_EXPERT, d), lambda t, te, tv: (layer, te[t], 0, 0))],
        out_specs=pl.BlockSpec((blk, LANES), lambda t, te, tv: (t, 0)),
        scratch_shapes=[pltpu.VMEM((d, 2 * D_EXPERT), BF16), pltpu.VMEM((D_EXPERT, d), BF16)])
    return pl.pallas_call(
        _expert_kernel, grid_spec=grid_spec,
        out_shape=jax.ShapeDtypeStruct(xs.shape, F32),
        compiler_params=_cparams(("arbitrary",)),
        name="moe_experts")(tile_expert, tile_valid, xs, w_gate_up, w_down)


def _combine_kernel(pos_ref, x_ref, info_ref, g_ref, ys_ref, o_ref, buf0_ref, buf1_ref, sem, *,
                    final_norm):
    tm = x_ref.shape[0]
    bufs = (buf0_ref, buf1_ref)

    def copies(r):
        return [pltpu.make_async_copy(_tile_at(ys_ref, pos_ref[0, 0, k * tm + r]),
                                      _tile_at(bufs[k], r * _ROW_TILES), sem)
                for k in range(2)]

    _row_copies(tm, copies)
    for k in range(2):
        pltpu.make_async_copy(ys_ref.at[pl.ds(0, tm * _ROW_TILES)], bufs[k], sem).wait()
    info = info_ref[...]
    out = (x_ref[...] + info[:, _R_W1:_R_W1 + 1] * _tiles_to_rows(buf0_ref, tm)
           + info[:, _R_W2:_R_W2 + 1] * _tiles_to_rows(buf1_ref, tm))
    if final_norm:
        out = _rms(out, g_ref[...])
    o_ref[...] = out


def moe_combine(x, info, pos, ys, g_final, *, tm, final_norm):
    t, d = x.shape
    n = t // tm
    pos_blk = _pos_blocks(pos, tm)
    return pl.pallas_call(
        functools.partial(_combine_kernel, final_norm=final_norm), grid=(n,),
        in_specs=[pl.BlockSpec((1, 1, 2 * tm), lambda i: (i, 0, 0), memory_space=pltpu.SMEM),
                  pl.BlockSpec((tm, d), lambda i: (i, 0)),
                  pl.BlockSpec((tm, LANES), lambda i: (i, 0)),
                  pl.BlockSpec((1, d), lambda i: (0, 0)),
                  pl.BlockSpec(memory_space=pl.ANY)],
        out_specs=pl.BlockSpec((tm, d), lambda i: (i, 0)),
        out_shape=jax.ShapeDtypeStruct((t, d), F32),
        scratch_shapes=[pltpu.VMEM((tm * _ROW_TILES, LANES), F32)] * 2 + [pltpu.SemaphoreType.DMA(())],
        compiler_params=_cparams(("arbitrary",)),
        name="moe_combine")(pos_blk, x, info, g_final, ys)


_SC_CORES = 2
_SC_SUBCORES = 16
_SC_CHUNK = 64


def sc_gather_rows(table, idx):
    b = idx.shape[0]
    n_workers = _SC_CORES * _SC_SUBCORES
    per_worker = b // n_workers
    assert per_worker * n_workers == b and per_worker % _SC_CHUNK == 0
    mesh = plsc.VectorSubcoreMesh(core_axis_name="c", subcore_axis_name="s",
                                  num_cores=_SC_CORES, num_subcores=_SC_SUBCORES)

    @functools.partial(
        pl.kernel, mesh=mesh,
        out_type=jax.ShapeDtypeStruct((b,) + table.shape[1:], table.dtype),
        scratch_types=[pltpu.VMEM((_SC_CHUNK,), I32),
                       pltpu.VMEM((_SC_CHUNK,) + table.shape[1:], table.dtype),
                       pltpu.SemaphoreType.DMA],
        name="sc_gather_rows")
    def gather(table_hbm, idx_hbm, out_hbm, idx_v, rows_v, sem):
        worker = lax.axis_index("s") * _SC_CORES + lax.axis_index("c")
        base = worker * per_worker

        def chunk(i, carry):
            off = pl.multiple_of(base + i * _SC_CHUNK, _SC_CHUNK)
            pltpu.sync_copy(idx_hbm.at[pl.ds(off, _SC_CHUNK)], idx_v)
            pltpu.async_copy(table_hbm.at[idx_v], rows_v, sem).wait()
            pltpu.sync_copy(rows_v, out_hbm.at[pl.ds(off, _SC_CHUNK)])
            return carry

        lax.fori_loop(0, per_worker // _SC_CHUNK, chunk, 0)

    return gather(table, idx)


def sc_scatter_rows(rows, dst, n_tokens):
    b = dst.shape[0]
    n_workers = _SC_CORES * _SC_SUBCORES
    per_worker = b // n_workers
    assert per_worker * n_workers == b and per_worker % _SC_CHUNK == 0
    assert n_tokens % _SC_CHUNK == 0 and b - 2 * n_tokens <= n_tokens
    mesh = plsc.VectorSubcoreMesh(core_axis_name="c", subcore_axis_name="s",
                                  num_cores=_SC_CORES, num_subcores=_SC_SUBCORES)

    @functools.partial(
        pl.kernel, mesh=mesh,
        out_type=jax.ShapeDtypeStruct((b,) + rows.shape[1:], rows.dtype),
        scratch_types=[pltpu.VMEM((_SC_CHUNK,), I32),
                       pltpu.VMEM((_SC_CHUNK,) + rows.shape[1:], rows.dtype),
                       pltpu.SemaphoreType.DMA],
        name="sc_scatter_rows")
    def scatter(rows_hbm, dst_hbm, out_hbm, idx_v, rows_v, sem):
        worker = lax.axis_index("s") * _SC_CORES + lax.axis_index("c")
        base = worker * per_worker

        def chunk(i, carry):
            off = pl.multiple_of(base + i * _SC_CHUNK, _SC_CHUNK)
            src = jnp.where(off < 2 * n_tokens, lax.rem(off, n_tokens), off - 2 * n_tokens)
            src = pl.multiple_of(src, _SC_CHUNK)
            pltpu.sync_copy(dst_hbm.at[pl.ds(off, _SC_CHUNK)], idx_v)
            pltpu.sync_copy(rows_hbm.at[pl.ds(src, _SC_CHUNK)], rows_v)
            pltpu.async_copy(rows_v, out_hbm.at[idx_v], sem).wait()
            return carry

        lax.fori_loop(0, per_worker // _SC_CHUNK, chunk, 0)

    return scatter(rows, dst)


def _combine_dense_kernel(x_ref, info_ref, g_ref, y0_ref, y1_ref, o_ref, *, final_norm):
    tm = x_ref.shape[0]
    info = info_ref[...]
    out = (x_ref[...] + info[:, _R_W1:_R_W1 + 1] * _tiles_to_rows(y0_ref, tm)
           + info[:, _R_W2:_R_W2 + 1] * _tiles_to_rows(y1_ref, tm))
    if final_norm:
        out = _rms(out, g_ref[...])
    o_ref[...] = out


def moe_combine_dense(x, info, y_slots, g_final, *, tm, final_norm):
    t, d = x.shape
    n = t // tm
    blk = tm * _ROW_TILES
    return pl.pallas_call(
        functools.partial(_combine_dense_kernel, final_norm=final_norm), grid=(n,),
        in_specs=[pl.BlockSpec((tm, d), lambda i: (i, 0)),
                  pl.BlockSpec((tm, LANES), lambda i: (i, 0)),
                  pl.BlockSpec((1, d), lambda i: (0, 0)),
                  pl.BlockSpec((blk, LANES), lambda i: (i, 0)),
                  pl.BlockSpec((blk, LANES), lambda i: (n + i, 0))],
        out_specs=pl.BlockSpec((tm, d), lambda i: (i, 0)),
        out_shape=jax.ShapeDtypeStruct((t, d), F32),
        compiler_params=_cparams(("parallel",)),
        name="moe_combine_dense")(x, info, g_final, y_slots, y_slots)


def _router_weight(w_group, w_expert, dtype):
    d = w_group.shape[0]
    we = jnp.transpose(w_expert, (1, 0, 2)).reshape(d, N_EXPERTS)
    w = jnp.concatenate([we, w_group], axis=1)
    return jnp.pad(w, ((0, 0), (0, LANES - w.shape[1]))).astype(dtype)


def hier_moe_layer(x, norm_g, w_group, w_expert, w_gate_up, w_down, layer, g_final, *, tm, tmx,
                   final_norm, route_dtype):
    t, d = x.shape
    g = norm_g.reshape(1, d)
    n_rows = 2 * t + N_EXPERTS * tmx
    sc_rows = _SC_CORES * _SC_SUBCORES * _SC_CHUNK
    use_sc = (2 * t) % sc_rows == 0 and n_rows % sc_rows == 0
    routed = moe_router(x, g, _router_weight(w_group, w_expert, route_dtype), tm=tm, emit_tiles=use_sc)
    info, counts = routed[0], routed[1]

    counts = counts[0, :N_EXPERTS].astype(I32)
    padded = ((counts + tmx - 1) // tmx) * tmx
    ends = jnp.cumsum(padded)
    offsets = ends - padded
    e12 = routed[2][_R_E1:_R_E2 + 1].astype(I32)
    rank12 = routed[2][_R_RANK1:_R_RANK2 + 1].astype(I32)
    expert_ids = jnp.arange(N_EXPERTS, dtype=I32)[:, None, None]
    seg_start = jnp.sum(jnp.where(e12[None] == expert_ids, offsets[:, None, None], 0), axis=0)
    slot_major = (seg_start + rank12).reshape(2 * t)
    n_tiles = n_rows // tmx
    tile_start = jnp.arange(n_tiles, dtype=I32) * tmx
    tile_valid = (tile_start < ends[-1]).astype(I32)
    probe = jnp.minimum(tile_start, jnp.maximum(ends[-1] - 1, 0))
    tile_expert = jnp.sum((probe[:, None] >= ends[None, :]).astype(I32), axis=1)
    tile_expert = jnp.minimum(tile_expert, N_EXPERTS - 1)

    depth = w_gate_up.shape[0]
    pos = slot_major.reshape(2, t).T
    if use_sc:
        j = jnp.arange(tmx, dtype=I32)[None, :]
        is_pad = (j < (padded - counts)[:, None]).reshape(-1)
        pad_pos = ((offsets + counts)[:, None] + j).reshape(-1)
        spare = jnp.cumsum((~is_pad).astype(I32)) - 1
        dst = jnp.concatenate([slot_major, jnp.where(is_pad, pad_pos, ends[-1] + spare)])
        xs = sc_scatter_rows(routed[3].reshape(t, _ROW_TILES, LANES), dst, t)
        xs = xs.reshape(n_rows * _ROW_TILES, LANES)
    else:
        xs = moe_dispatch(x, g, pos, jnp.zeros((n_rows * _ROW_TILES, LANES), F32), tm=tm)
    ys = moe_experts(xs, w_gate_up.reshape(depth, N_EXPERTS, d, 2 * D_EXPERT),
                     w_down.reshape(depth, N_EXPERTS, D_EXPERT, d), layer,
                     tile_expert, tile_valid, tmx=tmx)
    if use_sc:
        rows = sc_gather_rows(ys.reshape(n_rows, _ROW_TILES, LANES), slot_major)
        return moe_combine_dense(x, info, rows.reshape(2 * t * _ROW_TILES, LANES),
                                 g_final.reshape(1, d), tm=tm, final_norm=final_norm)
    return moe_combine(x, info, pos, ys, g_final.reshape(1, d), tm=tm, final_norm=final_norm)


def kernel(x_prompt, x_sample, state_conv, state_ssm, cache_kv_w128, cache_kv_w512, cache_kv_w2048, norm_mix, norm_ffn, norm_final, ssm_in_proj, ssm_conv_w, ssm_conv_b, ssm_dt_bias, ssm_a_log, ssm_d, ssm_norm, ssm_out_proj, attn_qkv, attn_out, moe_route_group, moe_route_expert, moe_w_gate_up, moe_w_down):
    nb, seq, d = x_prompt.shape
    ns = x_sample.shape[0]
    t = nb * seq
    tm = min(t, 1024)
    n_groups = len(ATTN_PATTERNS)
    kv_caches = (cache_kv_w128, cache_kv_w512, cache_kv_w2048)

    ssm_params = (ssm_in_proj[0], ssm_conv_w[0], ssm_conv_b[0], ssm_dt_bias[0], ssm_a_log[0],
                  ssm_d[0], ssm_norm[0], ssm_out_proj[0])
    xp, conv_p, ssm_p = _mamba_prompt(x_prompt, norm_mix[0], *ssm_params)
    xs, conv_s, ssm_s = _mamba_sample(x_sample, norm_mix[0], state_conv[0], state_ssm[0], *ssm_params)

    def moe(x2d, i, tm_moe, tmx, final_norm, route_dtype):
        return hier_moe_layer(x2d, norm_ffn[i], moe_route_group[i], moe_route_expert[i],
                              moe_w_gate_up, moe_w_down, i, norm_final,
                              tm=tm_moe, tmx=tmx, final_norm=final_norm, route_dtype=route_dtype)

    xp = moe(xp.reshape(t, d), 0, 256, 256, False, BF16)
    xs = moe(xs.reshape(ns, d), 0, ns, 16, False, F32)

    g1 = norm_mix[1].reshape(1, d)
    w_qkv = attn_qkv[0].astype(BF16)
    w_o = attn_out[0].astype(BF16)
    qkv_p = norm_matmul(xp, g1, w_qkv, tm=tm, tn=1024)
    y_p, kv_p = attn_prompt(qkv_p.reshape(nb, seq, -1), nb, seq)
    xp = matmul_res(y_p.reshape(t, ATTN_WIDTH), w_o, xp, tm=tm)

    qkv_s = norm_matmul(xs, g1, attn_qkv[0], tm=ns, tn=512)
    qkv_s5 = qkv_s.reshape(ns, n_groups, 3, ATTN_HEADS, ATTN_HEAD_DIM)
    y_s = attn_decode(qkv_s5, [c[0] for c in kv_caches])
    xs = matmul_res(y_s.reshape(ns, ATTN_WIDTH), attn_out[0], xs, tm=ns)
    kv_s = [qkv_s5[:, g, 1:3][:, None] for g in range(n_groups)]

    y_prompt = moe(xp, 1, 256, 256, True, BF16).reshape(nb, seq, d)
    y_sample = moe(xs, 1, ns, 16, True, F32).reshape(ns, 1, d)

    return (y_prompt, y_sample, conv_p[None], conv_s[None], ssm_p[None], ssm_s[None],
            kv_p[0][None], kv_s[0][None], kv_p[1][None], kv_s[1][None], kv_p[2][None], kv_s[2][None])
```
